```python
import jax, jax.numpy as jnp
from jax import lax
import numpy as np

D_MODEL = 2048
BATCH = 8
SEQ = 2048
DEPTH = 2

GRID_W = 64
CTX_LEN = 256
N_MIXERS = 2
N_SUB = 3
D_FF = 5632
MACARON = 0.5
LRU_WIDTH = D_MODEL
LRU_HEADS = 16
LRU_BLOCK = LRU_WIDTH // LRU_HEADS
CONV_W = 4
RG_C = 8.0
N_FGROUPS = 8
FGROUP = D_MODEL // N_FGROUPS
N_REC = (DEPTH + 1) // 2
N_FOU = DEPTH // 2
EPS = 1e-6

kernel_name = 'hybrid_rglru_fnet_macaron_dit'


def rmsnorm(x, g):
    xf = x.astype(jnp.float32)
    y = xf * lax.rsqrt(jnp.mean(xf * xf, axis=-1, keepdims=True) + EPS)
    return (y * g.astype(jnp.float32)).astype(x.dtype)


def modulate(h, shift, scale):
    return h * (1 + scale) + shift


def adaln(cvec, w, b):
    m = (jax.nn.silu(cvec) @ w + b)[:, None, :]
    return jnp.split(m, 3 * N_SUB, axis=-1)


def swiglu(h, w_in, w_out):
    gate, up = jnp.split(h @ w_in, 2, axis=-1)
    return (jax.nn.silu(gate) * up) @ w_out


def ffn_sublayer(x, g_pre, g_post, shift, scale, gate, w_in, w_out):
    y = swiglu(modulate(rmsnorm(x, g_pre), shift, scale), w_in, w_out)
    return x + MACARON * gate * rmsnorm(y, g_post)


def centred_dwconv(v, w, b, axis):
    n = v.shape[axis]
    left = CONV_W // 2
    pad = [(0, 0)] * v.ndim
    pad[axis] = (left, CONV_W - 1 - left)
    vp = jnp.pad(v, pad)
    out = b
    for k in range(CONV_W):
        out = out + w[k] * lax.slice_in_dim(vp, k, k + n, axis=axis)
    return out


def block_diag(v, w, b):
    bsz, n, r = v.shape
    y = jnp.einsum('blhi,hij->blhj', v.reshape(bsz, n, LRU_HEADS, LRU_BLOCK), w)
    return y.reshape(bsz, n, r) + b


def rglru_coeffs(v, wa, ba, wx, bx, lam):
    r = jax.nn.sigmoid(block_diag(v, wa, ba)).astype(jnp.float32)
    i = jax.nn.sigmoid(block_diag(v, wx, bx)).astype(jnp.float32)
    log_a = -RG_C * r * jax.nn.softplus(-lam.astype(jnp.float32))
    a = jnp.exp(log_a)
    mult = jnp.sqrt(-jnp.expm1(2.0 * log_a))
    return a, mult * i * v.astype(jnp.float32)


def linear_scan(a, b, h0, reverse):
    def comb(lhs, rhs):
        a1, b1 = lhs
        a2, b2 = rhs
        return a1 * a2, a2 * b1 + b2
    a_cum, b_cum = lax.associative_scan(comb, (a, b), axis=1, reverse=reverse)
    return a_cum * h0[:, None, :] + b_cum


def rglru_mixer(h_lat, h_ctx, rows, w_in, conv_w, conv_b, gate_w, gate_b, lam, w_out, need_ctx_out):
    bsz, n, _ = h_lat.shape
    g_lat, v_lat = jnp.split(h_lat @ w_in, 2, axis=-1)
    v_lat = centred_dwconv(v_lat.reshape(bsz, rows, GRID_W, LRU_WIDTH), conv_w, conv_b, axis=2)
    v_lat = v_lat.reshape(bsz, n, LRU_WIDTH)
    v_ctx = centred_dwconv(h_ctx @ w_in[:, LRU_WIDTH:], conv_w, conv_b, axis=1)
    zeros = jnp.zeros((bsz, LRU_WIDTH), jnp.float32)
    y_lat = jnp.zeros((bsz, n, LRU_WIDTH), jnp.float32)
    y_ctx = jnp.zeros(v_ctx.shape, jnp.float32)
    for d, reverse in enumerate((False, True)):
        a_c, b_c = rglru_coeffs(v_ctx, gate_w[d, 0], gate_b[d, 0], gate_w[d, 1], gate_b[d, 1], lam[d])
        h_c = linear_scan(a_c, b_c, zeros, reverse)
        h_end = h_c[:, 0] if reverse else h_c[:, -1]
        a_l, b_l = rglru_coeffs(v_lat, gate_w[d, 0], gate_b[d, 0], gate_w[d, 1], gate_b[d, 1], lam[d])
        y_lat = y_lat + linear_scan(a_l, b_l, h_end, reverse)
        if need_ctx_out:
            y_ctx = y_ctx + h_c
    out_lat = (jax.nn.gelu(g_lat) * y_lat.astype(h_lat.dtype)) @ w_out
    out_ctx = None
    if need_ctx_out:
        g_ctx = h_ctx @ w_in[:, :LRU_WIDTH]
        out_ctx = (jax.nn.gelu(g_ctx) * y_ctx.astype(h_ctx.dtype)) @ w_out
    return out_lat, out_ctx


def fourier_mixer(h, w_out):
    bsz, n, _ = h.shape
    hg = h.reshape(bsz, n, N_FGROUPS, FGROUP).astype(jnp.float32)
    f = jnp.fft.fftn(hg, axes=(1, 3), norm='ortho').real
    return f.reshape(bsz, n, D_MODEL).astype(h.dtype) @ w_out


def setup_inputs(seed: int = 0) -> dict:
    key = jax.random.key(seed)
    ks = jax.random.split(key, 20)
    f32 = jnp.float32
    x = jax.random.normal(ks[0], (BATCH, SEQ, D_MODEL), f32)
    c = jax.random.normal(ks[1], (BATCH, D_MODEL), f32)
    ctx = jax.random.normal(ks[2], (BATCH, CTX_LEN, D_MODEL), f32)
    c_ctx = jax.random.normal(ks[3], (D_MODEL,), f32)
    mod_w = jax.random.normal(ks[4], (DEPTH, D_MODEL, 3 * N_SUB * D_MODEL), f32) * (0.5 * D_MODEL ** -0.5)
    mod_b = jax.random.normal(ks[5], (DEPTH, 3 * N_SUB * D_MODEL), f32) * 0.01
    norm_g = 1.0 + 0.01 * jax.random.normal(ks[6], (DEPTH, 2 * N_SUB, D_MODEL), f32)
    ffn_w_in = jax.random.normal(ks[7], (DEPTH, 2, D_MODEL, 2 * D_FF), f32) * D_MODEL ** -0.5
    ffn_w_out = jax.random.normal(ks[8], (DEPTH, 2, D_FF, D_MODEL), f32) * D_FF ** -0.5
    rec_w_in = jax.random.normal(ks[9], (N_REC, D_MODEL, 2 * LRU_WIDTH), f32) * D_MODEL ** -0.5
    rec_conv_w = jax.random.normal(ks[10], (N_REC, CONV_W, LRU_WIDTH), f32) * CONV_W ** -0.5
    rec_conv_b = jax.random.normal(ks[11], (N_REC, LRU_WIDTH), f32) * 0.01
    rec_gate_w = jax.random.normal(ks[12], (N_REC, 2, 2, LRU_HEADS, LRU_BLOCK, LRU_BLOCK), f32) * LRU_BLOCK ** -0.5
    rec_gate_b = jax.random.normal(ks[13], (N_REC, 2, 2, LRU_WIDTH), f32) * 0.01
    a_target = jax.random.uniform(ks[14], (N_REC, 2, LRU_WIDTH), f32, minval=0.9, maxval=0.999)
    s = a_target ** (1.0 / RG_C)
    rec_lam = jnp.log(s) - jnp.log1p(-s)
    rec_w_out = jax.random.normal(ks[15], (N_REC, LRU_WIDTH, D_MODEL), f32) * LRU_WIDTH ** -0.5
    fou_w_out = jax.random.normal(ks[16], (N_FOU, D_MODEL, D_MODEL), f32) * D_MODEL ** -0.5
    return {'x': x, 'c': c, 'ctx': ctx, 'c_ctx': c_ctx, 'mod_w': mod_w, 'mod_b': mod_b,
            'norm_g': norm_g, 'ffn_w_in': ffn_w_in, 'ffn_w_out': ffn_w_out,
            'rec_w_in': rec_w_in, 'rec_conv_w': rec_conv_w, 'rec_conv_b': rec_conv_b,
            'rec_gate_w': rec_gate_w, 'rec_gate_b': rec_gate_b, 'rec_lam': rec_lam,
            'rec_w_out': rec_w_out, 'fou_w_out': fou_w_out}


def reference(x, c, ctx, c_ctx, mod_w, mod_b, norm_g, ffn_w_in, ffn_w_out, rec_w_in, rec_conv_w,
              rec_conv_b, rec_gate_w, rec_gate_b, rec_lam, rec_w_out, fou_w_out):
    rows = x.shape[1] // GRID_W
    last_rec = ((DEPTH - 1) // N_MIXERS) * N_MIXERS
    for i in range(DEPTH):
        is_rec = (i % N_MIXERS) == 0
        j = i // N_MIXERS
        ctx_used = i <= last_rec
        ctx_full = i < last_rec
        g = norm_g[i]
        sh1, sc1, gt1, sh2, sc2, gt2, sh3, sc3, gt3 = adaln(c, mod_w[i], mod_b[i])
        x = ffn_sublayer(x, g[0], g[1], sh1, sc1, gt1, ffn_w_in[i, 0], ffn_w_out[i, 0])
        hc = None
        if ctx_used:
            m_ctx = adaln(c_ctx[None], mod_w[i], mod_b[i])
            ctx = ffn_sublayer(ctx, g[0], g[1], m_ctx[0], m_ctx[1], m_ctx[2], ffn_w_in[i, 0], ffn_w_out[i, 0])
            hc = modulate(rmsnorm(ctx, g[2]), m_ctx[3], m_ctx[4])
        hx = modulate(rmsnorm(x, g[2]), sh2, sc2)
        if is_rec:
            y, yc = rglru_mixer(hx, hc, rows, rec_w_in[j], rec_conv_w[j], rec_conv_b[j], rec_gate_w[j],
                                rec_gate_b[j], rec_lam[j], rec_w_out[j], ctx_full)
        else:
            y = fourier_mixer(hx, fou_w_out[j])
            yc = fourier_mixer(hc, fou_w_out[j]) if ctx_full else None
        x = x + gt2 * rmsnorm(y, g[3])
        x = ffn_sublayer(x, g[4], g[5], sh3, sc3, gt3, ffn_w_in[i, 1], ffn_w_out[i, 1])
        if ctx_full:
            ctx = ctx + m_ctx[5] * rmsnorm(yc, g[3])
            ctx = ffn_sublayer(ctx, g[4], g[5], m_ctx[6], m_ctx[7], m_ctx[8], ffn_w_in[i, 1], ffn_w_out[i, 1])
    return x
```

```python
import functools
import math

import jax
import jax.numpy as jnp
import numpy as np
from jax import lax
from jax.experimental import pallas as pl
from jax.experimental.pallas import tpu as pltpu

N_SUB = 3
N_MOD = 3 * N_SUB
MACARON = 0.5
GRID_W = 64
CONV_W = 4
CONV_LEFT = CONV_W // 2
LRU_BLOCK = 128
RG_C = 8.0
FGROUP = 256
EPS = 1e-6

V7X_SUBLANES = 8
V7X_LANES = 128
MIB = 1024 * 1024

_BF16 = jnp.bfloat16
_F32 = jnp.float32


def _params(semantics, vmem_mib):
    return pltpu.CompilerParams(dimension_semantics=semantics,
                                vmem_limit_bytes=vmem_mib * MIB)


def _rms(x, g):
    return x * lax.rsqrt(jnp.mean(x * x, axis=-1, keepdims=True) + EPS) * g


def _norm_mod(x, g, shift, scale):
    return _rms(x, g) * (1.0 + scale) + shift


def _dot(a, b):
    return jnp.dot(a, b, preferred_element_type=_F32)


def _adaln_kernel(c_ref, w_ref, b_ref, o_ref):
    c = c_ref[...]
    a = (c * jax.nn.sigmoid(c)).astype(_BF16)
    o_ref[0] = _dot(a, w_ref[0].astype(_BF16)) + b_ref[0]


def _adaln(cc, mod_w, mod_b, tn=1024):
    depth, d, n = mod_w.shape
    rows = cc.shape[0]
    return pl.pallas_call(
        _adaln_kernel,
        grid=(depth, n // tn),
        in_specs=[pl.BlockSpec((rows, d), lambda l, j: (0, 0)),
                  pl.BlockSpec((1, d, tn), lambda l, j: (l, 0, j)),
                  pl.BlockSpec((1, 1, tn), lambda l, j: (l, 0, j))],
        out_specs=pl.BlockSpec((1, rows, tn), lambda l, j: (l, 0, j)),
        out_shape=jax.ShapeDtypeStruct((depth, rows, n), _F32),
        compiler_params=_params(("arbitrary", "arbitrary"), 40),
        name="adaln",
    )(cc, mod_w, mod_b.reshape(depth, 1, n))


def _ffn_kernel(x_ref, sh_ref, sc_ref, gt_ref, gpre_ref, gpost_ref, wg_ref, wu_ref, wo_ref,
                o_ref, h_ref, *, rc):
    j = pl.program_id(2)
    nj = pl.num_programs(2)
    tm = h_ref.shape[0]

    @pl.when(j == 0)
    def _():
        h_ref[...] = _norm_mod(x_ref[0], gpre_ref[...], sh_ref[0], sc_ref[0]).astype(_BF16)

    def chunk(r, carry):
        rows = pl.ds(pl.multiple_of(r * rc, rc), rc)
        h = h_ref[rows, :]
        g = _dot(h, wg_ref[...])
        u = _dot(h, wu_ref[...])
        a = (g * jax.nn.sigmoid(g) * u).astype(_BF16)
        y = _dot(a, wo_ref[...])

        @pl.when(j == 0)
        def _():
            o_ref[0, rows, :] = y

        @pl.when(j > 0)
        def _():
            o_ref[0, rows, :] += y
        return carry

    lax.fori_loop(0, tm // rc, chunk, 0)

    @pl.when(j == nj - 1)
    def _():
        y = o_ref[0]
        o_ref[0] = x_ref[0] + (MACARON * gt_ref[0]) * _rms(y, gpost_ref[...])


def _ffn(x, mods, row_of, k0, g_pre, g_post, w_in, w_out, tm=1024, tf=512, rc=256):
    nb, L, d = x.shape
    f = w_out.shape[0]
    nf = f // tf
    mod_spec = lambda k: pl.BlockSpec((1, 1, d), lambda b, i, j: (row_of(b), 0, k))
    vec_spec = pl.BlockSpec((1, d), lambda b, i, j: (0, 0))
    return pl.pallas_call(
        functools.partial(_ffn_kernel, rc=rc),
        grid=(nb, L // tm, nf),
        in_specs=[pl.BlockSpec((1, tm, d), lambda b, i, j: (b, i, 0)),
                  mod_spec(k0), mod_spec(k0 + 1), mod_spec(k0 + 2),
                  vec_spec, vec_spec,
                  pl.BlockSpec((d, tf), lambda b, i, j: (0, j)),
                  pl.BlockSpec((d, tf), lambda b, i, j: (0, nf + j)),
                  pl.BlockSpec((tf, d), lambda b, i, j: (j, 0))],
        out_specs=pl.BlockSpec((1, tm, d), lambda b, i, j: (b, i, 0)),
        out_shape=jax.ShapeDtypeStruct((nb, L, d), _F32),
        scratch_shapes=[pltpu.VMEM((tm, d), _BF16)],
        compiler_params=_params(("arbitrary", "arbitrary", "arbitrary"), 60),
        name="ffn",
    )(x, mods, mods, mods, g_pre.reshape(1, d), g_post.reshape(1, d), w_in, w_in, w_out)


def _conv_rows(v, cw, cb, blocklen):
    tm = v.shape[0]
    pos = lax.broadcasted_iota(jnp.int32, (tm, 1), 0) % blocklen
    out = cb + cw[CONV_LEFT:CONV_LEFT + 1] * v
    for k in range(CONV_W):
        off = k - CONV_LEFT
        if off == 0:
            continue
        shifted = pltpu.roll(v, (-off) % tm, axis=0)
        valid = (pos + off >= 0) & (pos + off < blocklen)
        out = out + cw[k:k + 1] * jnp.where(valid, shifted, 0.0)
    return out


def _recin_kernel(x_ref, sh_ref, sc_ref, g_ref, wg_ref, wv_ref, cw_ref, cb_ref,
                  og_ref, ov_ref, h_ref, *, blocklen):
    n = pl.program_id(2)

    @pl.when(n == 0)
    def _():
        h_ref[...] = _norm_mod(x_ref[0], g_ref[...], sh_ref[0], sc_ref[0]).astype(_BF16)

    h = h_ref[...]
    if og_ref is not None:
        og_ref[0] = _dot(h, wg_ref[...]).astype(_BF16)
    v = _dot(h, wv_ref[...])
    ov_ref[0] = _conv_rows(v, cw_ref[...], cb_ref[...], blocklen).astype(_BF16)


def _recin_nogate_kernel(x_ref, sh_ref, sc_ref, g_ref, wv_ref, cw_ref, cb_ref, ov_ref, h_ref,
                         *, blocklen):
    _recin_kernel(x_ref, sh_ref, sc_ref, g_ref, None, wv_ref, cw_ref, cb_ref, None, ov_ref,
                  h_ref, blocklen=blocklen)


def _recin(x, mods, row_of, k0, g, w_in, conv_w, conv_b, tm, blocklen, need_gate, tn=512):
    nb, L, d = x.shape
    r = w_in.shape[1] // 2
    nn = r // tn
    mod_spec = lambda k: pl.BlockSpec((1, 1, d), lambda b, i, n: (row_of(b), 0, k))
    x_spec = pl.BlockSpec((1, tm, d), lambda b, i, n: (b, i, 0))
    vec_spec = pl.BlockSpec((1, d), lambda b, i, n: (0, 0))
    wg_spec = pl.BlockSpec((d, tn), lambda b, i, n: (0, n))
    wv_spec = pl.BlockSpec((d, tn), lambda b, i, n: (0, nn + n))
    cw_spec = pl.BlockSpec((CONV_W, tn), lambda b, i, n: (0, n))
    cb_spec = pl.BlockSpec((1, tn), lambda b, i, n: (0, n))
    o_spec = pl.BlockSpec((1, tm, tn), lambda b, i, n: (b, i, n))
    o_shape = jax.ShapeDtypeStruct((nb, L, r), _BF16)
    common = dict(grid=(nb, L // tm, nn),
                  scratch_shapes=[pltpu.VMEM((tm, d), _BF16)],
                  compiler_params=_params(("arbitrary", "arbitrary", "arbitrary"), 48))
    g2 = g.reshape(1, d)
    cb2 = conv_b.reshape(1, r)
    if need_gate:
        return pl.pallas_call(
            functools.partial(_recin_kernel, blocklen=blocklen),
            in_specs=[x_spec, mod_spec(k0), mod_spec(k0 + 1), vec_spec, wg_spec, wv_spec,
                      cw_spec, cb_spec],
            out_specs=[o_spec, o_spec], out_shape=[o_shape, o_shape],
            name="rec_in", **common,
        )(x, mods, mods, g2, w_in, w_in, conv_w, cb2)
    return None, pl.pallas_call(
        functools.partial(_recin_nogate_kernel, blocklen=blocklen),
        in_specs=[x_spec, mod_spec(k0), mod_spec(k0 + 1), vec_spec, wv_spec, cw_spec, cb_spec],
        out_specs=o_spec, out_shape=o_shape,
        name="rec_in_ctx", **common,
    )(x, mods, mods, g2, w_in, conv_w, cb2)


def _gelu_tanh(x):
    return 0.5 * x * (1.0 + jnp.tanh(math.sqrt(2.0 / math.pi) * (x + 0.044715 * (x * x * x))))


def _scan_kernel(v_ref, vc_ref, g_ref, w_ref, bias_ref, lam_ref, o_ref, a_sc, b_sc, y_sc,
                 *, nh, tc):
    lc = vc_ref.shape[1]
    s = v_ref.shape[1]
    cc = v_ref.shape[2]
    sub = V7X_SUBLANES
    neg_lam = -lam_ref[...]
    sp = jnp.maximum(neg_lam, 0.0) + jnp.log(1.0 + jnp.exp(-jnp.abs(neg_lam)))

    def coeffs(src_ref, src0, dst0):
        rows = pl.ds(src0, tc)
        drows = pl.ds(dst0, tc)
        for hh in range(nh):
            lanes = slice(hh * LRU_BLOCK, (hh + 1) * LRU_BLOCK)
            vb = src_ref[0, rows, lanes]
            pre = _dot(vb, w_ref[hh])
            vf = vb.astype(_F32)
            for d in range(2):
                c0 = (2 * d) * LRU_BLOCK
                pa = pre[:, c0:c0 + LRU_BLOCK] + bias_ref[2 * d:2 * d + 1, lanes]
                px = pre[:, c0 + LRU_BLOCK:c0 + 2 * LRU_BLOCK] + bias_ref[2 * d + 1:2 * d + 2, lanes]
                r = jax.nn.sigmoid(pa)
                i = jax.nn.sigmoid(px)
                log_a = (-RG_C) * r * sp[d:d + 1, lanes]
                a = jnp.exp(log_a)
                mult = jnp.sqrt(1.0 - a * a)
                a_sc[d, drows, lanes] = a
                b_sc[d, drows, lanes] = mult * i * vf

    def coeffs_ctx(c, carry):
        off = pl.multiple_of(c * tc, tc)
        coeffs(vc_ref, off, off)
        return carry

    def coeffs_lat(c, carry):
        off = pl.multiple_of(c * tc, tc)
        coeffs(v_ref, off, lc + off)
        return carry

    lax.fori_loop(0, lc // tc, coeffs_ctx, 0)
    lax.fori_loop(0, s // tc, coeffs_lat, 0)

    row = lax.broadcasted_iota(jnp.int32, (sub, cc), 0)

    def local_scan(a, b, reverse):
        for k in (1, 2, 4):
            if reverse:
                keep = row < sub - k
                shift = sub - k
            else:
                keep = row >= k
                shift = k
            a_sh = jnp.where(keep, pltpu.roll(a, shift, axis=0), 1.0)
            b_sh = jnp.where(keep, pltpu.roll(b, shift, axis=0), 0.0)
            b = a * b_sh + b
            a = a * a_sh
        return a, b

    def step(d, jrow, h_in, reverse):
        rows = pl.ds(pl.multiple_of(jrow * sub, sub), sub)
        a, b = local_scan(a_sc[d, rows, :], b_sc[d, rows, :], reverse)
        h = a * h_in + b
        last = h[0:1, :] if reverse else h[sub - 1:sub, :]
        return rows, h, jnp.broadcast_to(last, (sub, cc))

    def fwd(jrow, h_in):
        rows, h, h_out = step(0, jrow, h_in, False)
        y_sc[rows, :] = h
        return h_out

    h0 = jnp.zeros((sub, cc), _F32)
    lax.fori_loop(0, (lc + s) // sub, fwd, h0, unroll=4)

    def bwd_ctx(jj, h_in):
        _, _, h_out = step(1, lc // sub - 1 - jj, h_in, True)
        return h_out

    h_end = lax.fori_loop(0, lc // sub, bwd_ctx, h0, unroll=4)

    def bwd_lat(jj, h_in):
        jrow = (lc + s) // sub - 1 - jj
        rows, h, h_out = step(1, jrow, h_in, True)
        y_sc[rows, :] = y_sc[rows, :] + h
        return h_out

    lax.fori_loop(0, s // sub, bwd_lat, h_end, unroll=4)

    def gate_out(c, carry):
        rows = pl.ds(pl.multiple_of(c * tc, tc), tc)
        yrows = pl.ds(pl.multiple_of(lc + c * tc, tc), tc)
        g = g_ref[0, rows, :].astype(_F32)
        o_ref[0, rows, :] = (_gelu_tanh(g) * y_sc[yrows, :]).astype(_BF16)
        return carry

    lax.fori_loop(0, s // tc, gate_out, 0)


def _scan(v_lat, v_ctx, g_lat, w_cat, bias, lam, cc=512, tc=256):
    nb, s, r = v_lat.shape
    lc = v_ctx.shape[1]
    nh = cc // LRU_BLOCK
    blk = lambda L: pl.BlockSpec((1, L, cc), lambda b, c: (b, 0, c))
    return pl.pallas_call(
        functools.partial(_scan_kernel, nh=nh, tc=tc),
        grid=(nb, r // cc),
        in_specs=[blk(s), blk(lc), blk(s),
                  pl.BlockSpec((nh, LRU_BLOCK, 4 * LRU_BLOCK), lambda b, c: (c, 0, 0)),
                  pl.BlockSpec((4, cc), lambda b, c: (0, c)),
                  pl.BlockSpec((2, cc), lambda b, c: (0, c))],
        out_specs=blk(s),
        out_shape=jax.ShapeDtypeStruct((nb, s, r), _BF16),
        scratch_shapes=[pltpu.VMEM((2, lc + s, cc), _F32),
                        pltpu.VMEM((2, lc + s, cc), _F32),
                        pltpu.VMEM((lc + s, cc), _F32)],
        compiler_params=_params(("arbitrary", "arbitrary"), 48),
        name="rglru_scan",
    )(v_lat, v_ctx, g_lat, w_cat, bias, lam)


def _outproj_kernel(x_ref, a_ref, gt_ref, g_ref, w_ref, o_ref):
    y = _dot(a_ref[0], w_ref[...])
    o_ref[0] = x_ref[0] + gt_ref[0] * _rms(y, g_ref[...])


def _outproj(x, a, mods, k_gate, g, w, tm=512):
    nb, L, d = x.shape
    r = a.shape[2]
    return pl.pallas_call(
        _outproj_kernel,
        grid=(nb, L // tm),
        in_specs=[pl.BlockSpec((1, tm, d), lambda b, i: (b, i, 0)),
                  pl.BlockSpec((1, tm, r), lambda b, i: (b, i, 0)),
                  pl.BlockSpec((1, 1, d), lambda b, i: (b, 0, k_gate)),
                  pl.BlockSpec((1, d), lambda b, i: (0, 0)),
                  pl.BlockSpec((r, d), lambda b, i: (0, 0))],
        out_specs=pl.BlockSpec((1, tm, d), lambda b, i: (b, i, 0)),
        out_shape=jax.ShapeDtypeStruct((nb, L, d), _F32),
        compiler_params=_params(("arbitrary", "arbitrary"), 48),
        name="mixer_out",
    )(x, a, mods, g.reshape(1, d), w)


def _dft_tables(n):
    k = np.arange(n, dtype=np.int64)
    ang = 2.0 * np.pi * ((k[:, None] * k[None, :]) % n).astype(np.float64) / n
    return np.cos(ang), np.sin(ang)


def _chdft_kernel(x_ref, sh_ref, sc_ref, g_ref, cs_ref, o_ref):
    h = _norm_mod(x_ref[0], g_ref[...], sh_ref[0], sc_ref[0]).astype(_BF16)
    d = h.shape[1]
    for gi in range(d // FGROUP):
        lanes = slice(gi * FGROUP, (gi + 1) * FGROUP)
        t = _dot(h[:, lanes], cs_ref[...])
        o_ref[0, 0, :, lanes] = t[:, :FGROUP].astype(_BF16)
        o_ref[0, 1, :, lanes] = t[:, FGROUP:].astype(_BF16)


def _chdft(x, mods, k0, g, cs, tm=512):
    nb, L, d = x.shape
    mod_spec = lambda k: pl.BlockSpec((1, 1, d), lambda b, i: (b, 0, k))
    return pl.pallas_call(
        _chdft_kernel,
        grid=(nb, L // tm),
        in_specs=[pl.BlockSpec((1, tm, d), lambda b, i: (b, i, 0)),
                  mod_spec(k0), mod_spec(k0 + 1),
                  pl.BlockSpec((1, d), lambda b, i: (0, 0)),
                  pl.BlockSpec((FGROUP, 2 * FGROUP), lambda b, i: (0, 0))],
        out_specs=pl.BlockSpec((1, 2, tm, d), lambda b, i: (b, 0, i, 0)),
        out_shape=jax.ShapeDtypeStruct((nb, 2, L, d), _BF16),
        compiler_params=_params(("arbitrary", "arbitrary"), 40),
        name="fourier_channel_dft",
    )(x, mods, mods, g.reshape(1, d), cs)


def _seqdft_kernel(a_ref, b_ref, o_ref, *, scale):
    o_ref[0] = (_dot(a_ref[...], b_ref[0]) * scale).astype(_BF16)


def _seqdft(seq_tab, xcs, scale, tm=1024, tn=1024):
    nb, k2, d = xcs.shape
    L = seq_tab.shape[0]
    return pl.pallas_call(
        functools.partial(_seqdft_kernel, scale=scale),
        grid=(nb, d // tn, L // tm),
        in_specs=[pl.BlockSpec((tm, k2), lambda b, n, i: (i, 0)),
                  pl.BlockSpec((1, k2, tn), lambda b, n, i: (b, 0, n))],
        out_specs=pl.BlockSpec((1, tm, tn), lambda b, n, i: (b, i, n)),
        out_shape=jax.ShapeDtypeStruct((nb, L, d), _BF16),
        compiler_params=_params(("arbitrary", "arbitrary", "arbitrary"), 48),
        name="fourier_seq_dft",
    )(seq_tab, xcs)


def kernel(x, c, ctx, c_ctx, mod_w, mod_b, norm_g, ffn_w_in, ffn_w_out, rec_w_in, rec_conv_w,
           rec_conv_b, rec_gate_w, rec_gate_b, rec_lam, rec_w_out, fou_w_out):
    nb, s, d = x.shape
    lc = ctx.shape[1]
    depth = mod_w.shape[0]
    n_mixers = 2
    last_rec = ((depth - 1) // n_mixers) * n_mixers
    assert last_rec == 0, "context stream is only carried up to the first RG-LRU mixer"

    rows = 2 * V7X_SUBLANES
    cc = jnp.concatenate([c, c_ctx[None], jnp.zeros((rows - nb - 1, d), _F32)], axis=0)
    mods_all = _adaln(cc, mod_w, mod_b).reshape(depth, rows, 1, N_MOD * d)

    w_in_bf = ffn_w_in.astype(_BF16)
    w_out_bf = ffn_w_out.astype(_BF16)
    lat_row = lambda b: b
    ctx_row = lambda b: nb

    for i in range(depth):
        is_rec = (i % n_mixers) == 0
        j = i // n_mixers
        mods = mods_all[i]
        g = norm_g[i]
        x = _ffn(x, mods, lat_row, 0, g[0], g[1], w_in_bf[i, 0], w_out_bf[i, 0])
        if is_rec:
            ctx1 = _ffn(ctx.reshape(1, nb * lc, d), mods, ctx_row, 0, g[0], g[1],
                        w_in_bf[i, 0], w_out_bf[i, 0]).reshape(nb, lc, d)
            w_rec = rec_w_in[j].astype(_BF16)
            g_lat, v_lat = _recin(x, mods, lat_row, 3, g[2], w_rec, rec_conv_w[j], rec_conv_b[j],
                                  tm=1024, blocklen=GRID_W, need_gate=True)
            _, v_ctx = _recin(ctx1, mods, ctx_row, 3, g[2], w_rec, rec_conv_w[j], rec_conv_b[j],
                              tm=lc, blocklen=lc, need_gate=False)
            gw = rec_gate_w[j]
            nh = gw.shape[2]
            w_cat = jnp.transpose(gw, (2, 3, 0, 1, 4)).reshape(nh, LRU_BLOCK, 4 * LRU_BLOCK)
            a = _scan(v_lat, v_ctx, g_lat, w_cat.astype(_BF16),
                      rec_gate_b[j].reshape(4, -1), rec_lam[j])
            w_mix = rec_w_out[j].astype(_BF16)
        else:
            cos_c, sin_c = _dft_tables(FGROUP)
            cs = jnp.asarray(np.concatenate([cos_c, sin_c], axis=1), _BF16)
            cos_s, sin_s = _dft_tables(s)
            seq_tab = jnp.asarray(np.concatenate([cos_s, -sin_s], axis=1), _BF16)
            xcs = _chdft(x, mods, 3, g[2], cs).reshape(nb, 2 * s, d)
            a = _seqdft(seq_tab, xcs, 1.0 / math.sqrt(s * FGROUP))
            w_mix = fou_w_out[j].astype(_BF16)
        x = _outproj(x, a, mods, 5, g[3], w_mix)
        x = _ffn(x, mods, lat_row, 6, g[4], g[5], w_in_bf[i, 1], w_out_bf[i, 1])
    return x
```

```python
import functools
import math

import jax
import jax.numpy as jnp
import numpy as np
from jax import lax
from jax.experimental import pallas as pl
from jax.experimental.pallas import tpu as pltpu

N_SUB = 3
N_MOD = 3 * N_SUB
MACARON = 0.5
GRID_W = 64
CONV_W = 4
CONV_LEFT = CONV_W // 2
LRU_BLOCK = 128
RG_C = 8.0
FGROUP = 256
EPS = 1e-6

V7X_SUBLANES = 8
ROW_CHUNK = 64
REC_ROW_CHUNK = 256
V7X_LANES = 128
MIB = 1024 * 1024

_BF16 = jnp.bfloat16
_F32 = jnp.float32


def _params(semantics, vmem_mib):
    return pltpu.CompilerParams(dimension_semantics=semantics,
                                vmem_limit_bytes=vmem_mib * MIB)


def _rms(x, g):
    return x * lax.rsqrt(jnp.mean(x * x, axis=-1, keepdims=True) + EPS) * g


def _norm_mod(x, g, shift, scale):
    return _rms(x, g) * (1.0 + scale) + shift


def _dot(a, b):
    return jnp.dot(a, b, preferred_element_type=_F32)


def _for_row_chunks(n_rows, fn, rn=ROW_CHUNK, unroll=2):
    def body(r, carry):
        fn(pl.ds(pl.multiple_of(r * rn, rn), rn))
        return carry
    lax.fori_loop(0, n_rows // rn, body, 0, unroll=unroll)


def _inv_rms(x):
    return lax.rsqrt(jnp.mean(x * x, axis=-1, keepdims=True) + EPS)


def _norm_mod_into(x_ref, h_ref, g, shift, scale):
    gm = g * (1.0 + scale)

    def fn(rows):
        inv = _inv_rms(x_ref[rows, :])
        h_ref[rows, :] = (x_ref[rows, :] * inv * gm + shift).astype(h_ref.dtype)
    _for_row_chunks(x_ref.shape[0], fn)


def _adaln_kernel(c_ref, w_ref, b_ref, o_ref):
    c = c_ref[...]
    a = (c * jax.nn.sigmoid(c)).astype(_BF16)
    o_ref[0] = _dot(a, w_ref[0].astype(_BF16)) + b_ref[0]


def _adaln(cc, mod_w, mod_b, tn=1024):
    depth, d, n = mod_w.shape
    rows = cc.shape[0]
    return pl.pallas_call(
        _adaln_kernel,
        grid=(depth, n // tn),
        in_specs=[pl.BlockSpec((rows, d), lambda l, j: (0, 0)),
                  pl.BlockSpec((1, d, tn), lambda l, j: (l, 0, j)),
                  pl.BlockSpec((1, 1, tn), lambda l, j: (l, 0, j))],
        out_specs=pl.BlockSpec((1, rows, tn), lambda l, j: (l, 0, j)),
        out_shape=jax.ShapeDtypeStruct((depth, rows, n), _F32),
        compiler_params=_params(("arbitrary", "arbitrary"), 40),
        name="adaln",
    )(cc, mod_w, mod_b.reshape(depth, 1, n))


def _ffn_kernel(x_ref, sh_ref, sc_ref, gt_ref, gpre_ref, gpost_ref, wg_ref, wu_ref, wo_ref,
                o_ref, h_ref, *, rc):
    j = pl.program_id(2)
    nj = pl.num_programs(2)
    tm = h_ref.shape[0]
    x2 = x_ref.at[0]
    o2 = o_ref.at[0]

    @pl.when(j == 0)
    def _():
        _norm_mod_into(x2, h_ref, gpre_ref[...], sh_ref[0], sc_ref[0])
        o2[...] = jnp.zeros(o2.shape, _F32)

    for r in range(tm // rc):
        rows = slice(r * rc, (r + 1) * rc)
        h = h_ref[rows, :]
        g = _dot(h, wg_ref[...])
        u = _dot(h, wu_ref[...])
        a = (g * jax.nn.sigmoid(g) * u).astype(_BF16)
        o2[rows, :] += _dot(a, wo_ref[...])

    @pl.when(j == nj - 1)
    def _():
        gg = (MACARON * gt_ref[0]) * gpost_ref[...]

        def fn(rows):
            inv = _inv_rms(o2[rows, :])
            o2[rows, :] = x2[rows, :] + (o2[rows, :] * inv) * gg
        _for_row_chunks(tm, fn)


def _ffn(x, mods, row_of, k0, g_pre, g_post, w_in, w_out, layer, sub, tm=1024, tf=512, rc=256):
    nb, L, d = x.shape
    f = w_out.shape[2]
    nf = f // tf
    mod_spec = lambda k: pl.BlockSpec((1, 1, d), lambda b, i, j: (row_of(b), 0, k))
    vec_spec = pl.BlockSpec((1, d), lambda b, i, j: (0, 0))
    return pl.pallas_call(
        functools.partial(_ffn_kernel, rc=rc),
        grid=(nb, L // tm, nf),
        in_specs=[pl.BlockSpec((1, tm, d), lambda b, i, j: (b, i, 0)),
                  mod_spec(k0), mod_spec(k0 + 1), mod_spec(k0 + 2),
                  vec_spec, vec_spec,
                  pl.BlockSpec((None, None, d, tf), lambda b, i, j: (layer, sub, 0, j)),
                  pl.BlockSpec((None, None, d, tf), lambda b, i, j: (layer, sub, 0, nf + j)),
                  pl.BlockSpec((None, None, tf, d), lambda b, i, j: (layer, sub, j, 0))],
        out_specs=pl.BlockSpec((1, tm, d), lambda b, i, j: (b, i, 0)),
        out_shape=jax.ShapeDtypeStruct((nb, L, d), _F32),
        scratch_shapes=[pltpu.VMEM((tm, d), _BF16)],
        compiler_params=_params(("arbitrary", "arbitrary", "arbitrary"), 60),
        name="ffn",
    )(x, mods, mods, mods, g_pre.reshape(1, d), g_post.reshape(1, d), w_in, w_in, w_out)


def _conv_rows(v, cw, cb, blocklen):
    tm = v.shape[0]
    pos = lax.broadcasted_iota(jnp.int32, (tm, 1), 0) % blocklen
    out = cb + cw[CONV_LEFT:CONV_LEFT + 1] * v
    for k in range(CONV_W):
        off = k - CONV_LEFT
        if off == 0:
            continue
        shifted = pltpu.roll(v, (-off) % tm, axis=0)
        valid = (pos + off >= 0) & (pos + off < blocklen)
        out = out + cw[k:k + 1] * jnp.where(valid, shifted, 0.0)
    return out


def _recin_kernel(x_ref, sh_ref, sc_ref, g_ref, wg_ref, wv_ref, cw_ref, cb_ref,
                  og_ref, ov_ref, h_ref, *, blocklen):
    n = pl.program_id(2)

    tm = h_ref.shape[0]
    rc = min(tm, REC_ROW_CHUNK)

    @pl.when(n == 0)
    def _():
        _norm_mod_into(x_ref.at[0], h_ref, g_ref[...], sh_ref[0], sc_ref[0])

    for r in range(tm // rc):
        rows = slice(r * rc, (r + 1) * rc)
        h = h_ref[rows, :]
        if og_ref is not None:
            og_ref[0, rows, :] = _dot(h, wg_ref[...]).astype(_BF16)
        v = _dot(h, wv_ref[...])
        ov_ref[0, rows, :] = _conv_rows(v, cw_ref[...], cb_ref[...], blocklen).astype(_BF16)


def _recin_nogate_kernel(x_ref, sh_ref, sc_ref, g_ref, wv_ref, cw_ref, cb_ref, ov_ref, h_ref,
                         *, blocklen):
    _recin_kernel(x_ref, sh_ref, sc_ref, g_ref, None, wv_ref, cw_ref, cb_ref, None, ov_ref,
                  h_ref, blocklen=blocklen)


def _recin(x, mods, row_of, k0, g, w_in, conv_w, conv_b, tm, blocklen, need_gate, tn=512):
    nb, L, d = x.shape
    r = w_in.shape[1] // 2
    nn = r // tn
    mod_spec = lambda k: pl.BlockSpec((1, 1, d), lambda b, i, n: (row_of(b), 0, k))
    x_spec = pl.BlockSpec((1, tm, d), lambda b, i, n: (b, i, 0))
    vec_spec = pl.BlockSpec((1, d), lambda b, i, n: (0, 0))
    wg_spec = pl.BlockSpec((d, tn), lambda b, i, n: (0, n))
    wv_spec = pl.BlockSpec((d, tn), lambda b, i, n: (0, nn + n))
    cw_spec = pl.BlockSpec((CONV_W, tn), lambda b, i, n: (0, n))
    cb_spec = pl.BlockSpec((1, tn), lambda b, i, n: (0, n))
    o_spec = pl.BlockSpec((1, tm, tn), lambda b, i, n: (b, i, n))
    o_shape = jax.ShapeDtypeStruct((nb, L, r), _BF16)
    common = dict(grid=(nb, L // tm, nn),
                  scratch_shapes=[pltpu.VMEM((tm, d), _BF16)],
                  compiler_params=_params(("arbitrary", "arbitrary", "arbitrary"), 48))
    g2 = g.reshape(1, d)
    cb2 = conv_b.reshape(1, r)
    if need_gate:
        return pl.pallas_call(
            functools.partial(_recin_kernel, blocklen=blocklen),
            in_specs=[x_spec, mod_spec(k0), mod_spec(k0 + 1), vec_spec, wg_spec, wv_spec,
                      cw_spec, cb_spec],
            out_specs=[o_spec, o_spec], out_shape=[o_shape, o_shape],
            name="rec_in", **common,
        )(x, mods, mods, g2, w_in, w_in, conv_w, cb2)
    return None, pl.pallas_call(
        functools.partial(_recin_nogate_kernel, blocklen=blocklen),
        in_specs=[x_spec, mod_spec(k0), mod_spec(k0 + 1), vec_spec, wv_spec, cw_spec, cb_spec],
        out_specs=o_spec, out_shape=o_shape,
        name="rec_in_ctx", **common,
    )(x, mods, mods, g2, w_in, conv_w, cb2)


def _gelu_tanh(x):
    return 0.5 * x * (1.0 + jnp.tanh(math.sqrt(2.0 / math.pi) * (x + 0.044715 * (x * x * x))))


def _scan_kernel(v_ref, vc_ref, g_ref, w_ref, bias_ref, lam_ref, o_ref, a_sc, b_sc, y_sc,
                 *, nh, tc):
    lc = vc_ref.shape[1]
    s = v_ref.shape[1]
    cc = v_ref.shape[2]
    sub = V7X_SUBLANES
    neg_lam = -lam_ref[...]
    sp = jnp.maximum(neg_lam, 0.0) + jnp.log(1.0 + jnp.exp(-jnp.abs(neg_lam)))

    def coeffs(src_ref, src0, dst0):
        rows = pl.ds(src0, tc)
        drows = pl.ds(dst0, tc)
        for hh in range(nh):
            lanes = slice(hh * LRU_BLOCK, (hh + 1) * LRU_BLOCK)
            vb = src_ref[0, rows, lanes]
            pre = _dot(vb, w_ref[hh])
            vf = vb.astype(_F32)
            for d in range(2):
                c0 = (2 * d) * LRU_BLOCK
                pa = pre[:, c0:c0 + LRU_BLOCK] + bias_ref[2 * d:2 * d + 1, lanes]
                px = pre[:, c0 + LRU_BLOCK:c0 + 2 * LRU_BLOCK] + bias_ref[2 * d + 1:2 * d + 2, lanes]
                r = jax.nn.sigmoid(pa)
                i = jax.nn.sigmoid(px)
                log_a = (-RG_C) * r * sp[d:d + 1, lanes]
                a = jnp.exp(log_a)
                mult = jnp.sqrt(1.0 - a * a)
                a_sc[d, drows, lanes] = a
                b_sc[d, drows, lanes] = mult * i * vf

    def coeffs_ctx(c, carry):
        off = pl.multiple_of(c * tc, tc)
        coeffs(vc_ref, off, off)
        return carry

    def coeffs_lat(c, carry):
        off = pl.multiple_of(c * tc, tc)
        coeffs(v_ref, off, lc + off)
        return carry

    lax.fori_loop(0, lc // tc, coeffs_ctx, 0)
    lax.fori_loop(0, s // tc, coeffs_lat, 0)

    row = lax.broadcasted_iota(jnp.int32, (sub, cc), 0)

    def local_scan(a, b, reverse):
        for k in (1, 2, 4):
            if reverse:
                keep = row < sub - k
                shift = sub - k
            else:
                keep = row >= k
                shift = k
            a_sh = jnp.where(keep, pltpu.roll(a, shift, axis=0), 1.0)
            b_sh = jnp.where(keep, pltpu.roll(b, shift, axis=0), 0.0)
            b = a * b_sh + b
            a = a * a_sh
        return a, b

    def step(d, jrow, h_in, reverse):
        rows = pl.ds(pl.multiple_of(jrow * sub, sub), sub)
        a, b = local_scan(a_sc[d, rows, :], b_sc[d, rows, :], reverse)
        h = a * h_in + b
        last = h[0:1, :] if reverse else h[sub - 1:sub, :]
        return rows, h, jnp.broadcast_to(last, (sub, cc))

    def fwd(jrow, h_in):
        rows, h, h_out = step(0, jrow, h_in, False)
        y_sc[rows, :] = h
        return h_out

    h0 = jnp.zeros((sub, cc), _F32)
    lax.fori_loop(0, (lc + s) // sub, fwd, h0, unroll=4)

    def bwd_ctx(jj, h_in):
        _, _, h_out = step(1, lc // sub - 1 - jj, h_in, True)
        return h_out

    h_end = lax.fori_loop(0, lc // sub, bwd_ctx, h0, unroll=4)

    def bwd_lat(jj, h_in):
        jrow = (lc + s) // sub - 1 - jj
        rows, h, h_out = step(1, jrow, h_in, True)
        y_sc[rows, :] = y_sc[rows, :] + h
        return h_out

    lax.fori_loop(0, s // sub, bwd_lat, h_end, unroll=4)

    def gate_out(c, carry):
        rows = pl.ds(pl.multiple_of(c * tc, tc), tc)
        yrows = pl.ds(pl.multiple_of(lc + c * tc, tc), tc)
        g = g_ref[0, rows, :].astype(_F32)
        o_ref[0, rows, :] = (_gelu_tanh(g) * y_sc[yrows, :]).astype(_BF16)
        return carry

    lax.fori_loop(0, s // tc, gate_out, 0)


def _scan(v_lat, v_ctx, g_lat, w_cat, bias, lam, cc=512, tc=256):
    nb, s, r = v_lat.shape
    lc = v_ctx.shape[1]
    nh = cc // LRU_BLOCK
    blk = lambda L: pl.BlockSpec((1, L, cc), lambda b, c: (b, 0, c))
    return pl.pallas_call(
        functools.partial(_scan_kernel, nh=nh, tc=tc),
        grid=(nb, r // cc),
        in_specs=[blk(s), blk(lc), blk(s),
                  pl.BlockSpec((nh, LRU_BLOCK, 4 * LRU_BLOCK), lambda b, c: (c, 0, 0)),
                  pl.BlockSpec((4, cc), lambda b, c: (0, c)),
                  pl.BlockSpec((2, cc), lambda b, c: (0, c))],
        out_specs=blk(s),
        out_shape=jax.ShapeDtypeStruct((nb, s, r), _BF16),
        scratch_shapes=[pltpu.VMEM((2, lc + s, cc), _F32),
                        pltpu.VMEM((2, lc + s, cc), _F32),
                        pltpu.VMEM((lc + s, cc), _F32)],
        compiler_params=_params(("arbitrary", "arbitrary"), 48),
        name="rglru_scan",
    )(v_lat, v_ctx, g_lat, w_cat, bias, lam)


def _outproj_kernel(x_ref, a_ref, gt_ref, g_ref, w_ref, o_ref, *, rc):
    tm = o_ref.shape[1]
    x2 = x_ref.at[0]
    o2 = o_ref.at[0]
    gg = gt_ref[0] * g_ref[...]

    def residual(r):
        for q in range(rc // ROW_CHUNK):
            rows = slice(r * rc + q * ROW_CHUNK, r * rc + (q + 1) * ROW_CHUNK)
            inv = _inv_rms(o2[rows, :])
            o2[rows, :] = x2[rows, :] + (o2[rows, :] * inv) * gg

    for r in range(tm // rc):
        rows = slice(r * rc, (r + 1) * rc)
        o2[rows, :] = _dot(a_ref[0, rows, :], w_ref[...])
        if r > 0:
            residual(r - 1)
    residual(tm // rc - 1)


def _outproj(x, a, mods, k_gate, g, w, tm=512, rc=256):
    nb, L, d = x.shape
    r = a.shape[2]
    return pl.pallas_call(
        functools.partial(_outproj_kernel, rc=rc),
        grid=(nb, L // tm),
        in_specs=[pl.BlockSpec((1, tm, d), lambda b, i: (b, i, 0)),
                  pl.BlockSpec((1, tm, r), lambda b, i: (b, i, 0)),
                  pl.BlockSpec((1, 1, d), lambda b, i: (b, 0, k_gate)),
                  pl.BlockSpec((1, d), lambda b, i: (0, 0)),
                  pl.BlockSpec((r, d), lambda b, i: (0, 0))],
        out_specs=pl.BlockSpec((1, tm, d), lambda b, i: (b, i, 0)),
        out_shape=jax.ShapeDtypeStruct((nb, L, d), _F32),
        compiler_params=_params(("arbitrary", "arbitrary"), 48),
        name="mixer_out",
    )(x, a, mods, g.reshape(1, d), w)


def _dft_tables(n):
    k = np.arange(n, dtype=np.int64)
    ang = 2.0 * np.pi * ((k[:, None] * k[None, :]) % n).astype(np.float64) / n
    return np.cos(ang), np.sin(ang)


def _chdft_kernel(x_ref, sh_ref, sc_ref, g_ref, cs_ref, o_ref, h_ref):
    _norm_mod_into(x_ref.at[0], h_ref, g_ref[...], sh_ref[0], sc_ref[0])
    d = h_ref.shape[1]
    for gi in range(d // FGROUP):
        lanes = slice(gi * FGROUP, (gi + 1) * FGROUP)
        t = _dot(h_ref[:, lanes], cs_ref[...])
        o_ref[0, 0, :, lanes] = t[:, :FGROUP].astype(_BF16)
        o_ref[0, 1, :, lanes] = t[:, FGROUP:].astype(_BF16)


def _chdft(x, mods, k0, g, cs, tm=512):
    nb, L, d = x.shape
    mod_spec = lambda k: pl.BlockSpec((1, 1, d), lambda b, i: (b, 0, k))
    return pl.pallas_call(
        _chdft_kernel,
        grid=(nb, L // tm),
        in_specs=[pl.BlockSpec((1, tm, d), lambda b, i: (b, i, 0)),
                  mod_spec(k0), mod_spec(k0 + 1),
                  pl.BlockSpec((1, d), lambda b, i: (0, 0)),
                  pl.BlockSpec((FGROUP, 2 * FGROUP), lambda b, i: (0, 0))],
        out_specs=pl.BlockSpec((1, 2, tm, d), lambda b, i: (b, 0, i, 0)),
        out_shape=jax.ShapeDtypeStruct((nb, 2, L, d), _BF16),
        scratch_shapes=[pltpu.VMEM((tm, d), _BF16)],
        compiler_params=_params(("arbitrary", "arbitrary"), 40),
        name="fourier_channel_dft",
    )(x, mods, mods, g.reshape(1, d), cs)


def _seqdft_kernel(a_ref, b_ref, o_ref, *, scale):
    o_ref[0] = (_dot(a_ref[...], b_ref[0]) * scale).astype(_BF16)


def _seqdft(seq_tab, xcs, scale, tm=1024, tn=1024):
    nb, k2, d = xcs.shape
    L = seq_tab.shape[0]
    return pl.pallas_call(
        functools.partial(_seqdft_kernel, scale=scale),
        grid=(nb, d // tn, L // tm),
        in_specs=[pl.BlockSpec((tm, k2), lambda b, n, i: (i, 0)),
                  pl.BlockSpec((1, k2, tn), lambda b, n, i: (b, 0, n))],
        out_specs=pl.BlockSpec((1, tm, tn), lambda b, n, i: (b, i, n)),
        out_shape=jax.ShapeDtypeStruct((nb, L, d), _BF16),
        compiler_params=_params(("arbitrary", "arbitrary", "arbitrary"), 48),
        name="fourier_seq_dft",
    )(seq_tab, xcs)


def kernel(x, c, ctx, c_ctx, mod_w, mod_b, norm_g, ffn_w_in, ffn_w_out, rec_w_in, rec_conv_w,
           rec_conv_b, rec_gate_w, rec_gate_b, rec_lam, rec_w_out, fou_w_out):
    nb, s, d = x.shape
    lc = ctx.shape[1]
    depth = mod_w.shape[0]
    n_mixers = 2
    last_rec = ((depth - 1) // n_mixers) * n_mixers
    assert last_rec == 0, "context stream is only carried up to the first RG-LRU mixer"

    rows = 2 * V7X_SUBLANES
    cc = jnp.concatenate([c, c_ctx[None], jnp.zeros((rows - nb - 1, d), _F32)], axis=0)
    mods_all = _adaln(cc, mod_w, mod_b).reshape(depth, rows, 1, N_MOD * d)

    w_in_bf = ffn_w_in.astype(_BF16)
    w_out_bf = ffn_w_out.astype(_BF16)
    lat_row = lambda b: b
    ctx_row = lambda b: nb

    for i in range(depth):
        is_rec = (i % n_mixers) == 0
        j = i // n_mixers
        mods = mods_all[i]
        g = norm_g[i]
        x = _ffn(x, mods, lat_row, 0, g[0], g[1], w_in_bf, w_out_bf, i, 0)
        if is_rec:
            ctx1 = _ffn(ctx.reshape(1, nb * lc, d), mods, ctx_row, 0, g[0], g[1],
                        w_in_bf, w_out_bf, i, 0).reshape(nb, lc, d)
            w_rec = rec_w_in[j].astype(_BF16)
            g_lat, v_lat = _recin(x, mods, lat_row, 3, g[2], w_rec, rec_conv_w[j], rec_conv_b[j],
                                  tm=1024, blocklen=GRID_W, need_gate=True)
            _, v_ctx = _recin(ctx1, mods, ctx_row, 3, g[2], w_rec, rec_conv_w[j], rec_conv_b[j],
                              tm=lc, blocklen=lc, need_gate=False)
            gw = rec_gate_w[j]
            nh = gw.shape[2]
            w_cat = jnp.transpose(gw, (2, 3, 0, 1, 4)).reshape(nh, LRU_BLOCK, 4 * LRU_BLOCK)
            a = _scan(v_lat, v_ctx, g_lat, w_cat.astype(_BF16),
                      rec_gate_b[j].reshape(4, -1), rec_lam[j])
            w_mix = rec_w_out[j].astype(_BF16)
        else:
            cos_c, sin_c = _dft_tables(FGROUP)
            cs = jnp.asarray(np.concatenate([cos_c, sin_c], axis=1), _BF16)
            cos_s, sin_s = _dft_tables(s)
            seq_tab = jnp.asarray(np.concatenate([cos_s, -sin_s], axis=1), _BF16)
            xcs = _chdft(x, mods, 3, g[2], cs).reshape(nb, 2 * s, d)
            a = _seqdft(seq_tab, xcs, 1.0 / math.sqrt(s * FGROUP))
            w_mix = fou_w_out[j].astype(_BF16)
        x = _outproj(x, a, mods, 5, g[3], w_mix)
        x = _ffn(x, mods, lat_row, 6, g[4], g[5], w_in_bf, w_out_bf, i, 1)
    return x
```

```python
import functools
import math

import jax
import jax.numpy as jnp
import numpy as np
from jax import lax
from jax.experimental import pallas as pl
from jax.experimental.pallas import tpu as pltpu

N_SUB = 3
N_MOD = 3 * N_SUB
MACARON = 0.5
GRID_W = 64
CONV_W = 4
CONV_LEFT = CONV_W // 2
LRU_BLOCK = 128
RG_C = 8.0
LOG2E = math.log2(math.e)
F32_TINY = float(np.finfo(np.float32).tiny)
FGROUP = 256
EPS = 1e-6

V7X_SUBLANES = 8
ROW_CHUNK = 64
REC_ROW_CHUNK = 256
V7X_LANES = 128
MIB = 1024 * 1024

_BF16 = jnp.bfloat16
_F32 = jnp.float32


def _params(semantics, vmem_mib):
    return pltpu.CompilerParams(dimension_semantics=semantics,
                                vmem_limit_bytes=vmem_mib * MIB)


def _rms(x, g):
    return x * lax.rsqrt(jnp.mean(x * x, axis=-1, keepdims=True) + EPS) * g


def _norm_mod(x, g, shift, scale):
    return _rms(x, g) * (1.0 + scale) + shift


def _dot(a, b):
    return jnp.dot(a, b, preferred_element_type=_F32)


def _for_row_chunks(n_rows, fn, rn=ROW_CHUNK, unroll=2):
    def body(r, carry):
        fn(pl.ds(pl.multiple_of(r * rn, rn), rn))
        return carry
    lax.fori_loop(0, n_rows // rn, body, 0, unroll=unroll)


def _inv_rms(x):
    return lax.rsqrt(jnp.mean(x * x, axis=-1, keepdims=True) + EPS)


def _norm_mod_into(x_ref, h_ref, g, shift, scale):
    gm = g * (1.0 + scale)

    def fn(rows):
        inv = _inv_rms(x_ref[rows, :])
        h_ref[rows, :] = (x_ref[rows, :] * inv * gm + shift).astype(h_ref.dtype)
    _for_row_chunks(x_ref.shape[0], fn)


def _adaln_kernel(c_ref, w_ref, b_ref, o_ref):
    c = c_ref[...]
    a = (c * jax.nn.sigmoid(c)).astype(_BF16)
    o_ref[0] = _dot(a, w_ref[0].astype(_BF16)) + b_ref[0]


def _adaln(cc, mod_w, mod_b, tn=1024):
    depth, d, n = mod_w.shape
    rows = cc.shape[0]
    return pl.pallas_call(
        _adaln_kernel,
        grid=(depth, n // tn),
        in_specs=[pl.BlockSpec((rows, d), lambda l, j: (0, 0)),
                  pl.BlockSpec((1, d, tn), lambda l, j: (l, 0, j)),
                  pl.BlockSpec((1, 1, tn), lambda l, j: (l, 0, j))],
        out_specs=pl.BlockSpec((1, rows, tn), lambda l, j: (l, 0, j)),
        out_shape=jax.ShapeDtypeStruct((depth, rows, n), _F32),
        compiler_params=_params(("arbitrary", "arbitrary"), 40),
        name="adaln",
    )(cc, mod_w, mod_b.reshape(depth, 1, n))


def _ffn_kernel(x_ref, sh_ref, sc_ref, gt_ref, gpre_ref, gpost_ref, wg_ref, wu_ref, wo_ref,
                o_ref, h_ref, *, rc, nf):
    j = pl.program_id(2)
    nj = pl.num_programs(2)
    tm = h_ref.shape[0]
    x2 = x_ref.at[0]
    o2 = o_ref.at[0]

    nr = tm // rc
    sub = [slice(q * ROW_CHUNK, (q + 1) * ROW_CHUNK) for q in range(tm // ROW_CHUNK)]
    per = rc // ROW_CHUNK

    def norm_rows(r, gm, shift):
        for rows in sub[r * per:(r + 1) * per]:
            inv = _inv_rms(x2[rows, :])
            h_ref[rows, :] = (x2[rows, :] * inv * gm + shift).astype(_BF16)

    def residual_rows(r, gg):
        for rows in sub[r * per:(r + 1) * per]:
            inv = _inv_rms(o2[rows, :])
            o2[rows, :] = x2[rows, :] + (o2[rows, :] * inv) * gg

    def step(first, last):
        if first:
            gm = gpre_ref[...] * (1.0 + sc_ref[0])
            shift = sh_ref[0]
            norm_rows(0, gm, shift)
        if last:
            gg = (MACARON * gt_ref[0]) * gpost_ref[...]
        for r in range(nr):
            rows = slice(r * rc, (r + 1) * rc)
            h = h_ref[rows, :]
            g = _dot(h, wg_ref[...])
            u = _dot(h, wu_ref[...])
            a = (g * jax.nn.sigmoid(g) * u).astype(_BF16)
            y = _dot(a, wo_ref[...])
            if first:
                o2[rows, :] = y
                if r + 1 < nr:
                    norm_rows(r + 1, gm, shift)
            else:
                o2[rows, :] += y
            if last and r > 0:
                residual_rows(r - 1, gg)
        if last:
            residual_rows(nr - 1, gg)

    if nf == 1:
        step(True, True)
    else:
        pl.when(j == 0)(lambda: step(True, False))
        pl.when(j == nj - 1)(lambda: step(False, True))
        if nf > 2:
            pl.when((j > 0) & (j < nj - 1))(lambda: step(False, False))


def _ffn(x, mods, row_of, k0, g_pre, g_post, w_in, w_out, layer, sub, tm=1024, tf=512, rc=256):
    nb, L, d = x.shape
    f = w_out.shape[2]
    nf = f // tf
    mod_spec = lambda k: pl.BlockSpec((1, 1, d), lambda b, i, j: (row_of(b), 0, k))
    vec_spec = pl.BlockSpec((1, d), lambda b, i, j: (0, 0))
    return pl.pallas_call(
        functools.partial(_ffn_kernel, rc=rc, nf=nf),
        grid=(nb, L // tm, nf),
        in_specs=[pl.BlockSpec((1, tm, d), lambda b, i, j: (b, i, 0)),
                  mod_spec(k0), mod_spec(k0 + 1), mod_spec(k0 + 2),
                  vec_spec, vec_spec,
                  pl.BlockSpec((None, None, d, tf), lambda b, i, j: (layer, sub, 0, j)),
                  pl.BlockSpec((None, None, d, tf), lambda b, i, j: (layer, sub, 0, nf + j)),
                  pl.BlockSpec((None, None, tf, d), lambda b, i, j: (layer, sub, j, 0))],
        out_specs=pl.BlockSpec((1, tm, d), lambda b, i, j: (b, i, 0)),
        out_shape=jax.ShapeDtypeStruct((nb, L, d), _F32),
        scratch_shapes=[pltpu.VMEM((tm, d), _BF16)],
        compiler_params=_params(("arbitrary", "arbitrary", "arbitrary"), 60),
        name="ffn",
    )(x, mods, mods, mods, g_pre.reshape(1, d), g_post.reshape(1, d), w_in, w_in, w_out)


def _conv_rows(v, cw, cb, blocklen):
    tm = v.shape[0]
    pos = lax.broadcasted_iota(jnp.int32, (tm, 1), 0) % blocklen
    out = cb + cw[CONV_LEFT:CONV_LEFT + 1] * v
    for k in range(CONV_W):
        off = k - CONV_LEFT
        if off == 0:
            continue
        shifted = pltpu.roll(v, (-off) % tm, axis=0)
        valid = (pos + off >= 0) & (pos + off < blocklen)
        out = out + cw[k:k + 1] * jnp.where(valid, shifted, 0.0)
    return out


def _recin_kernel(x_ref, sh_ref, sc_ref, g_ref, wg_ref, wv_ref, cw_ref, cb_ref,
                  og_ref, ov_ref, h_ref, *, blocklen):
    n = pl.program_id(2)

    tm = h_ref.shape[0]
    rc = min(tm, REC_ROW_CHUNK)

    @pl.when(n == 0)
    def _():
        _norm_mod_into(x_ref.at[0], h_ref, g_ref[...], sh_ref[0], sc_ref[0])

    for r in range(tm // rc):
        rows = slice(r * rc, (r + 1) * rc)
        h = h_ref[rows, :]
        if og_ref is not None:
            og_ref[0, rows, :] = _dot(h, wg_ref[...]).astype(_BF16)
        v = _dot(h, wv_ref[...])
        ov_ref[0, rows, :] = _conv_rows(v, cw_ref[...], cb_ref[...], blocklen).astype(_BF16)


def _recin_nogate_kernel(x_ref, sh_ref, sc_ref, g_ref, wv_ref, cw_ref, cb_ref, ov_ref, h_ref,
                         *, blocklen):
    _recin_kernel(x_ref, sh_ref, sc_ref, g_ref, None, wv_ref, cw_ref, cb_ref, None, ov_ref,
                  h_ref, blocklen=blocklen)


def _recin(x, mods, row_of, k0, g, w_in, conv_w, conv_b, tm, blocklen, need_gate, tn=512):
    nb, L, d = x.shape
    r = w_in.shape[1] // 2
    nn = r // tn
    mod_spec = lambda k: pl.BlockSpec((1, 1, d), lambda b, i, n: (row_of(b), 0, k))
    x_spec = pl.BlockSpec((1, tm, d), lambda b, i, n: (b, i, 0))
    vec_spec = pl.BlockSpec((1, d), lambda b, i, n: (0, 0))
    wg_spec = pl.BlockSpec((d, tn), lambda b, i, n: (0, n))
    wv_spec = pl.BlockSpec((d, tn), lambda b, i, n: (0, nn + n))
    cw_spec = pl.BlockSpec((CONV_W, tn), lambda b, i, n: (0, n))
    cb_spec = pl.BlockSpec((1, tn), lambda b, i, n: (0, n))
    o_spec = pl.BlockSpec((1, tm, tn), lambda b, i, n: (b, i, n))
    o_shape = jax.ShapeDtypeStruct((nb, L, r), _BF16)
    common = dict(grid=(nb, L // tm, nn),
                  scratch_shapes=[pltpu.VMEM((tm, d), _BF16)],
                  compiler_params=_params(("arbitrary", "arbitrary", "arbitrary"), 48))
    g2 = g.reshape(1, d)
    cb2 = conv_b.reshape(1, r)
    if need_gate:
        return pl.pallas_call(
            functools.partial(_recin_kernel, blocklen=blocklen),
            in_specs=[x_spec, mod_spec(k0), mod_spec(k0 + 1), vec_spec, wg_spec, wv_spec,
                      cw_spec, cb_spec],
            out_specs=[o_spec, o_spec], out_shape=[o_shape, o_shape],
            name="rec_in", **common,
        )(x, mods, mods, g2, w_in, w_in, conv_w, cb2)
    return None, pl.pallas_call(
        functools.partial(_recin_nogate_kernel, blocklen=blocklen),
        in_specs=[x_spec, mod_spec(k0), mod_spec(k0 + 1), vec_spec, wv_spec, cw_spec, cb_spec],
        out_specs=o_spec, out_shape=o_shape,
        name="rec_in_ctx", **common,
    )(x, mods, mods, g2, w_in, conv_w, cb2)


def _gelu_tanh(x):
    c = math.sqrt(2.0 / math.pi)
    hx = 0.5 * x
    return hx + hx * jnp.tanh(x * (c + (0.044715 * c) * (x * x)))


def _seg_pitch(seg_len):
    assert seg_len % (2 * V7X_SUBLANES) == 0
    return seg_len + V7X_SUBLANES


def _scan_kernel(v_ref, vc_ref, g_ref, w_ref, bias_ref, lam_ref, o_ref,
                 a_sc, b_sc, hl_sc, al_sc, yt_sc, y_sc, *, nh):
    lc = vc_ref.shape[1]
    s = v_ref.shape[1]
    nseg = V7X_SUBLANES
    seg, cseg = s // nseg, lc // nseg
    p, pc = _seg_pitch(seg), _seg_pitch(cseg)
    cb = nseg * p

    neg_lam = -lam_ref[...]
    sp = jnp.maximum(neg_lam, 0.0) + jnp.log(1.0 + jnp.exp(-jnp.abs(neg_lam)))
    q = (-0.5 * RG_C * LOG2E) * sp
    hb = 0.5 * bias_ref[...]

    def coeffs(vb, pre, hh, store):
        lanes = slice(hh * LRU_BLOCK, (hh + 1) * LRU_BLOCK)
        vh = 0.5 * vb.astype(_F32)
        for d in range(2):
            c0 = (2 * d) * LRU_BLOCK
            ta = jnp.tanh(pre[:, c0:c0 + LRU_BLOCK] + hb[2 * d:2 * d + 1, lanes])
            tx = jnp.tanh(pre[:, c0 + LRU_BLOCK:c0 + 2 * LRU_BLOCK]
                          + hb[2 * d + 1:2 * d + 2, lanes])
            qd = q[d:d + 1, lanes]
            a = jnp.exp2(qd + qd * ta)
            m2 = 1.0 - a * a
            mult = m2 * lax.rsqrt(jnp.maximum(m2, F32_TINY))
            store(d, a, (mult * vh) * (1.0 + tx))

    def lat_coeffs(m, carry):
        src = pl.ds(pl.multiple_of(m * seg, seg), seg)
        dst = pl.ds(pl.multiple_of(m * p, V7X_SUBLANES), seg)
        for hh in range(nh):
            vb = v_ref[0, src, hh * LRU_BLOCK:(hh + 1) * LRU_BLOCK]

            def store(d, a, b, hh=hh):
                a_sc[d, hh, dst, :] = a
                b_sc[d, hh, dst, :] = b
            coeffs(vb, _dot(vb, w_ref[hh]), hh, store)
        return carry

    lax.fori_loop(0, nseg, lat_coeffs, 0)

    for hh in range(nh):
        vb = vc_ref[0, :, hh * LRU_BLOCK:(hh + 1) * LRU_BLOCK]

        def store(d, a, b, hh=hh):
            for m in range(nseg):
                dst = slice(cb + m * pc, cb + m * pc + cseg)
                a_sc[d, hh, dst, :] = a[m * cseg:(m + 1) * cseg]
                b_sc[d, hh, dst, :] = b[m * cseg:(m + 1) * cseg]
        coeffs(vb, _dot(vb, w_ref[hh]), hh, store)

    vshape = (nseg, V7X_LANES)
    row = lax.broadcasted_iota(jnp.int32, vshape, 0)
    zeros = tuple(jnp.zeros(vshape, _F32) for _ in range(nh))
    ones = tuple(jnp.ones(vshape, _F32) for _ in range(nh))

    def sublane_scan(a, b, reverse):
        for k in (1, 2, 4):
            keep = (row < nseg - k) if reverse else (row >= k)
            shift = (nseg - k) if reverse else k
            a_sh = jnp.where(keep, pltpu.roll(a, shift, axis=0), 1.0)
            b_sh = jnp.where(keep, pltpu.roll(b, shift, axis=0), 0.0)
            b = a * b_sh + b
            a = a * a_sh
        return a, b

    def local_pass(d, base, pitch, length, reverse, save):
        def body(i, carry):
            k = (length - 1 - i) if reverse else i
            hs, prods = carry
            new_h, new_p = [], []
            for hh in range(nh):
                idx = pl.ds(base + k, nseg, stride=pitch)
                a = a_sc[d, hh, idx, :]
                h = a * hs[hh] + b_sc[d, hh, idx, :]
                pr = prods[hh] * a
                if save:
                    t = pl.ds(pl.multiple_of(k * nseg, nseg), nseg)
                    hl_sc[hh, t, :] = h
                    al_sc[hh, t, :] = pr
                new_h.append(h)
                new_p.append(pr)
            return tuple(new_h), tuple(new_p)
        return lax.fori_loop(0, length, body, (zeros, ones), unroll=8)

    def entry_states(h_fin, a_tot, g_in, reverse):
        acum, bcum = sublane_scan(a_tot, h_fin, reverse)
        e = acum * g_in + bcum
        if reverse:
            entry = jnp.where(row < nseg - 1, pltpu.roll(e, nseg - 1, axis=0), g_in)
            final = e[0:1, :]
        else:
            entry = jnp.where(row >= 1, pltpu.roll(e, 1, axis=0), g_in)
            final = e[nseg - 1:nseg, :]
        return entry, jnp.broadcast_to(final, vshape)

    def direction(d, reverse, finish):
        hc, pc_tot = local_pass(d, cb, pc, cseg, reverse, save=False)
        hl, pl_tot = local_pass(d, 0, p, seg, reverse, save=True)
        entries = []
        for hh in range(nh):
            _, g_ctx = entry_states(hc[hh], pc_tot[hh], zeros[hh], reverse)
            entry, _ = entry_states(hl[hh], pl_tot[hh], g_ctx, reverse)
            entries.append(entry)

        def fix(k, carry):
            t = pl.ds(pl.multiple_of(k * nseg, nseg), nseg)
            for hh in range(nh):
                finish(hh, k, t, hl_sc[hh, t, :] + al_sc[hh, t, :] * entries[hh])
            return carry
        lax.fori_loop(0, seg, fix, 0, unroll=8)

    def keep_forward(hh, k, t, y):
        yt_sc[hh, t, :] = y

    def add_backward(hh, k, t, y):
        y_sc[hh, pl.ds(k, nseg, stride=p), :] = yt_sc[hh, t, :] + y

    direction(0, False, keep_forward)
    direction(1, True, add_backward)

    def gate_out(m, carry):
        rows = pl.ds(pl.multiple_of(m * seg, seg), seg)
        yrows = pl.ds(pl.multiple_of(m * p, V7X_SUBLANES), seg)
        for hh in range(nh):
            lanes = slice(hh * LRU_BLOCK, (hh + 1) * LRU_BLOCK)
            g = g_ref[0, rows, lanes].astype(_F32)
            o_ref[0, rows, lanes] = (_gelu_tanh(g) * y_sc[hh, yrows, :]).astype(_BF16)
        return carry

    lax.fori_loop(0, nseg, gate_out, 0)


def _scan(v_lat, v_ctx, g_lat, w_cat, bias, lam, cc=512):
    nb, s, r = v_lat.shape
    lc = v_ctx.shape[1]
    nh = cc // LRU_BLOCK
    nseg = V7X_SUBLANES
    coef_rows = nseg * (_seg_pitch(s // nseg) + _seg_pitch(lc // nseg))
    blk = lambda L: pl.BlockSpec((1, L, cc), lambda b, c: (b, 0, c))
    slab = lambda rows: pltpu.VMEM((nh, rows, V7X_LANES), _F32)
    return pl.pallas_call(
        functools.partial(_scan_kernel, nh=nh),
        grid=(nb, r // cc),
        in_specs=[blk(s), blk(lc), blk(s),
                  pl.BlockSpec((nh, LRU_BLOCK, 4 * LRU_BLOCK), lambda b, c: (c, 0, 0)),
                  pl.BlockSpec((4, cc), lambda b, c: (0, c)),
                  pl.BlockSpec((2, cc), lambda b, c: (0, c))],
        out_specs=blk(s),
        out_shape=jax.ShapeDtypeStruct((nb, s, r), _BF16),
        scratch_shapes=[pltpu.VMEM((2, nh, coef_rows, V7X_LANES), _F32),
                        pltpu.VMEM((2, nh, coef_rows, V7X_LANES), _F32),
                        slab(s), slab(s), slab(s), slab(nseg * _seg_pitch(s // nseg))],
        compiler_params=_params(("arbitrary", "arbitrary"), 56),
        name="rglru_scan",
    )(v_lat, v_ctx, g_lat, w_cat, bias, lam)


def _outproj_kernel(x_ref, a_ref, gt_ref, g_ref, w_ref, o_ref, *, rc):
    tm = o_ref.shape[1]
    x2 = x_ref.at[0]
    o2 = o_ref.at[0]
    gg = gt_ref[0] * g_ref[...]

    def residual(r):
        for q in range(rc // ROW_CHUNK):
            rows = slice(r * rc + q * ROW_CHUNK, r * rc + (q + 1) * ROW_CHUNK)
            inv = _inv_rms(o2[rows, :])
            o2[rows, :] = x2[rows, :] + (o2[rows, :] * inv) * gg

    for r in range(tm // rc):
        rows = slice(r * rc, (r + 1) * rc)
        o2[rows, :] = _dot(a_ref[0, rows, :], w_ref[...])
        if r > 0:
            residual(r - 1)
    residual(tm // rc - 1)


def _outproj(x, a, mods, k_gate, g, w, tm=512, rc=256):
    nb, L, d = x.shape
    r = a.shape[2]
    return pl.pallas_call(
        functools.partial(_outproj_kernel, rc=rc),
        grid=(nb, L // tm),
        in_specs=[pl.BlockSpec((1, tm, d), lambda b, i: (b, i, 0)),
                  pl.BlockSpec((1, tm, r), lambda b, i: (b, i, 0)),
                  pl.BlockSpec((1, 1, d), lambda b, i: (b, 0, k_gate)),
                  pl.BlockSpec((1, d), lambda b, i: (0, 0)),
                  pl.BlockSpec((r, d), lambda b, i: (0, 0))],
        out_specs=pl.BlockSpec((1, tm, d), lambda b, i: (b, i, 0)),
        out_shape=jax.ShapeDtypeStruct((nb, L, d), _F32),
        compiler_params=_params(("arbitrary", "arbitrary"), 48),
        name="mixer_out",
    )(x, a, mods, g.reshape(1, d), w)


def _dft_tables(n):
    k = np.arange(n, dtype=np.int64)
    ang = 2.0 * np.pi * ((k[:, None] * k[None, :]) % n).astype(np.float64) / n
    return np.cos(ang), np.sin(ang)


def _chdft_kernel(x_ref, sh_ref, sc_ref, g_ref, cs_ref, o_ref, h_ref):
    _norm_mod_into(x_ref.at[0], h_ref, g_ref[...], sh_ref[0], sc_ref[0])
    d = h_ref.shape[1]
    for gi in range(d // FGROUP):
        lanes = slice(gi * FGROUP, (gi + 1) * FGROUP)
        t = _dot(h_ref[:, lanes], cs_ref[...])
        o_ref[0, 0, :, lanes] = t[:, :FGROUP].astype(_BF16)
        o_ref[0, 1, :, lanes] = t[:, FGROUP:].astype(_BF16)


def _chdft(x, mods, k0, g, cs, tm=512):
    nb, L, d = x.shape
    mod_spec = lambda k: pl.BlockSpec((1, 1, d), lambda b, i: (b, 0, k))
    return pl.pallas_call(
        _chdft_kernel,
        grid=(nb, L // tm),
        in_specs=[pl.BlockSpec((1, tm, d), lambda b, i: (b, i, 0)),
                  mod_spec(k0), mod_spec(k0 + 1),
                  pl.BlockSpec((1, d), lambda b, i: (0, 0)),
                  pl.BlockSpec((FGROUP, 2 * FGROUP), lambda b, i: (0, 0))],
        out_specs=pl.BlockSpec((1, 2, tm, d), lambda b, i: (b, 0, i, 0)),
        out_shape=jax.ShapeDtypeStruct((nb, 2, L, d), _BF16),
        scratch_shapes=[pltpu.VMEM((tm, d), _BF16)],
        compiler_params=_params(("arbitrary", "arbitrary"), 40),
        name="fourier_channel_dft",
    )(x, mods, mods, g.reshape(1, d), cs)


def _seqdft_kernel(a_ref, b_ref, o_ref, *, scale):
    o_ref[0] = (_dot(a_ref[...], b_ref[0]) * scale).astype(_BF16)


def _seqdft(seq_tab, xcs, scale, tm=1024, tn=1024):
    nb, k2, d = xcs.shape
    L = seq_tab.shape[0]
    return pl.pallas_call(
        functools.partial(_seqdft_kernel, scale=scale),
        grid=(nb, d // tn, L // tm),
        in_specs=[pl.BlockSpec((tm, k2), lambda b, n, i: (i, 0)),
                  pl.BlockSpec((1, k2, tn), lambda b, n, i: (b, 0, n))],
        out_specs=pl.BlockSpec((1, tm, tn), lambda b, n, i: (b, i, n)),
        out_shape=jax.ShapeDtypeStruct((nb, L, d), _BF16),
        compiler_params=_params(("arbitrary", "arbitrary", "arbitrary"), 48),
        name="fourier_seq_dft",
    )(seq_tab, xcs)


def kernel(x, c, ctx, c_ctx, mod_w, mod_b, norm_g, ffn_w_in, ffn_w_out, rec_w_in, rec_conv_w,
           rec_conv_b, rec_gate_w, rec_gate_b, rec_lam, rec_w_out, fou_w_out):
    nb, s, d = x.shape
    lc = ctx.shape[1]
    depth = mod_w.shape[0]
    n_mixers = 2
    last_rec = ((depth - 1) // n_mixers) * n_mixers
    assert last_rec == 0, "context stream is only carried up to the first RG-LRU mixer"

    rows = 2 * V7X_SUBLANES
    cc = jnp.concatenate([c, c_ctx[None], jnp.zeros((rows - nb - 1, d), _F32)], axis=0)
    mods_all = _adaln(cc, mod_w, mod_b).reshape(depth, rows, 1, N_MOD * d)

    w_in_bf = ffn_w_in.astype(_BF16)
    w_out_bf = ffn_w_out.astype(_BF16)
    lat_row = lambda b: b
    ctx_row = lambda b: nb

    for i in range(depth):
        is_rec = (i % n_mixers) == 0
        j = i // n_mixers
        mods = mods_all[i]
        g = norm_g[i]
        x = _ffn(x, mods, lat_row, 0, g[0], g[1], w_in_bf, w_out_bf, i, 0)
        if is_rec:
            ctx1 = _ffn(ctx.reshape(1, nb * lc, d), mods, ctx_row, 0, g[0], g[1],
                        w_in_bf, w_out_bf, i, 0).reshape(nb, lc, d)
            w_rec = rec_w_in[j].astype(_BF16)
            g_lat, v_lat = _recin(x, mods, lat_row, 3, g[2], w_rec, rec_conv_w[j], rec_conv_b[j],
                                  tm=1024, blocklen=GRID_W, need_gate=True)
            _, v_ctx = _recin(ctx1, mods, ctx_row, 3, g[2], w_rec, rec_conv_w[j], rec_conv_b[j],
                              tm=lc, blocklen=lc, need_gate=False)
            gw = rec_gate_w[j]
            nh = gw.shape[2]
            w_cat = jnp.transpose(gw, (2, 3, 0, 1, 4)).reshape(nh, LRU_BLOCK, 4 * LRU_BLOCK)
            a = _scan(v_lat, v_ctx, g_lat, (0.5 * w_cat).astype(_BF16),
                      rec_gate_b[j].reshape(4, -1), rec_lam[j])
            w_mix = rec_w_out[j].astype(_BF16)
        else:
            cos_c, sin_c = _dft_tables(FGROUP)
            cs = jnp.asarray(np.concatenate([cos_c, sin_c], axis=1), _BF16)
            cos_s, sin_s = _dft_tables(s)
            seq_tab = jnp.asarray(np.concatenate([cos_s, -sin_s], axis=1), _BF16)
            xcs = _chdft(x, mods, 3, g[2], cs).reshape(nb, 2 * s, d)
            a = _seqdft(seq_tab, xcs, 1.0 / math.sqrt(s * FGROUP))
            w_mix = fou_w_out[j].astype(_BF16)
        x = _outproj(x, a, mods, 5, g[3], w_mix)
        x = _ffn(x, mods, lat_row, 6, g[4], g[5], w_in_bf, w_out_bf, i, 1)
    return x
```

```python
import functools
import math

import jax
import jax.numpy as jnp
import numpy as np
from jax import lax
from jax.experimental import pallas as pl
from jax.experimental.pallas import tpu as pltpu

N_SUB = 3
N_MOD = 3 * N_SUB
MACARON = 0.5
GRID_W = 64
CONV_W = 4
CONV_LEFT = CONV_W // 2
LRU_BLOCK = 128
RG_C = 8.0
LOG2E = math.log2(math.e)
F32_TINY = float(np.finfo(np.float32).tiny)
FGROUP = 256
EPS = 1e-6

V7X_SUBLANES = 8
ROW_CHUNK = 64
REC_ROW_CHUNK = 256
V7X_LANES = 128
MIB = 1024 * 1024

_BF16 = jnp.bfloat16
_F32 = jnp.float32


def _params(semantics, vmem_mib):
    return pltpu.CompilerParams(dimension_semantics=semantics,
                                vmem_limit_bytes=vmem_mib * MIB)


def _dot(a, b):
    return jnp.dot(a, b, preferred_element_type=_F32)


def _for_row_chunks(n_rows, fn, rn=ROW_CHUNK, unroll=2):
    def body(r, carry):
        fn(pl.ds(pl.multiple_of(r * rn, rn), rn))
        return carry
    lax.fori_loop(0, n_rows // rn, body, 0, unroll=unroll)


def _inv_rms(x):
    return lax.rsqrt(jnp.mean(x * x, axis=-1, keepdims=True) + EPS)


def _norm_mod_into(x_ref, h_ref, g, shift, scale):
    gm = g * (1.0 + scale)

    def fn(rows):
        inv = _inv_rms(x_ref[rows, :])
        h_ref[rows, :] = (x_ref[rows, :] * inv * gm + shift).astype(h_ref.dtype)
    _for_row_chunks(x_ref.shape[0], fn)


def _adaln_kernel(c_ref, w_ref, b_ref, o_ref):
    c = c_ref[...]
    a = (c * jax.nn.sigmoid(c)).astype(_BF16)
    o_ref[0] = _dot(a, w_ref[0].astype(_BF16)) + b_ref[0]


def _adaln(cc, mod_w, mod_b, tn=1024):
    depth, d, n = mod_w.shape
    rows = cc.shape[0]
    return pl.pallas_call(
        _adaln_kernel,
        grid=(depth, n // tn),
        in_specs=[pl.BlockSpec((rows, d), lambda l, j: (0, 0)),
                  pl.BlockSpec((1, d, tn), lambda l, j: (l, 0, j)),
                  pl.BlockSpec((1, 1, tn), lambda l, j: (l, 0, j))],
        out_specs=pl.BlockSpec((1, rows, tn), lambda l, j: (l, 0, j)),
        out_shape=jax.ShapeDtypeStruct((depth, rows, n), _F32),
        compiler_params=_params(("arbitrary", "arbitrary"), 40),
        name="adaln",
    )(cc, mod_w, mod_b.reshape(depth, 1, n))


def _ffn_kernel(x_ref, sh_ref, sc_ref, gt_ref, gpre_ref, gpost_ref, wg_ref, wu_ref, wo_ref,
                *rest, rc, nf, cast_next):
    if cast_next:
        nwi_ref, nwo_ref, o_ref, cwi_ref, cwo_ref, h_ref = rest
        cwi_ref[...] = nwi_ref[...].astype(_BF16)
        cwo_ref[...] = nwo_ref[...].astype(_BF16)
    else:
        o_ref, h_ref = rest
    j = pl.program_id(2)
    nj = pl.num_programs(2)
    tm = h_ref.shape[0]
    x2 = x_ref.at[0]
    o2 = o_ref.at[0]

    nr = tm // rc
    sub = [slice(q * ROW_CHUNK, (q + 1) * ROW_CHUNK) for q in range(tm // ROW_CHUNK)]
    per = rc // ROW_CHUNK

    def norm_rows(r, gm, shift):
        for rows in sub[r * per:(r + 1) * per]:
            inv = _inv_rms(x2[rows, :])
            h_ref[rows, :] = (x2[rows, :] * inv * gm + shift).astype(_BF16)

    def residual_rows(r, gg):
        for rows in sub[r * per:(r + 1) * per]:
            inv = _inv_rms(o2[rows, :])
            o2[rows, :] = x2[rows, :] + (o2[rows, :] * inv) * gg

    def step(first, last):
        if first:
            gm = gpre_ref[...] * (1.0 + sc_ref[0])
            shift = sh_ref[0]
            norm_rows(0, gm, shift)
        if last:
            gg = (MACARON * gt_ref[0]) * gpost_ref[...]
        for r in range(nr):
            rows = slice(r * rc, (r + 1) * rc)
            h = h_ref[rows, :]
            g = _dot(h, wg_ref[...])
            u = _dot(h, wu_ref[...])
            a = (g * jax.nn.sigmoid(g) * u).astype(_BF16)
            y = _dot(a, wo_ref[...])
            if first:
                o2[rows, :] = y
                if r + 1 < nr:
                    norm_rows(r + 1, gm, shift)
            else:
                o2[rows, :] += y
            if last and r > 0:
                residual_rows(r - 1, gg)
        if last:
            residual_rows(nr - 1, gg)

    if nf == 1:
        step(True, True)
    else:
        pl.when(j == 0)(lambda: step(True, False))
        pl.when(j == nj - 1)(lambda: step(False, True))
        if nf > 2:
            pl.when((j > 0) & (j < nj - 1))(lambda: step(False, False))


def _ffn(x, mods, row_of, k0, g_pre, g_post, w_in, w_out, next_w=None, tm=1024, tf=512, rc=256):
    nb, L, d = x.shape
    f = w_out.shape[0]
    nf = f // tf
    nt = nb * (L // tm)
    mod_spec = lambda k: pl.BlockSpec((1, 1, d), lambda b, i, j: (row_of(b), 0, k))
    vec_spec = pl.BlockSpec((1, d), lambda b, i, j: (0, 0))
    in_specs = [pl.BlockSpec((1, tm, d), lambda b, i, j: (b, i, 0)),
                mod_spec(k0), mod_spec(k0 + 1), mod_spec(k0 + 2),
                vec_spec, vec_spec,
                pl.BlockSpec((d, tf), lambda b, i, j: (0, j)),
                pl.BlockSpec((d, tf), lambda b, i, j: (0, nf + j)),
                pl.BlockSpec((tf, d), lambda b, i, j: (j, 0))]
    args = [x, mods, mods, mods, g_pre.reshape(1, d), g_post.reshape(1, d), w_in, w_in, w_out]
    out_specs = [pl.BlockSpec((1, tm, d), lambda b, i, j: (b, i, 0))]
    out_shape = [jax.ShapeDtypeStruct((nb, L, d), _F32)]
    if next_w is not None:
        nw_in, nw_out, layer, sub = next_w
        assert d % nt == 0 and (2 * f) % nf == 0 and f % (nt * nf) == 0
        ri, ci, ro = d // nt, 2 * f // nf, f // (nt * nf)
        tile = lambda i: i * (L // tm)
        in_specs += [pl.BlockSpec((None, None, ri, ci),
                                  lambda b, i, j: (layer, sub, tile(b) + i, j)),
                     pl.BlockSpec((None, None, ro, d),
                                  lambda b, i, j: (layer, sub, (tile(b) + i) * nf + j, 0))]
        args += [nw_in, nw_out]
        out_specs += [pl.BlockSpec((ri, ci), lambda b, i, j: (tile(b) + i, j)),
                      pl.BlockSpec((ro, d), lambda b, i, j: ((tile(b) + i) * nf + j, 0))]
        out_shape += [jax.ShapeDtypeStruct((d, 2 * f), _BF16), jax.ShapeDtypeStruct((f, d), _BF16)]
    out = pl.pallas_call(
        functools.partial(_ffn_kernel, rc=rc, nf=nf, cast_next=next_w is not None),
        grid=(nb, L // tm, nf),
        in_specs=in_specs, out_specs=out_specs, out_shape=out_shape,
        scratch_shapes=[pltpu.VMEM((tm, d), _BF16)],
        compiler_params=_params(("arbitrary", "arbitrary", "arbitrary"), 60),
        name="ffn",
    )(*args)
    return out if next_w is not None else out[0]


def _conv_rows(v, cw, cb, blocklen):
    tm = v.shape[0]
    pos = lax.broadcasted_iota(jnp.int32, (tm, 1), 0) % blocklen
    out = cb + cw[CONV_LEFT:CONV_LEFT + 1] * v
    for k in range(CONV_W):
        off = k - CONV_LEFT
        if off == 0:
            continue
        shifted = pltpu.roll(v, (-off) % tm, axis=0)
        valid = (pos + off >= 0) & (pos + off < blocklen)
        out = out + cw[k:k + 1] * jnp.where(valid, shifted, 0.0)
    return out


def _recin_kernel(x_ref, sh_ref, sc_ref, g_ref, wg_ref, wv_ref, cw_ref, cb_ref,
                  og_ref, ov_ref, h_ref, *, blocklen):
    n = pl.program_id(2)

    tm = h_ref.shape[0]
    rc = min(tm, REC_ROW_CHUNK)

    x2 = x_ref.at[0]
    nr = tm // rc

    def norm_rows(r, gm, shift):
        for q in range(rc // ROW_CHUNK):
            rows = slice(r * rc + q * ROW_CHUNK, r * rc + (q + 1) * ROW_CHUNK)
            inv = _inv_rms(x2[rows, :])
            h_ref[rows, :] = (x2[rows, :] * inv * gm + shift).astype(_BF16)

    def step(first):
        if first:
            gm = g_ref[...] * (1.0 + sc_ref[0])
            shift = sh_ref[0]
            norm_rows(0, gm, shift)
        for r in range(nr):
            rows = slice(r * rc, (r + 1) * rc)
            h = h_ref[rows, :]
            if og_ref is not None:
                og_ref[0, rows, :] = _gelu_tanh(_dot(h, wg_ref[...])).astype(_BF16)
            v = _dot(h, wv_ref[...])
            ov_ref[0, rows, :] = _conv_rows(v, cw_ref[...], cb_ref[...], blocklen).astype(_BF16)
            if first and r + 1 < nr:
                norm_rows(r + 1, gm, shift)

    pl.when(n == 0)(lambda: step(True))
    pl.when(n > 0)(lambda: step(False))


def _recin_nogate_kernel(x_ref, sh_ref, sc_ref, g_ref, wv_ref, cw_ref, cb_ref, ov_ref, h_ref,
                         *, blocklen):
    _recin_kernel(x_ref, sh_ref, sc_ref, g_ref, None, wv_ref, cw_ref, cb_ref, None, ov_ref,
                  h_ref, blocklen=blocklen)


def _recin(x, mods, row_of, k0, g, w_in, conv_w, conv_b, tm, blocklen, need_gate, tn=1024):
    nb, L, d = x.shape
    r = w_in.shape[1] // 2
    nn = r // tn
    mod_spec = lambda k: pl.BlockSpec((1, 1, d), lambda b, i, n: (row_of(b), 0, k))
    x_spec = pl.BlockSpec((1, tm, d), lambda b, i, n: (b, i, 0))
    vec_spec = pl.BlockSpec((1, d), lambda b, i, n: (0, 0))
    wg_spec = pl.BlockSpec((d, tn), lambda b, i, n: (0, n))
    wv_spec = pl.BlockSpec((d, tn), lambda b, i, n: (0, nn + n))
    cw_spec = pl.BlockSpec((CONV_W, tn), lambda b, i, n: (0, n))
    cb_spec = pl.BlockSpec((1, tn), lambda b, i, n: (0, n))
    o_spec = pl.BlockSpec((1, tm, tn), lambda b, i, n: (b, i, n))
    o_shape = jax.ShapeDtypeStruct((nb, L, r), _BF16)
    common = dict(grid=(nb, L // tm, nn),
                  scratch_shapes=[pltpu.VMEM((tm, d), _BF16)],
                  compiler_params=_params(("arbitrary", "arbitrary", "arbitrary"), 56))
    g2 = g.reshape(1, d)
    cb2 = conv_b.reshape(1, r)
    if need_gate:
        return pl.pallas_call(
            functools.partial(_recin_kernel, blocklen=blocklen),
            in_specs=[x_spec, mod_spec(k0), mod_spec(k0 + 1), vec_spec, wg_spec, wv_spec,
                      cw_spec, cb_spec],
            out_specs=[o_spec, o_spec], out_shape=[o_shape, o_shape],
            name="rec_in", **common,
        )(x, mods, mods, g2, w_in, w_in, conv_w, cb2)
    return None, pl.pallas_call(
        functools.partial(_recin_nogate_kernel, blocklen=blocklen),
        in_specs=[x_spec, mod_spec(k0), mod_spec(k0 + 1), vec_spec, wv_spec, cw_spec, cb_spec],
        out_specs=o_spec, out_shape=o_shape,
        name="rec_in_ctx", **common,
    )(x, mods, mods, g2, w_in, conv_w, cb2)


def _gelu_tanh(x):
    c = math.sqrt(2.0 / math.pi)
    hx = 0.5 * x
    return hx + hx * jnp.tanh(x * (c + (0.044715 * c) * (x * x)))


def _seg_pitch(seg_len):
    assert seg_len % (2 * V7X_SUBLANES) == 0
    return seg_len + V7X_SUBLANES


def _scan_kernel(v_ref, vc_ref, g_ref, w_ref, bias_ref, lam_ref, o_ref,
                 a_sc, b_sc, hl_sc, al_sc, *, nh):
    lc = vc_ref.shape[1]
    s = v_ref.shape[1]
    nseg = V7X_SUBLANES
    seg, cseg = s // nseg, lc // nseg
    p, pc = _seg_pitch(seg), _seg_pitch(cseg)
    cb = nseg * p

    neg_lam = -lam_ref[...]
    sp = jnp.maximum(neg_lam, 0.0) + jnp.log(1.0 + jnp.exp(-jnp.abs(neg_lam)))
    q = (-0.5 * RG_C * LOG2E) * sp
    hb = 0.5 * bias_ref[...]

    def coeffs(vb, pre, hh, store):
        lanes = slice(hh * LRU_BLOCK, (hh + 1) * LRU_BLOCK)
        vh = 0.5 * vb.astype(_F32)
        for d in range(2):
            c0 = (2 * d) * LRU_BLOCK
            ta = jnp.tanh(pre[:, c0:c0 + LRU_BLOCK] + hb[2 * d:2 * d + 1, lanes])
            tx = jnp.tanh(pre[:, c0 + LRU_BLOCK:c0 + 2 * LRU_BLOCK]
                          + hb[2 * d + 1:2 * d + 2, lanes])
            qd = q[d:d + 1, lanes]
            a = jnp.exp2(qd + qd * ta)
            m2 = 1.0 - a * a
            mult = m2 * lax.rsqrt(jnp.maximum(m2, F32_TINY))
            store(d, a, (mult * vh) * (1.0 + tx))

    def lat_coeffs(m, carry):
        src = pl.ds(pl.multiple_of(m * seg, seg), seg)
        dst = pl.ds(pl.multiple_of(m * p, V7X_SUBLANES), seg)
        for hh in range(nh):
            vb = v_ref[0, src, hh * LRU_BLOCK:(hh + 1) * LRU_BLOCK]

            def store(d, a, b, hh=hh):
                a_sc[d, hh, dst, :] = a
                b_sc[d, hh, dst, :] = b
            coeffs(vb, _dot(vb, w_ref[hh]), hh, store)
        return carry

    lax.fori_loop(0, nseg, lat_coeffs, 0)

    for hh in range(nh):
        vb = vc_ref[0, :, hh * LRU_BLOCK:(hh + 1) * LRU_BLOCK]

        def store(d, a, b, hh=hh):
            for m in range(nseg):
                dst = slice(cb + m * pc, cb + m * pc + cseg)
                a_sc[d, hh, dst, :] = a[m * cseg:(m + 1) * cseg]
                b_sc[d, hh, dst, :] = b[m * cseg:(m + 1) * cseg]
        coeffs(vb, _dot(vb, w_ref[hh]), hh, store)

    vshape = (nseg, V7X_LANES)
    row = lax.broadcasted_iota(jnp.int32, vshape, 0)
    zeros = tuple(jnp.zeros(vshape, _F32) for _ in range(nh))
    ones = tuple(jnp.ones(vshape, _F32) for _ in range(nh))

    def sublane_scan(a, b, reverse):
        for k in (1, 2, 4):
            keep = (row < nseg - k) if reverse else (row >= k)
            shift = (nseg - k) if reverse else k
            a_sh = jnp.where(keep, pltpu.roll(a, shift, axis=0), 1.0)
            b_sh = jnp.where(keep, pltpu.roll(b, shift, axis=0), 0.0)
            b = a * b_sh + b
            a = a * a_sh
        return a, b

    def local_pass(base, pitch, length, save):
        def body(i, carry):
            hs, prods = carry
            new_h, new_p = [], []
            for d, k in ((0, i), (1, length - 1 - i)):
                idx = pl.ds(base + k, nseg, stride=pitch)
                t = pl.ds(pl.multiple_of(k * nseg, nseg), nseg)
                for hh in range(nh):
                    a = a_sc[d, hh, idx, :]
                    h = a * hs[d * nh + hh] + b_sc[d, hh, idx, :]
                    pr = prods[d * nh + hh] * a
                    if save:
                        hl_sc[d, hh, t, :] = h
                        al_sc[d, hh, t, :] = pr
                    new_h.append(h)
                    new_p.append(pr)
            return tuple(new_h), tuple(new_p)
        return lax.fori_loop(0, length, body, (zeros + zeros, ones + ones), unroll=8)

    def entry_states(h_fin, a_tot, g_in, reverse):
        acum, bcum = sublane_scan(a_tot, h_fin, reverse)
        e = acum * g_in + bcum
        if reverse:
            entry = jnp.where(row < nseg - 1, pltpu.roll(e, nseg - 1, axis=0), g_in)
            final = e[0:1, :]
        else:
            entry = jnp.where(row >= 1, pltpu.roll(e, 1, axis=0), g_in)
            final = e[nseg - 1:nseg, :]
        return entry, jnp.broadcast_to(final, vshape)

    hc, pc_tot = local_pass(cb, pc, cseg, save=False)
    hl, pl_tot = local_pass(0, p, seg, save=True)
    entries = []
    for d, reverse in ((0, False), (1, True)):
        for hh in range(nh):
            n = d * nh + hh
            _, g_ctx = entry_states(hc[n], pc_tot[n], zeros[hh], reverse)
            entry, _ = entry_states(hl[n], pl_tot[n], g_ctx, reverse)
            entries.append(entry)

    def fix(k, carry):
        t = pl.ds(pl.multiple_of(k * nseg, nseg), nseg)
        for hh in range(nh):
            hl_sc[0, hh, t, :] = ((hl_sc[0, hh, t, :] + al_sc[0, hh, t, :] * entries[hh])
                                  + (hl_sc[1, hh, t, :] + al_sc[1, hh, t, :] * entries[nh + hh]))
        return carry
    lax.fori_loop(0, seg, fix, 0, unroll=8)

    def gate_out(m, carry):
        rows = pl.ds(pl.multiple_of(m * seg, seg), seg)
        for hh in range(nh):
            lanes = slice(hh * LRU_BLOCK, (hh + 1) * LRU_BLOCK)
            y = hl_sc[0, hh, pl.ds(m, seg, stride=nseg), :]
            o_ref[0, rows, lanes] = (g_ref[0, rows, lanes].astype(_F32) * y).astype(_BF16)
        return carry

    lax.fori_loop(0, nseg, gate_out, 0)


def _scan(v_lat, v_ctx, g_lat, w_cat, bias, lam, cc=512):
    nb, s, r = v_lat.shape
    lc = v_ctx.shape[1]
    nh = cc // LRU_BLOCK
    nseg = V7X_SUBLANES
    coef_rows = nseg * (_seg_pitch(s // nseg) + _seg_pitch(lc // nseg))
    blk = lambda L: pl.BlockSpec((1, L, cc), lambda b, c: (b, 0, c))
    return pl.pallas_call(
        functools.partial(_scan_kernel, nh=nh),
        grid=(nb, r // cc),
        in_specs=[blk(s), blk(lc), blk(s),
                  pl.BlockSpec((nh, LRU_BLOCK, 4 * LRU_BLOCK), lambda b, c: (c, 0, 0)),
                  pl.BlockSpec((4, cc), lambda b, c: (0, c)),
                  pl.BlockSpec((2, cc), lambda b, c: (0, c))],
        out_specs=blk(s),
        out_shape=jax.ShapeDtypeStruct((nb, s, r), _BF16),
        scratch_shapes=[pltpu.VMEM((2, nh, coef_rows, V7X_LANES), _F32),
                        pltpu.VMEM((2, nh, coef_rows, V7X_LANES), _F32),
                        pltpu.VMEM((2, nh, s, V7X_LANES), _F32),
                        pltpu.VMEM((2, nh, s, V7X_LANES), _F32)],
        compiler_params=_params(("arbitrary", "arbitrary"), 56),
        name="rglru_scan",
    )(v_lat, v_ctx, g_lat, w_cat, bias, lam)


def _outproj_kernel(x_ref, a_ref, gt_ref, g_ref, w_ref, o_ref, *, rc):
    tm = o_ref.shape[1]
    x2 = x_ref.at[0]
    o2 = o_ref.at[0]
    gg = gt_ref[0] * g_ref[...]

    def residual(r):
        for q in range(rc // ROW_CHUNK):
            rows = slice(r * rc + q * ROW_CHUNK, r * rc + (q + 1) * ROW_CHUNK)
            inv = _inv_rms(o2[rows, :])
            o2[rows, :] = x2[rows, :] + (o2[rows, :] * inv) * gg

    for r in range(tm // rc):
        rows = slice(r * rc, (r + 1) * rc)
        o2[rows, :] = _dot(a_ref[0, rows, :], w_ref[...])
        if r > 0:
            residual(r - 1)
    residual(tm // rc - 1)


def _outproj(x, a, mods, k_gate, g, w, tm=512, rc=256):
    nb, L, d = x.shape
    r = a.shape[2]
    return pl.pallas_call(
        functools.partial(_outproj_kernel, rc=rc),
        grid=(nb, L // tm),
        in_specs=[pl.BlockSpec((1, tm, d), lambda b, i: (b, i, 0)),
                  pl.BlockSpec((1, tm, r), lambda b, i: (b, i, 0)),
                  pl.BlockSpec((1, 1, d), lambda b, i: (b, 0, k_gate)),
                  pl.BlockSpec((1, d), lambda b, i: (0, 0)),
                  pl.BlockSpec((r, d), lambda b, i: (0, 0))],
        out_specs=pl.BlockSpec((1, tm, d), lambda b, i: (b, i, 0)),
        out_shape=jax.ShapeDtypeStruct((nb, L, d), _F32),
        compiler_params=_params(("arbitrary", "arbitrary"), 48),
        name="mixer_out",
    )(x, a, mods, g.reshape(1, d), w)


def _dft_tables(n):
    k = np.arange(n, dtype=np.int64)
    ang = 2.0 * np.pi * ((k[:, None] * k[None, :]) % n).astype(np.float64) / n
    return np.cos(ang), np.sin(ang)


def _chdft_kernel(x_ref, sh_ref, sc_ref, g_ref, cs_ref, o_ref, h_ref):
    _norm_mod_into(x_ref.at[0], h_ref, g_ref[...], sh_ref[0], sc_ref[0])
    d = h_ref.shape[1]
    for gi in range(d // FGROUP):
        lanes = slice(gi * FGROUP, (gi + 1) * FGROUP)
        t = _dot(h_ref[:, lanes], cs_ref[...])
        o_ref[0, 0, :, lanes] = t[:, :FGROUP].astype(_BF16)
        o_ref[0, 1, :, lanes] = t[:, FGROUP:].astype(_BF16)


def _chdft(x, mods, k0, g, cs, tm=512):
    nb, L, d = x.shape
    mod_spec = lambda k: pl.BlockSpec((1, 1, d), lambda b, i: (b, 0, k))
    return pl.pallas_call(
        _chdft_kernel,
        grid=(nb, L // tm),
        in_specs=[pl.BlockSpec((1, tm, d), lambda b, i: (b, i, 0)),
                  mod_spec(k0), mod_spec(k0 + 1),
                  pl.BlockSpec((1, d), lambda b, i: (0, 0)),
                  pl.BlockSpec((FGROUP, 2 * FGROUP), lambda b, i: (0, 0))],
        out_specs=pl.BlockSpec((1, 2, tm, d), lambda b, i: (b, 0, i, 0)),
        out_shape=jax.ShapeDtypeStruct((nb, 2, L, d), _BF16),
        scratch_shapes=[pltpu.VMEM((tm, d), _BF16)],
        compiler_params=_params(("arbitrary", "arbitrary"), 40),
        name="fourier_channel_dft",
    )(x, mods, mods, g.reshape(1, d), cs)


def _seqdft_kernel(a_ref, b_ref, o_ref, *, scale):
    o_ref[0] = (_dot(a_ref[...], b_ref[0]) * scale).astype(_BF16)


def _seqdft(seq_tab, xcs, scale, tm=1024, tn=1024):
    nb, k2, d = xcs.shape
    L = seq_tab.shape[0]
    return pl.pallas_call(
        functools.partial(_seqdft_kernel, scale=scale),
        grid=(nb, d // tn, L // tm),
        in_specs=[pl.BlockSpec((tm, k2), lambda b, n, i: (i, 0)),
                  pl.BlockSpec((1, k2, tn), lambda b, n, i: (b, 0, n))],
        out_specs=pl.BlockSpec((1, tm, tn), lambda b, n, i: (b, i, n)),
        out_shape=jax.ShapeDtypeStruct((nb, L, d), _BF16),
        compiler_params=_params(("arbitrary", "arbitrary", "arbitrary"), 48),
        name="fourier_seq_dft",
    )(seq_tab, xcs)


def kernel(x, c, ctx, c_ctx, mod_w, mod_b, norm_g, ffn_w_in, ffn_w_out, rec_w_in, rec_conv_w,
           rec_conv_b, rec_gate_w, rec_gate_b, rec_lam, rec_w_out, fou_w_out):
    nb, s, d = x.shape
    lc = ctx.shape[1]
    depth = mod_w.shape[0]
    n_mixers = 2
    last_rec = ((depth - 1) // n_mixers) * n_mixers
    assert last_rec == 0, "context stream is only carried up to the first RG-LRU mixer"

    rows = 2 * V7X_SUBLANES
    cc = jnp.concatenate([c, c_ctx[None], jnp.zeros((rows - nb - 1, d), _F32)], axis=0)
    mods_all = _adaln(cc, mod_w, mod_b).reshape(depth, rows, 1, N_MOD * d)

    w_ffn = (ffn_w_in[0, 0].astype(_BF16), ffn_w_out[0, 0].astype(_BF16))
    order = [(i, sub) for i in range(depth) for sub in range(2)]

    def ffn_lat(x, mods, k0, g_pre, g_post, layer, sub):
        nonlocal w_ffn
        nxt = order.index((layer, sub)) + 1
        next_w = (ffn_w_in, ffn_w_out) + order[nxt] if nxt < len(order) else None
        out = _ffn(x, mods, lat_row, k0, g_pre, g_post, *w_ffn, next_w=next_w)
        if next_w is None:
            return out
        w_ffn = (out[1], out[2])
        return out[0]

    lat_row = lambda b: b
    ctx_row = lambda b: nb

    for i in range(depth):
        is_rec = (i % n_mixers) == 0
        j = i // n_mixers
        mods = mods_all[i]
        g = norm_g[i]
        w_pre = w_ffn
        x = ffn_lat(x, mods, 0, g[0], g[1], i, 0)
        if is_rec:
            ctx1 = _ffn(ctx.reshape(1, nb * lc, d), mods, ctx_row, 0, g[0], g[1],
                        *w_pre).reshape(nb, lc, d)
            w_rec = rec_w_in[j].astype(_BF16)
            g_lat, v_lat = _recin(x, mods, lat_row, 3, g[2], w_rec, rec_conv_w[j], rec_conv_b[j],
                                  tm=1024, blocklen=GRID_W, need_gate=True)
            _, v_ctx = _recin(ctx1, mods, ctx_row, 3, g[2], w_rec, rec_conv_w[j], rec_conv_b[j],
                              tm=lc, blocklen=lc, need_gate=False)
            gw = rec_gate_w[j]
            nh = gw.shape[2]
            w_cat = jnp.transpose(gw, (2, 3, 0, 1, 4)).reshape(nh, LRU_BLOCK, 4 * LRU_BLOCK)
            a = _scan(v_lat, v_ctx, g_lat, (0.5 * w_cat).astype(_BF16),
                      rec_gate_b[j].reshape(4, -1), rec_lam[j])
            w_mix = rec_w_out[j].astype(_BF16)
        else:
            cos_c, sin_c = _dft_tables(FGROUP)
            cs = jnp.asarray(np.concatenate([cos_c, sin_c], axis=1), _BF16)
            cos_s, sin_s = _dft_tables(s)
            seq_tab = jnp.asarray(np.concatenate([cos_s, -sin_s], axis=1), _BF16)
            xcs = _chdft(x, mods, 3, g[2], cs).reshape(nb, 2 * s, d)
            a = _seqdft(seq_tab, xcs, 1.0 / math.sqrt(s * FGROUP))
            w_mix = fou_w_out[j].astype(_BF16)
        x = _outproj(x, a, mods, 5, g[3], w_mix)
        x = ffn_lat(x, mods, 6, g[4], g[5], i, 1)
    return x
```

```python
import functools
import math

import jax
import jax.numpy as jnp
import numpy as np
from jax import lax
from jax.experimental import pallas as pl
from jax.experimental.pallas import tpu as pltpu

N_SUB = 3
N_MOD = 3 * N_SUB
MACARON = 0.5
GRID_W = 64
CONV_W = 4
CONV_LEFT = CONV_W // 2
LRU_BLOCK = 128
RG_C = 8.0
LOG2E = math.log2(math.e)
F32_TINY = float(np.finfo(np.float32).tiny)
FGROUP = 256
MIRROR_BLOCK = 128
EPS = 1e-6

V7X_SUBLANES = 8
ROW_CHUNK = 64
REC_ROW_CHUNK = 256
V7X_LANES = 128
MIB = 1024 * 1024

_BF16 = jnp.bfloat16
_F32 = jnp.float32


def _params(semantics, vmem_mib):
    return pltpu.CompilerParams(dimension_semantics=semantics,
                                vmem_limit_bytes=vmem_mib * MIB)


def _dot(a, b):
    return jnp.dot(a, b, preferred_element_type=_F32)


def _for_row_chunks(n_rows, fn, rn=ROW_CHUNK, unroll=2):
    def body(r, carry):
        fn(pl.ds(pl.multiple_of(r * rn, rn), rn))
        return carry
    lax.fori_loop(0, n_rows // rn, body, 0, unroll=unroll)


def _inv_rms(x):
    return lax.rsqrt(jnp.mean(x * x, axis=-1, keepdims=True) + EPS)


def _norm_mod_into(x_ref, h_ref, g, shift, scale):
    gm = g * (1.0 + scale)

    def fn(rows):
        inv = _inv_rms(x_ref[rows, :])
        h_ref[rows, :] = (x_ref[rows, :] * inv * gm + shift).astype(h_ref.dtype)
    _for_row_chunks(x_ref.shape[0], fn)


def _adaln_kernel(c_ref, w_ref, b_ref, o_ref):
    c = c_ref[...]
    a = (c * jax.nn.sigmoid(c)).astype(_BF16)
    o_ref[0] = _dot(a, w_ref[0].astype(_BF16)) + b_ref[0]


def _adaln(cc, mod_w, mod_b, tn=1024):
    depth, d, n = mod_w.shape
    rows = cc.shape[0]
    return pl.pallas_call(
        _adaln_kernel,
        grid=(depth, n // tn),
        in_specs=[pl.BlockSpec((rows, d), lambda l, j: (0, 0)),
                  pl.BlockSpec((1, d, tn), lambda l, j: (l, 0, j)),
                  pl.BlockSpec((1, 1, tn), lambda l, j: (l, 0, j))],
        out_specs=pl.BlockSpec((1, rows, tn), lambda l, j: (l, 0, j)),
        out_shape=jax.ShapeDtypeStruct((depth, rows, n), _F32),
        compiler_params=_params(("arbitrary", "arbitrary"), 40),
        name="adaln",
    )(cc, mod_w, mod_b.reshape(depth, 1, n))


def _ffn_kernel(x_ref, sh_ref, sc_ref, gt_ref, gpre_ref, gpost_ref, wg_ref, wu_ref, wo_ref,
                *rest, rc, nf, cast_next):
    if cast_next:
        nwi_ref, nwo_ref, o_ref, cwi_ref, cwo_ref, h_ref = rest
        cwi_ref[...] = nwi_ref[...].astype(_BF16)
        cwo_ref[...] = nwo_ref[...].astype(_BF16)
    else:
        o_ref, h_ref = rest
    j = pl.program_id(2)
    nj = pl.num_programs(2)
    tm = h_ref.shape[0]
    x2 = x_ref.at[0]
    o2 = o_ref.at[0]

    nr = tm // rc
    sub = [slice(q * ROW_CHUNK, (q + 1) * ROW_CHUNK) for q in range(tm // ROW_CHUNK)]
    per = rc // ROW_CHUNK

    def norm_rows(r, gm, shift):
        for rows in sub[r * per:(r + 1) * per]:
            inv = _inv_rms(x2[rows, :])
            h_ref[rows, :] = (x2[rows, :] * inv * gm + shift).astype(_BF16)

    def residual_rows(r, gg):
        for rows in sub[r * per:(r + 1) * per]:
            inv = _inv_rms(o2[rows, :])
            o2[rows, :] = x2[rows, :] + (o2[rows, :] * inv) * gg

    def step(first, last):
        if first:
            gm = gpre_ref[...] * (1.0 + sc_ref[0])
            shift = sh_ref[0]
            norm_rows(0, gm, shift)
        if last:
            gg = (MACARON * gt_ref[0]) * gpost_ref[...]
        for r in range(nr):
            rows = slice(r * rc, (r + 1) * rc)
            h = h_ref[rows, :]
            g = _dot(h, wg_ref[...])
            u = _dot(h, wu_ref[...])
            a = (g * jax.nn.sigmoid(g) * u).astype(_BF16)
            y = _dot(a, wo_ref[...])
            if first:
                o2[rows, :] = y
                if r + 1 < nr:
                    norm_rows(r + 1, gm, shift)
            else:
                o2[rows, :] += y
            if last and r > 0:
                residual_rows(r - 1, gg)
        if last:
            residual_rows(nr - 1, gg)

    if nf == 1:
        step(True, True)
    else:
        pl.when(j == 0)(lambda: step(True, False))
        pl.when(j == nj - 1)(lambda: step(False, True))
        if nf > 2:
            pl.when((j > 0) & (j < nj - 1))(lambda: step(False, False))


def _ffn(x, mods, row_of, k0, g_pre, g_post, w_in, w_out, next_w=None, tm=1024, tf=512, rc=256):
    nb, L, d = x.shape
    f = w_out.shape[0]
    nf = f // tf
    nt = nb * (L // tm)
    mod_spec = lambda k: pl.BlockSpec((1, 1, d), lambda b, i, j: (row_of(b), 0, k))
    vec_spec = pl.BlockSpec((1, d), lambda b, i, j: (0, 0))
    in_specs = [pl.BlockSpec((1, tm, d), lambda b, i, j: (b, i, 0)),
                mod_spec(k0), mod_spec(k0 + 1), mod_spec(k0 + 2),
                vec_spec, vec_spec,
                pl.BlockSpec((d, tf), lambda b, i, j: (0, j)),
                pl.BlockSpec((d, tf), lambda b, i, j: (0, nf + j)),
                pl.BlockSpec((tf, d), lambda b, i, j: (j, 0))]
    args = [x, mods, mods, mods, g_pre.reshape(1, d), g_post.reshape(1, d), w_in, w_in, w_out]
    out_specs = [pl.BlockSpec((1, tm, d), lambda b, i, j: (b, i, 0))]
    out_shape = [jax.ShapeDtypeStruct((nb, L, d), _F32)]
    if next_w is not None:
        nw_in, nw_out, layer, sub = next_w
        assert d % nt == 0 and (2 * f) % nf == 0 and f % (nt * nf) == 0
        ri, ci, ro = d // nt, 2 * f // nf, f // (nt * nf)
        tile = lambda i: i * (L // tm)
        in_specs += [pl.BlockSpec((None, None, ri, ci),
                                  lambda b, i, j: (layer, sub, tile(b) + i, j)),
                     pl.BlockSpec((None, None, ro, d),
                                  lambda b, i, j: (layer, sub, (tile(b) + i) * nf + j, 0))]
        args += [nw_in, nw_out]
        out_specs += [pl.BlockSpec((ri, ci), lambda b, i, j: (tile(b) + i, j)),
                      pl.BlockSpec((ro, d), lambda b, i, j: ((tile(b) + i) * nf + j, 0))]
        out_shape += [jax.ShapeDtypeStruct((d, 2 * f), _BF16), jax.ShapeDtypeStruct((f, d), _BF16)]
    out = pl.pallas_call(
        functools.partial(_ffn_kernel, rc=rc, nf=nf, cast_next=next_w is not None),
        grid=(nb, L // tm, nf),
        in_specs=in_specs, out_specs=out_specs, out_shape=out_shape,
        scratch_shapes=[pltpu.VMEM((tm, d), _BF16)],
        compiler_params=_params(("arbitrary", "arbitrary", "arbitrary"), 60),
        name="ffn",
    )(*args)
    return out if next_w is not None else out[0]


def _conv_rows(v, cw, cb, blocklen):
    tm = v.shape[0]
    pos = lax.broadcasted_iota(jnp.int32, (tm, 1), 0) % blocklen
    out = cb + cw[CONV_LEFT:CONV_LEFT + 1] * v
    for k in range(CONV_W):
        off = k - CONV_LEFT
        if off == 0:
            continue
        shifted = pltpu.roll(v, (-off) % tm, axis=0)
        valid = (pos + off >= 0) & (pos + off < blocklen)
        out = out + cw[k:k + 1] * jnp.where(valid, shifted, 0.0)
    return out


def _recin_kernel(x_ref, sh_ref, sc_ref, g_ref, wg_ref, wv_ref, cw_ref, cb_ref,
                  og_ref, ov_ref, h_ref, *, blocklen):
    n = pl.program_id(2)

    tm = h_ref.shape[0]
    rc = min(tm, REC_ROW_CHUNK)

    x2 = x_ref.at[0]
    nr = tm // rc

    def norm_rows(r, gm, shift):
        for q in range(rc // ROW_CHUNK):
            rows = slice(r * rc + q * ROW_CHUNK, r * rc + (q + 1) * ROW_CHUNK)
            inv = _inv_rms(x2[rows, :])
            h_ref[rows, :] = (x2[rows, :] * inv * gm + shift).astype(_BF16)

    def step(first):
        if first:
            gm = g_ref[...] * (1.0 + sc_ref[0])
            shift = sh_ref[0]
            norm_rows(0, gm, shift)
        for r in range(nr):
            rows = slice(r * rc, (r + 1) * rc)
            h = h_ref[rows, :]
            if og_ref is not None:
                og_ref[0, rows, :] = _gelu_tanh(_dot(h, wg_ref[...])).astype(_BF16)
            v = _dot(h, wv_ref[...])
            ov_ref[0, rows, :] = _conv_rows(v, cw_ref[...], cb_ref[...], blocklen).astype(_BF16)
            if first and r + 1 < nr:
                norm_rows(r + 1, gm, shift)

    pl.when(n == 0)(lambda: step(True))
    pl.when(n > 0)(lambda: step(False))


def _recin_nogate_kernel(x_ref, sh_ref, sc_ref, g_ref, wv_ref, cw_ref, cb_ref, ov_ref, h_ref,
                         *, blocklen):
    _recin_kernel(x_ref, sh_ref, sc_ref, g_ref, None, wv_ref, cw_ref, cb_ref, None, ov_ref,
                  h_ref, blocklen=blocklen)


def _recin(x, mods, row_of, k0, g, w_in, conv_w, conv_b, tm, blocklen, need_gate, tn=512):
    nb, L, d = x.shape
    r = w_in.shape[1] // 2
    nn = r // tn
    mod_spec = lambda k: pl.BlockSpec((1, 1, d), lambda b, i, n: (row_of(b), 0, k))
    x_spec = pl.BlockSpec((1, tm, d), lambda b, i, n: (b, i, 0))
    vec_spec = pl.BlockSpec((1, d), lambda b, i, n: (0, 0))
    wg_spec = pl.BlockSpec((d, tn), lambda b, i, n: (0, n))
    wv_spec = pl.BlockSpec((d, tn), lambda b, i, n: (0, nn + n))
    cw_spec = pl.BlockSpec((CONV_W, tn), lambda b, i, n: (0, n))
    cb_spec = pl.BlockSpec((1, tn), lambda b, i, n: (0, n))
    o_spec = pl.BlockSpec((1, tm, tn), lambda b, i, n: (b, i, n))
    o_shape = jax.ShapeDtypeStruct((nb, L, r), _BF16)
    common = dict(grid=(nb, L // tm, nn),
                  scratch_shapes=[pltpu.VMEM((tm, d), _BF16)],
                  compiler_params=_params(("arbitrary", "arbitrary", "arbitrary"), 56))
    g2 = g.reshape(1, d)
    cb2 = conv_b.reshape(1, r)
    if need_gate:
        return pl.pallas_call(
            functools.partial(_recin_kernel, blocklen=blocklen),
            in_specs=[x_spec, mod_spec(k0), mod_spec(k0 + 1), vec_spec, wg_spec, wv_spec,
                      cw_spec, cb_spec],
            out_specs=[o_spec, o_spec], out_shape=[o_shape, o_shape],
            name="rec_in", **common,
        )(x, mods, mods, g2, w_in, w_in, conv_w, cb2)
    return None, pl.pallas_call(
        functools.partial(_recin_nogate_kernel, blocklen=blocklen),
        in_specs=[x_spec, mod_spec(k0), mod_spec(k0 + 1), vec_spec, wv_spec, cw_spec, cb_spec],
        out_specs=o_spec, out_shape=o_shape,
        name="rec_in_ctx", **common,
    )(x, mods, mods, g2, w_in, conv_w, cb2)


def _gelu_tanh(x):
    c = math.sqrt(2.0 / math.pi)
    hx = 0.5 * x
    return hx + hx * jnp.tanh(x * (c + (0.044715 * c) * (x * x)))


def _seg_pitch(seg_len):
    assert seg_len % (2 * V7X_SUBLANES) == 0
    return seg_len + V7X_SUBLANES


def _scan_kernel(v_ref, vc_ref, g_ref, w_ref, bias_ref, lam_ref, o_ref,
                 a_sc, b_sc, hl_sc, al_sc, *, nh):
    lc = vc_ref.shape[1]
    s = v_ref.shape[1]
    nseg = V7X_SUBLANES
    seg, cseg = s // nseg, lc // nseg
    p, pc = _seg_pitch(seg), _seg_pitch(cseg)
    cb = nseg * p

    neg_lam = -lam_ref[...]
    sp = jnp.maximum(neg_lam, 0.0) + jnp.log(1.0 + jnp.exp(-jnp.abs(neg_lam)))
    q = (-0.5 * RG_C * LOG2E) * sp
    hb = 0.5 * bias_ref[...]

    def coeffs(vb, pre, hh, store):
        lanes = slice(hh * LRU_BLOCK, (hh + 1) * LRU_BLOCK)
        vh = 0.5 * vb.astype(_F32)
        for d in range(2):
            c0 = (2 * d) * LRU_BLOCK
            ta = jnp.tanh(pre[:, c0:c0 + LRU_BLOCK] + hb[2 * d:2 * d + 1, lanes])
            tx = jnp.tanh(pre[:, c0 + LRU_BLOCK:c0 + 2 * LRU_BLOCK]
                          + hb[2 * d + 1:2 * d + 2, lanes])
            qd = q[d:d + 1, lanes]
            a = jnp.exp2(qd + qd * ta)
            m2 = 1.0 - a * a
            mult = m2 * lax.rsqrt(jnp.maximum(m2, F32_TINY))
            store(d, a, (mult * vh) * (1.0 + tx))

    def lat_coeffs(m, carry):
        src = pl.ds(pl.multiple_of(m * seg, seg), seg)
        dst = pl.ds(pl.multiple_of(m * p, V7X_SUBLANES), seg)
        for hh in range(nh):
            vb = v_ref[0, src, hh * LRU_BLOCK:(hh + 1) * LRU_BLOCK]

            def store(d, a, b, hh=hh):
                a_sc[d, hh, dst, :] = a
                b_sc[d, hh, dst, :] = b
            coeffs(vb, _dot(vb, w_ref[hh]), hh, store)
        return carry

    lax.fori_loop(0, nseg, lat_coeffs, 0)

    for hh in range(nh):
        vb = vc_ref[0, :, hh * LRU_BLOCK:(hh + 1) * LRU_BLOCK]

        def store(d, a, b, hh=hh):
            for m in range(nseg):
                dst = slice(cb + m * pc, cb + m * pc + cseg)
                a_sc[d, hh, dst, :] = a[m * cseg:(m + 1) * cseg]
                b_sc[d, hh, dst, :] = b[m * cseg:(m + 1) * cseg]
        coeffs(vb, _dot(vb, w_ref[hh]), hh, store)

    vshape = (nseg, V7X_LANES)
    row = lax.broadcasted_iota(jnp.int32, vshape, 0)
    zeros = tuple(jnp.zeros(vshape, _F32) for _ in range(nh))
    ones = tuple(jnp.ones(vshape, _F32) for _ in range(nh))

    def sublane_scan(a, b, reverse):
        for k in (1, 2, 4):
            keep = (row < nseg - k) if reverse else (row >= k)
            shift = (nseg - k) if reverse else k
            a_sh = jnp.where(keep, pltpu.roll(a, shift, axis=0), 1.0)
            b_sh = jnp.where(keep, pltpu.roll(b, shift, axis=0), 0.0)
            b = a * b_sh + b
            a = a * a_sh
        return a, b

    def local_pass(base, pitch, length, save):
        def body(i, carry):
            hs, prods = carry
            new_h, new_p = [], []
            for d, k in ((0, i), (1, length - 1 - i)):
                idx = pl.ds(base + k, nseg, stride=pitch)
                t = pl.ds(pl.multiple_of(k * nseg, nseg), nseg)
                for hh in range(nh):
                    a = a_sc[d, hh, idx, :]
                    h = a * hs[d * nh + hh] + b_sc[d, hh, idx, :]
                    pr = prods[d * nh + hh] * a
                    if save:
                        hl_sc[d, hh, t, :] = h
                        al_sc[d, hh, t, :] = pr
                    new_h.append(h)
                    new_p.append(pr)
            return tuple(new_h), tuple(new_p)
        return lax.fori_loop(0, length, body, (zeros + zeros, ones + ones), unroll=8)

    def entry_states(h_fin, a_tot, g_in, reverse):
        acum, bcum = sublane_scan(a_tot, h_fin, reverse)
        e = acum * g_in + bcum
        if reverse:
            entry = jnp.where(row < nseg - 1, pltpu.roll(e, nseg - 1, axis=0), g_in)
            final = e[0:1, :]
        else:
            entry = jnp.where(row >= 1, pltpu.roll(e, 1, axis=0), g_in)
            final = e[nseg - 1:nseg, :]
        return entry, jnp.broadcast_to(final, vshape)

    hc, pc_tot = local_pass(cb, pc, cseg, save=False)
    hl, pl_tot = local_pass(0, p, seg, save=True)
    entries = []
    for d, reverse in ((0, False), (1, True)):
        for hh in range(nh):
            n = d * nh + hh
            _, g_ctx = entry_states(hc[n], pc_tot[n], zeros[hh], reverse)
            entry, _ = entry_states(hl[n], pl_tot[n], g_ctx, reverse)
            entries.append(entry)

    def fix(k, carry):
        t = pl.ds(pl.multiple_of(k * nseg, nseg), nseg)
        for hh in range(nh):
            hl_sc[0, hh, t, :] = ((hl_sc[0, hh, t, :] + al_sc[0, hh, t, :] * entries[hh])
                                  + (hl_sc[1, hh, t, :] + al_sc[1, hh, t, :] * entries[nh + hh]))
        return carry
    lax.fori_loop(0, seg, fix, 0, unroll=8)

    def gate_out(m, carry):
        rows = pl.ds(pl.multiple_of(m * seg, seg), seg)
        for hh in range(nh):
            lanes = slice(hh * LRU_BLOCK, (hh + 1) * LRU_BLOCK)
            y = hl_sc[0, hh, pl.ds(m, seg, stride=nseg), :]
            o_ref[0, rows, lanes] = (g_ref[0, rows, lanes].astype(_F32) * y).astype(_BF16)
        return carry

    lax.fori_loop(0, nseg, gate_out, 0)


def _scan(v_lat, v_ctx, g_lat, w_cat, bias, lam, cc=512):
    nb, s, r = v_lat.shape
    lc = v_ctx.shape[1]
    nh = cc // LRU_BLOCK
    nseg = V7X_SUBLANES
    coef_rows = nseg * (_seg_pitch(s // nseg) + _seg_pitch(lc // nseg))
    blk = lambda L: pl.BlockSpec((1, L, cc), lambda b, c: (b, 0, c))
    return pl.pallas_call(
        functools.partial(_scan_kernel, nh=nh),
        grid=(nb, r // cc),
        in_specs=[blk(s), blk(lc), blk(s),
                  pl.BlockSpec((nh, LRU_BLOCK, 4 * LRU_BLOCK), lambda b, c: (c, 0, 0)),
                  pl.BlockSpec((4, cc), lambda b, c: (0, c)),
                  pl.BlockSpec((2, cc), lambda b, c: (0, c))],
        out_specs=blk(s),
        out_shape=jax.ShapeDtypeStruct((nb, s, r), _BF16),
        scratch_shapes=[pltpu.VMEM((2, nh, coef_rows, V7X_LANES), _F32),
                        pltpu.VMEM((2, nh, coef_rows, V7X_LANES), _F32),
                        pltpu.VMEM((2, nh, s, V7X_LANES), _F32),
                        pltpu.VMEM((2, nh, s, V7X_LANES), _F32)],
        compiler_params=_params(("arbitrary", "arbitrary"), 56),
        name="rglru_scan",
    )(v_lat, v_ctx, g_lat, w_cat, bias, lam)


def _outproj_kernel(x_ref, *rest, rc, split):
    if split:
        alo_ref, ahi_ref, gt_ref, g_ref, w_ref, o_ref = rest
    else:
        alo_ref, gt_ref, g_ref, w_ref, o_ref = rest
    tm = o_ref.shape[1]
    x2 = x_ref.at[0]
    o2 = o_ref.at[0]

    def residual(r, gg):
        for q in range(rc // ROW_CHUNK):
            rows = slice(r * rc + q * ROW_CHUNK, r * rc + (q + 1) * ROW_CHUNK)
            inv = _inv_rms(o2[rows, :])
            o2[rows, :] = x2[rows, :] + (o2[rows, :] * inv) * gg

    def body(a_ref):
        gg = gt_ref[0] * g_ref[...]
        for r in range(tm // rc):
            rows = slice(r * rc, (r + 1) * rc)
            o2[rows, :] = _dot(a_ref[0, rows, :], w_ref[...])
            if r > 0:
                residual(r - 1, gg)
        residual(tm // rc - 1, gg)

    if split:
        i = pl.program_id(1)
        half = pl.num_programs(1) // 2
        pl.when(i < half)(lambda: body(alo_ref))
        pl.when(i >= half)(lambda: body(ahi_ref))
    else:
        body(alo_ref)


def _outproj(x, a, mods, k_gate, g, w, tm=512, rc=256):
    nb, L, d = x.shape
    split = isinstance(a, (tuple, list))
    r = w.shape[0]
    half = (L // tm) // 2
    if split:
        a_specs = [pl.BlockSpec((1, tm, r), lambda b, i: (b, jnp.minimum(i, half - 1), 0)),
                   pl.BlockSpec((1, tm, r), lambda b, i: (b, jnp.maximum(i - half, 0), 0))]
        a_args = list(a)
    else:
        a_specs = [pl.BlockSpec((1, tm, r), lambda b, i: (b, i, 0))]
        a_args = [a]
    return pl.pallas_call(
        functools.partial(_outproj_kernel, rc=rc, split=split),
        grid=(nb, L // tm),
        in_specs=[pl.BlockSpec((1, tm, d), lambda b, i: (b, i, 0))] + a_specs + [
            pl.BlockSpec((1, 1, d), lambda b, i: (b, 0, k_gate)),
            pl.BlockSpec((1, d), lambda b, i: (0, 0)),
            pl.BlockSpec((r, d), lambda b, i: (0, 0))],
        out_specs=pl.BlockSpec((1, tm, d), lambda b, i: (b, i, 0)),
        out_shape=jax.ShapeDtypeStruct((nb, L, d), _F32),
        compiler_params=_params(("arbitrary", "arbitrary"), 48),
        name="mixer_out",
    )(x, *a_args, mods, g.reshape(1, d), w)


def _dft_tables(n):
    k = np.arange(n, dtype=np.int64)
    ang = 2.0 * np.pi * ((k[:, None] * k[None, :]) % n).astype(np.float64) / n
    return np.cos(ang), np.sin(ang)


def _chdft_kernel(x_ref, sh_ref, sc_ref, g_ref, cs_ref, o_ref, h_ref):
    _norm_mod_into(x_ref.at[0], h_ref, g_ref[...], sh_ref[0], sc_ref[0])
    d = h_ref.shape[1]
    for gi in range(d // FGROUP):
        lanes = slice(gi * FGROUP, (gi + 1) * FGROUP)
        t = _dot(h_ref[:, lanes], cs_ref[...])
        o_ref[0, 0, :, lanes] = t[:, :FGROUP].astype(_BF16)
        o_ref[0, 1, :, lanes] = t[:, FGROUP:].astype(_BF16)


def _chdft(x, mods, k0, g, cs, tm=512):
    nb, L, d = x.shape
    mod_spec = lambda k: pl.BlockSpec((1, 1, d), lambda b, i: (b, 0, k))
    return pl.pallas_call(
        _chdft_kernel,
        grid=(nb, L // tm),
        in_specs=[pl.BlockSpec((1, tm, d), lambda b, i: (b, i, 0)),
                  mod_spec(k0), mod_spec(k0 + 1),
                  pl.BlockSpec((1, d), lambda b, i: (0, 0)),
                  pl.BlockSpec((FGROUP, 2 * FGROUP), lambda b, i: (0, 0))],
        out_specs=pl.BlockSpec((1, 2, tm, d), lambda b, i: (b, 0, i, 0)),
        out_shape=jax.ShapeDtypeStruct((nb, 2, L, d), _BF16),
        scratch_shapes=[pltpu.VMEM((tm, d), _BF16)],
        compiler_params=_params(("arbitrary", "arbitrary"), 40),
        name="fourier_channel_dft",
    )(x, mods, mods, g.reshape(1, d), cs)


def _seqdft_kernel(ac_ref, as_ref, ctop_ref, r_ref, bc_ref, bs_ref, lo_ref, hi_ref, h_sc,
                   *, scale, rc):
    ii = pl.program_id(2)
    tm = ac_ref.shape[0]
    mb = MIRROR_BLOCK

    @pl.when(ii == 0)
    def _():
        h_sc[tm:tm + mb, :] = (_dot(ctop_ref[...], bc_ref[...]) * scale).astype(_BF16)

    @pl.when(ii > 0)
    def _():
        h_sc[tm:tm + mb, :] = h_sc[0:mb, :]

    for r in range(tm // rc):
        rows = slice(r * rc, (r + 1) * rc)
        p = _dot(ac_ref[rows, :], bc_ref[...])
        q = _dot(as_ref[rows, :], bs_ref[...])
        lo_ref[0, rows, :] = ((p - q) * scale).astype(_BF16)
        h_sc[rows, :] = ((p + q) * scale).astype(_BF16)
    nblk = tm // mb
    for qb in range(nblk):
        src = h_sc[(nblk - 1 - qb) * mb:(nblk + 1 - qb) * mb, :]
        hi_ref[0, qb * mb:(qb + 1) * mb, :] = _dot(r_ref[...], src).astype(_BF16)


def _seqdft(xcs, scale, tm=512, tn=1024, rc=256):
    nb, _, L, d = xcs.shape
    half = L // 2
    ni = half // tm
    mb = MIRROR_BLOCK
    cos_s, sin_s = _dft_tables(L)
    a_cos = jnp.asarray(cos_s[:half], _BF16)
    a_sin = jnp.asarray(sin_s[:half], _BF16)
    c_top = jnp.asarray(cos_s[half:half + mb], _BF16)
    mirror = np.zeros((mb, 2 * mb), np.float32)
    mirror[np.arange(mb), mb - np.arange(mb)] = 1.0
    b_spec = lambda part: pl.BlockSpec((None, None, L, tn), lambda b, n, i: (b, part, 0, n))
    o_shape = jax.ShapeDtypeStruct((nb, half, d), _BF16)
    return pl.pallas_call(
        functools.partial(_seqdft_kernel, scale=scale, rc=rc),
        grid=(nb, d // tn, ni),
        in_specs=[pl.BlockSpec((tm, L), lambda b, n, i: (ni - 1 - i, 0)),
                  pl.BlockSpec((tm, L), lambda b, n, i: (ni - 1 - i, 0)),
                  pl.BlockSpec((mb, L), lambda b, n, i: (0, 0)),
                  pl.BlockSpec((mb, 2 * mb), lambda b, n, i: (0, 0)),
                  b_spec(0), b_spec(1)],
        out_specs=[pl.BlockSpec((1, tm, tn), lambda b, n, i: (b, ni - 1 - i, n)),
                   pl.BlockSpec((1, tm, tn), lambda b, n, i: (b, i, n))],
        out_shape=[o_shape, o_shape],
        scratch_shapes=[pltpu.VMEM((tm + mb, tn), _BF16)],
        compiler_params=_params(("arbitrary", "arbitrary", "arbitrary"), 48),
        name="fourier_seq_dft",
    )(a_cos, a_sin, c_top, jnp.asarray(mirror, _BF16), xcs, xcs)


def kernel(x, c, ctx, c_ctx, mod_w, mod_b, norm_g, ffn_w_in, ffn_w_out, rec_w_in, rec_conv_w,
           rec_conv_b, rec_gate_w, rec_gate_b, rec_lam, rec_w_out, fou_w_out):
    nb, s, d = x.shape
    lc = ctx.shape[1]
    depth = mod_w.shape[0]
    n_mixers = 2
    last_rec = ((depth - 1) // n_mixers) * n_mixers
    assert last_rec == 0, "context stream is only carried up to the first RG-LRU mixer"

    rows = 2 * V7X_SUBLANES
    cc = jnp.concatenate([c, c_ctx[None], jnp.zeros((rows - nb - 1, d), _F32)], axis=0)
    mods_all = _adaln(cc, mod_w, mod_b).reshape(depth, rows, 1, N_MOD * d)

    w_ffn = (ffn_w_in[0, 0].astype(_BF16), ffn_w_out[0, 0].astype(_BF16))
    order = [(i, sub) for i in range(depth) for sub in range(2)]

    def ffn_lat(x, mods, k0, g_pre, g_post, layer, sub):
        nonlocal w_ffn
        nxt = order.index((layer, sub)) + 1
        next_w = (ffn_w_in, ffn_w_out) + order[nxt] if nxt < len(order) else None
        out = _ffn(x, mods, lat_row, k0, g_pre, g_post, *w_ffn, next_w=next_w)
        if next_w is None:
            return out
        w_ffn = (out[1], out[2])
        return out[0]

    lat_row = lambda b: b
    ctx_row = lambda b: nb

    for i in range(depth):
        is_rec = (i % n_mixers) == 0
        j = i // n_mixers
        mods = mods_all[i]
        g = norm_g[i]
        w_pre = w_ffn
        x = ffn_lat(x, mods, 0, g[0], g[1], i, 0)
        if is_rec:
            ctx1 = _ffn(ctx.reshape(1, nb * lc, d), mods, ctx_row, 0, g[0], g[1],
                        *w_pre).reshape(nb, lc, d)
            w_rec = rec_w_in[j].astype(_BF16)
            g_lat, v_lat = _recin(x, mods, lat_row, 3, g[2], w_rec, rec_conv_w[j], rec_conv_b[j],
                                  tm=1024, blocklen=GRID_W, need_gate=True)
            _, v_ctx = _recin(ctx1, mods, ctx_row, 3, g[2], w_rec, rec_conv_w[j], rec_conv_b[j],
                              tm=lc, blocklen=lc, need_gate=False)
            gw = rec_gate_w[j]
            nh = gw.shape[2]
            w_cat = jnp.transpose(gw, (2, 3, 0, 1, 4)).reshape(nh, LRU_BLOCK, 4 * LRU_BLOCK)
            a = _scan(v_lat, v_ctx, g_lat, (0.5 * w_cat).astype(_BF16),
                      rec_gate_b[j].reshape(4, -1), rec_lam[j])
            w_mix = rec_w_out[j].astype(_BF16)
        else:
            cos_c, sin_c = _dft_tables(FGROUP)
            cs = jnp.asarray(np.concatenate([cos_c, sin_c], axis=1), _BF16)
            xcs = _chdft(x, mods, 3, g[2], cs)
            a = _seqdft(xcs, 1.0 / math.sqrt(s * FGROUP))
            w_mix = fou_w_out[j].astype(_BF16)
        x = _outproj(x, a, mods, 5, g[3], w_mix)
        x = ffn_lat(x, mods, 6, g[4], g[5], i, 1)
    return x
```

```python
import functools
import math

import jax
import jax.numpy as jnp
import numpy as np
from jax import lax
from jax.experimental import pallas as pl
from jax.experimental.pallas import tpu as pltpu

N_SUB = 3
N_MOD = 3 * N_SUB
MACARON = 0.5
GRID_W = 64
CONV_W = 4
CONV_LEFT = CONV_W // 2
LRU_BLOCK = 128
RG_C = 8.0
LOG2E = math.log2(math.e)
F32_TINY = float(np.finfo(np.float32).tiny)
FGROUP = 256
MIRROR_BLOCK = 128
EPS = 1e-6

V7X_SUBLANES = 8
ROW_CHUNK = 64
REC_ROW_CHUNK = 256
V7X_LANES = 128
MIB = 1024 * 1024

_BF16 = jnp.bfloat16
_F32 = jnp.float32


def _params(semantics, vmem_mib):
    return pltpu.CompilerParams(dimension_semantics=semantics,
                                vmem_limit_bytes=vmem_mib * MIB)


def _dot(a, b):
    return jnp.dot(a, b, preferred_element_type=_F32)


def _for_row_chunks(n_rows, fn, rn=ROW_CHUNK, unroll=2):
    def body(r, carry):
        fn(pl.ds(pl.multiple_of(r * rn, rn), rn))
        return carry
    lax.fori_loop(0, n_rows // rn, body, 0, unroll=unroll)


def _inv_rms(x):
    return lax.rsqrt(jnp.mean(x * x, axis=-1, keepdims=True) + EPS)


def _norm_mod_into(x_ref, h_ref, g, shift, scale):
    gm = g * (1.0 + scale)

    def fn(rows):
        inv = _inv_rms(x_ref[rows, :])
        h_ref[rows, :] = (x_ref[rows, :] * inv * gm + shift).astype(h_ref.dtype)
    _for_row_chunks(x_ref.shape[0], fn)


def _adaln_kernel(c_ref, w_ref, b_ref, o_ref):
    c = c_ref[...]
    a = (c * jax.nn.sigmoid(c)).astype(_BF16)
    o_ref[0] = _dot(a, w_ref[0].astype(_BF16)) + b_ref[0]


def _adaln(cc, mod_w, mod_b, tn=1024):
    depth, d, n = mod_w.shape
    rows = cc.shape[0]
    return pl.pallas_call(
        _adaln_kernel,
        grid=(depth, n // tn),
        in_specs=[pl.BlockSpec((rows, d), lambda l, j: (0, 0)),
                  pl.BlockSpec((1, d, tn), lambda l, j: (l, 0, j)),
                  pl.BlockSpec((1, 1, tn), lambda l, j: (l, 0, j))],
        out_specs=pl.BlockSpec((1, rows, tn), lambda l, j: (l, 0, j)),
        out_shape=jax.ShapeDtypeStruct((depth, rows, n), _F32),
        compiler_params=_params(("arbitrary", "arbitrary"), 40),
        name="adaln",
    )(cc, mod_w, mod_b.reshape(depth, 1, n))


def _ffn_kernel(x_ref, sh_ref, sc_ref, gt_ref, gpre_ref, gpost_ref, wg_ref, wu_ref, wo_ref,
                *rest, rc, nf, cast_next):
    if cast_next:
        nwi_ref, nwo_ref, o_ref, cwi_ref, cwo_ref, h_ref = rest
        cwi_ref[...] = nwi_ref[...].astype(_BF16)
        cwo_ref[...] = nwo_ref[...].astype(_BF16)
    else:
        o_ref, h_ref = rest
    j = pl.program_id(2)
    nj = pl.num_programs(2)
    tm = h_ref.shape[0]
    x2 = x_ref.at[0]
    o2 = o_ref.at[0]

    def sweep(chunk):
        start, size = chunk
        return [slice(q, q + ROW_CHUNK) for q in range(start, start + size, ROW_CHUNK)]

    def norm_rows(chunk, gm, shift):
        for rows in sweep(chunk):
            inv = _inv_rms(x2[rows, :])
            h_ref[rows, :] = (x2[rows, :] * inv * gm + shift).astype(_BF16)

    def residual_rows(chunk, gg):
        for rows in sweep(chunk):
            inv = _inv_rms(o2[rows, :])
            o2[rows, :] = x2[rows, :] + (o2[rows, :] * inv) * gg

    chunks = [(start, rc) for start in range(0, tm, rc)]

    def step(first, last):
        if first:
            gm = gpre_ref[...] * (1.0 + sc_ref[0])
            shift = sh_ref[0]
            norm_rows(chunks[0], gm, shift)
        if last:
            gg = (MACARON * gt_ref[0]) * gpost_ref[...]
        for r, (start, size) in enumerate(chunks):
            rows = slice(start, start + size)
            h = h_ref[rows, :]
            g = _dot(h, wg_ref[...])
            u = _dot(h, wu_ref[...])
            a = (g * jax.nn.sigmoid(g) * u).astype(_BF16)
            y = _dot(a, wo_ref[...])
            if first:
                o2[rows, :] = y
                if r + 1 < len(chunks):
                    norm_rows(chunks[r + 1], gm, shift)
            else:
                o2[rows, :] += y
            if last and r > 0:
                residual_rows(chunks[r - 1], gg)
        if last:
            residual_rows(chunks[-1], gg)

    if nf == 1:
        step(True, True)
    else:
        pl.when(j == 0)(lambda: step(True, False))
        pl.when(j == nj - 1)(lambda: step(False, True))
        if nf > 2:
            pl.when((j > 0) & (j < nj - 1))(lambda: step(False, False))


def _ffn(x, mods, row_of, k0, g_pre, g_post, w_in, w_out, next_w=None, tm=1024, tf=512, rc=512):
    nb, L, d = x.shape
    f = w_out.shape[0]
    nf = f // tf
    nt = nb * (L // tm)
    mod_spec = lambda k: pl.BlockSpec((1, 1, d), lambda b, i, j: (row_of(b), 0, k))
    vec_spec = pl.BlockSpec((1, d), lambda b, i, j: (0, 0))
    in_specs = [pl.BlockSpec((1, tm, d), lambda b, i, j: (b, i, 0)),
                mod_spec(k0), mod_spec(k0 + 1), mod_spec(k0 + 2),
                vec_spec, vec_spec,
                pl.BlockSpec((d, tf), lambda b, i, j: (0, j)),
                pl.BlockSpec((d, tf), lambda b, i, j: (0, nf + j)),
                pl.BlockSpec((tf, d), lambda b, i, j: (j, 0))]
    args = [x, mods, mods, mods, g_pre.reshape(1, d), g_post.reshape(1, d), w_in, w_in, w_out]
    out_specs = [pl.BlockSpec((1, tm, d), lambda b, i, j: (b, i, 0))]
    out_shape = [jax.ShapeDtypeStruct((nb, L, d), _F32)]
    if next_w is not None:
        nw_in, nw_out, layer, sub = next_w
        assert d % nt == 0 and (2 * f) % nf == 0 and f % (nt * nf) == 0
        ri, ci, ro = d // nt, 2 * f // nf, f // (nt * nf)
        tile = lambda i: i * (L // tm)
        in_specs += [pl.BlockSpec((None, None, ri, ci),
                                  lambda b, i, j: (layer, sub, tile(b) + i, j)),
                     pl.BlockSpec((None, None, ro, d),
                                  lambda b, i, j: (layer, sub, (tile(b) + i) * nf + j, 0))]
        args += [nw_in, nw_out]
        out_specs += [pl.BlockSpec((ri, ci), lambda b, i, j: (tile(b) + i, j)),
                      pl.BlockSpec((ro, d), lambda b, i, j: ((tile(b) + i) * nf + j, 0))]
        out_shape += [jax.ShapeDtypeStruct((d, 2 * f), _BF16), jax.ShapeDtypeStruct((f, d), _BF16)]
    out = pl.pallas_call(
        functools.partial(_ffn_kernel, rc=rc, nf=nf, cast_next=next_w is not None),
        grid=(nb, L // tm, nf),
        in_specs=in_specs, out_specs=out_specs, out_shape=out_shape,
        scratch_shapes=[pltpu.VMEM((tm, d), _BF16)],
        compiler_params=_params(("arbitrary", "arbitrary", "arbitrary"), 60),
        name="ffn",
    )(*args)
    return out if next_w is not None else out[0]


def _conv_rows(v, cw, cb, blocklen):
    tm = v.shape[0]
    pos = lax.broadcasted_iota(jnp.int32, (tm, 1), 0) % blocklen
    out = cb + cw[CONV_LEFT:CONV_LEFT + 1] * v
    for k in range(CONV_W):
        off = k - CONV_LEFT
        if off == 0:
            continue
        shifted = pltpu.roll(v, (-off) % tm, axis=0)
        valid = (pos + off >= 0) & (pos + off < blocklen)
        out = out + cw[k:k + 1] * jnp.where(valid, shifted, 0.0)
    return out


def _recin_kernel(x_ref, sh_ref, sc_ref, g_ref, wg_ref, wv_ref, cw_ref, cb_ref,
                  og_ref, ov_ref, h_ref, *, blocklen):
    n = pl.program_id(2)

    tm = h_ref.shape[0]
    rc = min(tm, REC_ROW_CHUNK)

    x2 = x_ref.at[0]
    nr = tm // rc

    def norm_rows(r, gm, shift):
        for q in range(rc // ROW_CHUNK):
            rows = slice(r * rc + q * ROW_CHUNK, r * rc + (q + 1) * ROW_CHUNK)
            inv = _inv_rms(x2[rows, :])
            h_ref[rows, :] = (x2[rows, :] * inv * gm + shift).astype(_BF16)

    def step(first):
        if first:
            gm = g_ref[...] * (1.0 + sc_ref[0])
            shift = sh_ref[0]
            norm_rows(0, gm, shift)
        for r in range(nr):
            rows = slice(r * rc, (r + 1) * rc)
            h = h_ref[rows, :]
            if og_ref is not None:
                og_ref[0, rows, :] = _gelu_tanh(_dot(h, wg_ref[...])).astype(_BF16)
            v = _dot(h, wv_ref[...])
            ov_ref[0, rows, :] = _conv_rows(v, cw_ref[...], cb_ref[...], blocklen).astype(_BF16)
            if first and r + 1 < nr:
                norm_rows(r + 1, gm, shift)

    pl.when(n == 0)(lambda: step(True))
    pl.when(n > 0)(lambda: step(False))


def _recin_nogate_kernel(x_ref, sh_ref, sc_ref, g_ref, wv_ref, cw_ref, cb_ref, ov_ref, h_ref,
                         *, blocklen):
    _recin_kernel(x_ref, sh_ref, sc_ref, g_ref, None, wv_ref, cw_ref, cb_ref, None, ov_ref,
                  h_ref, blocklen=blocklen)


def _recin(x, mods, row_of, k0, g, w_in, conv_w, conv_b, tm, blocklen, need_gate, tn=512):
    nb, L, d = x.shape
    r = w_in.shape[1] // 2
    nn = r // tn
    mod_spec = lambda k: pl.BlockSpec((1, 1, d), lambda b, i, n: (row_of(b), 0, k))
    x_spec = pl.BlockSpec((1, tm, d), lambda b, i, n: (b, i, 0))
    vec_spec = pl.BlockSpec((1, d), lambda b, i, n: (0, 0))
    wg_spec = pl.BlockSpec((d, tn), lambda b, i, n: (0, n))
    wv_spec = pl.BlockSpec((d, tn), lambda b, i, n: (0, nn + n))
    cw_spec = pl.BlockSpec((CONV_W, tn), lambda b, i, n: (0, n))
    cb_spec = pl.BlockSpec((1, tn), lambda b, i, n: (0, n))
    o_spec = pl.BlockSpec((1, tm, tn), lambda b, i, n: (b, i, n))
    o_shape = jax.ShapeDtypeStruct((nb, L, r), _BF16)
    common = dict(grid=(nb, L // tm, nn),
                  scratch_shapes=[pltpu.VMEM((tm, d), _BF16)],
                  compiler_params=_params(("arbitrary", "arbitrary", "arbitrary"), 56))
    g2 = g.reshape(1, d)
    cb2 = conv_b.reshape(1, r)
    if need_gate:
        return pl.pallas_call(
            functools.partial(_recin_kernel, blocklen=blocklen),
            in_specs=[x_spec, mod_spec(k0), mod_spec(k0 + 1), vec_spec, wg_spec, wv_spec,
                      cw_spec, cb_spec],
            out_specs=[o_spec, o_spec], out_shape=[o_shape, o_shape],
            name="rec_in", **common,
        )(x, mods, mods, g2, w_in, w_in, conv_w, cb2)
    return None, pl.pallas_call(
        functools.partial(_recin_nogate_kernel, blocklen=blocklen),
        in_specs=[x_spec, mod_spec(k0), mod_spec(k0 + 1), vec_spec, wv_spec, cw_spec, cb_spec],
        out_specs=o_spec, out_shape=o_shape,
        name="rec_in_ctx", **common,
    )(x, mods, mods, g2, w_in, conv_w, cb2)


def _gelu_tanh(x):
    c = math.sqrt(2.0 / math.pi)
    hx = 0.5 * x
    return hx + hx * jnp.tanh(x * (c + (0.044715 * c) * (x * x)))


def _scan_kernel(v_ref, vc_ref, g_ref, w_ref, bias_ref, lam_ref, o_ref,
                 a_sc, b_sc, hl_sc, al_sc, *, nh):
    lc = vc_ref.shape[1]
    s = v_ref.shape[1]
    nseg = V7X_SUBLANES
    seg, cseg = s // nseg, lc // nseg
    cb = s

    neg_lam = -lam_ref[...]
    sp = jnp.maximum(neg_lam, 0.0) + jnp.log(1.0 + jnp.exp(-jnp.abs(neg_lam)))
    q = (-0.5 * RG_C * LOG2E) * sp
    hb = 0.5 * bias_ref[...]

    def coeffs(vb, pre, hh, store):
        lanes = slice(hh * LRU_BLOCK, (hh + 1) * LRU_BLOCK)
        vh = 0.5 * vb.astype(_F32)
        for d in range(2):
            c0 = (2 * d) * LRU_BLOCK
            ta = jnp.tanh(pre[:, c0:c0 + LRU_BLOCK] + hb[2 * d:2 * d + 1, lanes])
            tx = jnp.tanh(pre[:, c0 + LRU_BLOCK:c0 + 2 * LRU_BLOCK]
                          + hb[2 * d + 1:2 * d + 2, lanes])
            qd = q[d:d + 1, lanes]
            a = jnp.exp2(qd + qd * ta)
            m2 = 1.0 - a * a
            mult = m2 * lax.rsqrt(jnp.maximum(m2, F32_TINY))
            store(d, a, (mult * vh) * (1.0 + tx))

    def lat_coeffs(m, carry):
        src = pl.ds(pl.multiple_of(m * seg, seg), seg)
        dst = pl.ds(m, seg, stride=nseg)
        for hh in range(nh):
            vb = v_ref[0, src, hh * LRU_BLOCK:(hh + 1) * LRU_BLOCK]

            def store(d, a, b, hh=hh):
                a_sc[d, hh, dst, :] = a
                b_sc[d, hh, dst, :] = b
            coeffs(vb, _dot(vb, w_ref[hh]), hh, store)
        return carry

    lax.fori_loop(0, nseg, lat_coeffs, 0)

    for hh in range(nh):
        vb = vc_ref[0, :, hh * LRU_BLOCK:(hh + 1) * LRU_BLOCK]

        def store(d, a, b, hh=hh):
            for m in range(nseg):
                dst = pl.ds(cb + m, cseg, stride=nseg)
                a_sc[d, hh, dst, :] = a[m * cseg:(m + 1) * cseg]
                b_sc[d, hh, dst, :] = b[m * cseg:(m + 1) * cseg]
        coeffs(vb, _dot(vb, w_ref[hh]), hh, store)

    vshape = (nseg, V7X_LANES)
    row = lax.broadcasted_iota(jnp.int32, vshape, 0)
    zeros = tuple(jnp.zeros(vshape, _F32) for _ in range(nh))
    ones = tuple(jnp.ones(vshape, _F32) for _ in range(nh))

    def sublane_scan(a, b, reverse):
        for k in (1, 2, 4):
            keep = (row < nseg - k) if reverse else (row >= k)
            shift = (nseg - k) if reverse else k
            a_sh = jnp.where(keep, pltpu.roll(a, shift, axis=0), 1.0)
            b_sh = jnp.where(keep, pltpu.roll(b, shift, axis=0), 0.0)
            b = a * b_sh + b
            a = a * a_sh
        return a, b

    def local_pass(base, length, save):
        def body(i, carry):
            hs, prods = carry
            new_h, new_p = [], []
            for d, k in ((0, i), (1, length - 1 - i)):
                t = pl.ds(pl.multiple_of(k * nseg, nseg), nseg)
                idx = pl.ds(pl.multiple_of(base + k * nseg, nseg), nseg)
                for hh in range(nh):
                    a = a_sc[d, hh, idx, :]
                    h = a * hs[d * nh + hh] + b_sc[d, hh, idx, :]
                    pr = prods[d * nh + hh] * a
                    if save:
                        hl_sc[d, hh, t, :] = h
                        al_sc[d, hh, t, :] = pr
                    new_h.append(h)
                    new_p.append(pr)
            return tuple(new_h), tuple(new_p)
        return lax.fori_loop(0, length, body, (zeros + zeros, ones + ones), unroll=8)

    def entry_states(h_fin, a_tot, g_in, reverse):
        acum, bcum = sublane_scan(a_tot, h_fin, reverse)
        e = acum * g_in + bcum
        if reverse:
            entry = jnp.where(row < nseg - 1, pltpu.roll(e, nseg - 1, axis=0), g_in)
            final = e[0:1, :]
        else:
            entry = jnp.where(row >= 1, pltpu.roll(e, 1, axis=0), g_in)
            final = e[nseg - 1:nseg, :]
        return entry, jnp.broadcast_to(final, vshape)

    hc, pc_tot = local_pass(cb, cseg, save=False)
    hl, pl_tot = local_pass(0, seg, save=True)
    entries = []
    for d, reverse in ((0, False), (1, True)):
        for hh in range(nh):
            n = d * nh + hh
            _, g_ctx = entry_states(hc[n], pc_tot[n], zeros[hh], reverse)
            entry, _ = entry_states(hl[n], pl_tot[n], g_ctx, reverse)
            entries.append(entry)

    def fix(k, carry):
        t = pl.ds(pl.multiple_of(k * nseg, nseg), nseg)
        for hh in range(nh):
            hl_sc[0, hh, t, :] = ((hl_sc[0, hh, t, :] + al_sc[0, hh, t, :] * entries[hh])
                                  + (hl_sc[1, hh, t, :] + al_sc[1, hh, t, :] * entries[nh + hh]))
        return carry
    lax.fori_loop(0, seg, fix, 0, unroll=8)

    def gate_out(m, carry):
        rows = pl.ds(pl.multiple_of(m * seg, seg), seg)
        for hh in range(nh):
            lanes = slice(hh * LRU_BLOCK, (hh + 1) * LRU_BLOCK)
            y = hl_sc[0, hh, pl.ds(m, seg, stride=nseg), :]
            o_ref[0, rows, lanes] = (g_ref[0, rows, lanes].astype(_F32) * y).astype(_BF16)
        return carry

    lax.fori_loop(0, nseg, gate_out, 0)


def _scan(v_lat, v_ctx, g_lat, w_cat, bias, lam, cc=512):
    nb, s, r = v_lat.shape
    lc = v_ctx.shape[1]
    nh = cc // LRU_BLOCK
    coef_rows = s + lc
    blk = lambda L: pl.BlockSpec((1, L, cc), lambda b, c: (b, 0, c))
    return pl.pallas_call(
        functools.partial(_scan_kernel, nh=nh),
        grid=(nb, r // cc),
        in_specs=[blk(s), blk(lc), blk(s),
                  pl.BlockSpec((nh, LRU_BLOCK, 4 * LRU_BLOCK), lambda b, c: (c, 0, 0)),
                  pl.BlockSpec((4, cc), lambda b, c: (0, c)),
                  pl.BlockSpec((2, cc), lambda b, c: (0, c))],
        out_specs=blk(s),
        out_shape=jax.ShapeDtypeStruct((nb, s, r), _BF16),
        scratch_shapes=[pltpu.VMEM((2, nh, coef_rows, V7X_LANES), _F32),
                        pltpu.VMEM((2, nh, coef_rows, V7X_LANES), _F32),
                        pltpu.VMEM((2, nh, s, V7X_LANES), _F32),
                        pltpu.VMEM((2, nh, s, V7X_LANES), _F32)],
        compiler_params=_params(("arbitrary", "arbitrary"), 56),
        name="rglru_scan",
    )(v_lat, v_ctx, g_lat, w_cat, bias, lam)


def _outproj_kernel(x_ref, *rest, rc, split):
    if split:
        alo_ref, ahi_ref, gt_ref, g_ref, w_ref, o_ref = rest
    else:
        alo_ref, gt_ref, g_ref, w_ref, o_ref = rest
    tm = o_ref.shape[1]
    x2 = x_ref.at[0]
    o2 = o_ref.at[0]

    def residual(r, gg):
        for q in range(rc // ROW_CHUNK):
            rows = slice(r * rc + q * ROW_CHUNK, r * rc + (q + 1) * ROW_CHUNK)
            inv = _inv_rms(o2[rows, :])
            o2[rows, :] = x2[rows, :] + (o2[rows, :] * inv) * gg

    def body(a_ref):
        gg = gt_ref[0] * g_ref[...]
        for r in range(tm // rc):
            rows = slice(r * rc, (r + 1) * rc)
            o2[rows, :] = _dot(a_ref[0, rows, :], w_ref[...])
            if r > 0:
                residual(r - 1, gg)
        residual(tm // rc - 1, gg)

    if split:
        i = pl.program_id(1)
        half = pl.num_programs(1) // 2
        pl.when(i < half)(lambda: body(alo_ref))
        pl.when(i >= half)(lambda: body(ahi_ref))
    else:
        body(alo_ref)


def _outproj(x, a, mods, k_gate, g, w, tm=512, rc=256):
    nb, L, d = x.shape
    split = isinstance(a, (tuple, list))
    r = w.shape[0]
    half = (L // tm) // 2
    if split:
        a_specs = [pl.BlockSpec((1, tm, r), lambda b, i: (b, jnp.minimum(i, half - 1), 0)),
                   pl.BlockSpec((1, tm, r), lambda b, i: (b, jnp.maximum(i - half, 0), 0))]
        a_args = list(a)
    else:
        a_specs = [pl.BlockSpec((1, tm, r), lambda b, i: (b, i, 0))]
        a_args = [a]
    return pl.pallas_call(
        functools.partial(_outproj_kernel, rc=rc, split=split),
        grid=(nb, L // tm),
        in_specs=[pl.BlockSpec((1, tm, d), lambda b, i: (b, i, 0))] + a_specs + [
            pl.BlockSpec((1, 1, d), lambda b, i: (b, 0, k_gate)),
            pl.BlockSpec((1, d), lambda b, i: (0, 0)),
            pl.BlockSpec((r, d), lambda b, i: (0, 0))],
        out_specs=pl.BlockSpec((1, tm, d), lambda b, i: (b, i, 0)),
        out_shape=jax.ShapeDtypeStruct((nb, L, d), _F32),
        compiler_params=_params(("arbitrary", "arbitrary"), 48),
        name="mixer_out",
    )(x, *a_args, mods, g.reshape(1, d), w)


def _dft_tables(n):
    k = np.arange(n, dtype=np.int64)
    ang = 2.0 * np.pi * ((k[:, None] * k[None, :]) % n).astype(np.float64) / n
    return np.cos(ang), np.sin(ang)


def _chdft_kernel(x_ref, sh_ref, sc_ref, g_ref, cs_ref, o_ref, h_ref):
    _norm_mod_into(x_ref.at[0], h_ref, g_ref[...], sh_ref[0], sc_ref[0])
    d = h_ref.shape[1]
    for gi in range(d // FGROUP):
        lanes = slice(gi * FGROUP, (gi + 1) * FGROUP)
        t = _dot(h_ref[:, lanes], cs_ref[...])
        o_ref[0, 0, :, lanes] = t[:, :FGROUP].astype(_BF16)
        o_ref[0, 1, :, lanes] = t[:, FGROUP:].astype(_BF16)


def _chdft(x, mods, k0, g, cs, tm=512):
    nb, L, d = x.shape
    mod_spec = lambda k: pl.BlockSpec((1, 1, d), lambda b, i: (b, 0, k))
    return pl.pallas_call(
        _chdft_kernel,
        grid=(nb, L // tm),
        in_specs=[pl.BlockSpec((1, tm, d), lambda b, i: (b, i, 0)),
                  mod_spec(k0), mod_spec(k0 + 1),
                  pl.BlockSpec((1, d), lambda b, i: (0, 0)),
                  pl.BlockSpec((FGROUP, 2 * FGROUP), lambda b, i: (0, 0))],
        out_specs=pl.BlockSpec((1, 2, tm, d), lambda b, i: (b, 0, i, 0)),
        out_shape=jax.ShapeDtypeStruct((nb, 2, L, d), _BF16),
        scratch_shapes=[pltpu.VMEM((tm, d), _BF16)],
        compiler_params=_params(("arbitrary", "arbitrary"), 40),
        name="fourier_channel_dft",
    )(x, mods, mods, g.reshape(1, d), cs)


def _seqdft_kernel(ac_ref, as_ref, ctop_ref, r_ref, bc_ref, bs_ref, lo_ref, hi_ref, h_sc,
                   *, scale, rc):
    ii = pl.program_id(2)
    tm = ac_ref.shape[0]
    mb = MIRROR_BLOCK

    @pl.when(ii == 0)
    def _():
        h_sc[tm:tm + mb, :] = (_dot(ctop_ref[...], bc_ref[...]) * scale).astype(_BF16)

    @pl.when(ii > 0)
    def _():
        h_sc[tm:tm + mb, :] = h_sc[0:mb, :]

    for r in range(tm // rc):
        rows = slice(r * rc, (r + 1) * rc)
        p = _dot(ac_ref[rows, :], bc_ref[...])
        q = _dot(as_ref[rows, :], bs_ref[...])
        lo_ref[0, rows, :] = ((p - q) * scale).astype(_BF16)
        h_sc[rows, :] = ((p + q) * scale).astype(_BF16)
    nblk = tm // mb
    for qb in range(nblk):
        src = h_sc[(nblk - 1 - qb) * mb:(nblk + 1 - qb) * mb, :]
        hi_ref[0, qb * mb:(qb + 1) * mb, :] = _dot(r_ref[...], src).astype(_BF16)


def _seqdft(xcs, scale, tm=512, tn=1024, rc=256):
    nb, _, L, d = xcs.shape
    half = L // 2
    ni = half // tm
    mb = MIRROR_BLOCK
    cos_s, sin_s = _dft_tables(L)
    a_cos = jnp.asarray(cos_s[:half], _BF16)
    a_sin = jnp.asarray(sin_s[:half], _BF16)
    c_top = jnp.asarray(cos_s[half:half + mb], _BF16)
    mirror = np.zeros((mb, 2 * mb), np.float32)
    mirror[np.arange(mb), mb - np.arange(mb)] = 1.0
    b_spec = lambda part: pl.BlockSpec((None, None, L, tn), lambda b, n, i: (b, part, 0, n))
    o_shape = jax.ShapeDtypeStruct((nb, half, d), _BF16)
    return pl.pallas_call(
        functools.partial(_seqdft_kernel, scale=scale, rc=rc),
        grid=(nb, d // tn, ni),
        in_specs=[pl.BlockSpec((tm, L), lambda b, n, i: (ni - 1 - i, 0)),
                  pl.BlockSpec((tm, L), lambda b, n, i: (ni - 1 - i, 0)),
                  pl.BlockSpec((mb, L), lambda b, n, i: (0, 0)),
                  pl.BlockSpec((mb, 2 * mb), lambda b, n, i: (0, 0)),
                  b_spec(0), b_spec(1)],
        out_specs=[pl.BlockSpec((1, tm, tn), lambda b, n, i: (b, ni - 1 - i, n)),
                   pl.BlockSpec((1, tm, tn), lambda b, n, i: (b, i, n))],
        out_shape=[o_shape, o_shape],
        scratch_shapes=[pltpu.VMEM((tm + mb, tn), _BF16)],
        compiler_params=_params(("arbitrary", "arbitrary", "arbitrary"), 48),
        name="fourier_seq_dft",
    )(a_cos, a_sin, c_top, jnp.asarray(mirror, _BF16), xcs, xcs)


def kernel(x, c, ctx, c_ctx, mod_w, mod_b, norm_g, ffn_w_in, ffn_w_out, rec_w_in, rec_conv_w,
           rec_conv_b, rec_gate_w, rec_gate_b, rec_lam, rec_w_out, fou_w_out):
    nb, s, d = x.shape
    lc = ctx.shape[1]
    depth = mod_w.shape[0]
    n_mixers = 2
    last_rec = ((depth - 1) // n_mixers) * n_mixers
    assert last_rec == 0, "context stream is only carried up to the first RG-LRU mixer"

    rows = 2 * V7X_SUBLANES
    cc = jnp.concatenate([c, c_ctx[None], jnp.zeros((rows - nb - 1, d), _F32)], axis=0)
    mods_all = _adaln(cc, mod_w, mod_b).reshape(depth, rows, 1, N_MOD * d)

    w_ffn = (ffn_w_in[0, 0].astype(_BF16), ffn_w_out[0, 0].astype(_BF16))
    order = [(i, sub) for i in range(depth) for sub in range(2)]

    def ffn_lat(x, mods, k0, g_pre, g_post, layer, sub):
        nonlocal w_ffn
        nxt = order.index((layer, sub)) + 1
        next_w = (ffn_w_in, ffn_w_out) + order[nxt] if nxt < len(order) else None
        out = _ffn(x, mods, lat_row, k0, g_pre, g_post, *w_ffn, next_w=next_w)
        if next_w is None:
            return out
        w_ffn = (out[1], out[2])
        return out[0]

    lat_row = lambda b: b
    ctx_row = lambda b: nb

    for i in range(depth):
        is_rec = (i % n_mixers) == 0
        j = i // n_mixers
        mods = mods_all[i]
        g = norm_g[i]
        w_pre = w_ffn
        x = ffn_lat(x, mods, 0, g[0], g[1], i, 0)
        if is_rec:
            ctx1 = _ffn(ctx.reshape(1, nb * lc, d), mods, ctx_row, 0, g[0], g[1],
                        *w_pre).reshape(nb, lc, d)
            w_rec = rec_w_in[j].astype(_BF16)
            g_lat, v_lat = _recin(x, mods, lat_row, 3, g[2], w_rec, rec_conv_w[j], rec_conv_b[j],
                                  tm=1024, blocklen=GRID_W, need_gate=True)
            _, v_ctx = _recin(ctx1, mods, ctx_row, 3, g[2], w_rec, rec_conv_w[j], rec_conv_b[j],
                              tm=lc, blocklen=lc, need_gate=False)
            gw = rec_gate_w[j]
            nh = gw.shape[2]
            w_cat = jnp.transpose(gw, (2, 3, 0, 1, 4)).reshape(nh, LRU_BLOCK, 4 * LRU_BLOCK)
            a = _scan(v_lat, v_ctx, g_lat, (0.5 * w_cat).astype(_BF16),
                      rec_gate_b[j].reshape(4, -1), rec_lam[j])
            w_mix = rec_w_out[j].astype(_BF16)
        else:
            cos_c, sin_c = _dft_tables(FGROUP)
            cs = jnp.asarray(np.concatenate([cos_c, sin_c], axis=1), _BF16)
            xcs = _chdft(x, mods, 3, g[2], cs)
            a = _seqdft(xcs, 1.0 / math.sqrt(s * FGROUP))
            w_mix = fou_w_out[j].astype(_BF16)
        x = _outproj(x, a, mods, 5, g[3], w_mix)
        x = ffn_lat(x, mods, 6, g[4], g[5], i, 1)
    return x
```

```python
import functools
import math

import jax
import jax.numpy as jnp
import numpy as np
from jax import lax
from jax.experimental import pallas as pl
from jax.experimental.pallas import tpu as pltpu

N_SUB = 3
N_MOD = 3 * N_SUB
MACARON = 0.5
GRID_W = 64
CONV_W = 4
CONV_LEFT = CONV_W // 2
LRU_BLOCK = 128
RG_C = 8.0
LOG2E = math.log2(math.e)
F32_TINY = float(np.finfo(np.float32).tiny)
FGROUP = 256
MIRROR_BLOCK = 128
EPS = 1e-6

V7X_SUBLANES = 8
ROW_CHUNK = 64
REC_ROW_CHUNK = 256
V7X_LANES = 128
MIB = 1024 * 1024

_BF16 = jnp.bfloat16
_F32 = jnp.float32


def _params(semantics, vmem_mib):
    return pltpu.CompilerParams(dimension_semantics=semantics,
                                vmem_limit_bytes=vmem_mib * MIB)


def _dot(a, b):
    return jnp.dot(a, b, preferred_element_type=_F32)


def _for_row_chunks(n_rows, fn, rn=ROW_CHUNK, unroll=2):
    def body(r, carry):
        fn(pl.ds(pl.multiple_of(r * rn, rn), rn))
        return carry
    lax.fori_loop(0, n_rows // rn, body, 0, unroll=unroll)


def _inv_rms(x):
    return lax.rsqrt(jnp.mean(x * x, axis=-1, keepdims=True) + EPS)


def _norm_mod_into(x_ref, h_ref, g, shift, scale):
    gm = g * (1.0 + scale)

    def fn(rows):
        inv = _inv_rms(x_ref[rows, :])
        h_ref[rows, :] = (x_ref[rows, :] * inv * gm + shift).astype(h_ref.dtype)
    _for_row_chunks(x_ref.shape[0], fn)


def _adaln_kernel(c_ref, w_ref, b_ref, o_ref):
    c = c_ref[...]
    a = (c * jax.nn.sigmoid(c)).astype(_BF16)
    o_ref[0] = _dot(a, w_ref[0].astype(_BF16)) + b_ref[0]


def _adaln(cc, mod_w, mod_b, tn=1024):
    depth, d, n = mod_w.shape
    rows = cc.shape[0]
    return pl.pallas_call(
        _adaln_kernel,
        grid=(depth, n // tn),
        in_specs=[pl.BlockSpec((rows, d), lambda l, j: (0, 0)),
                  pl.BlockSpec((1, d, tn), lambda l, j: (l, 0, j)),
                  pl.BlockSpec((1, 1, tn), lambda l, j: (l, 0, j))],
        out_specs=pl.BlockSpec((1, rows, tn), lambda l, j: (l, 0, j)),
        out_shape=jax.ShapeDtypeStruct((depth, rows, n), _F32),
        compiler_params=_params(("arbitrary", "arbitrary"), 40),
        name="adaln",
    )(cc, mod_w, mod_b.reshape(depth, 1, n))


def _ffn_kernel(x_ref, sh_ref, sc_ref, gt_ref, gpre_ref, gpost_ref, wg_ref, wu_ref, wo_ref,
                *rest, rc, rc_mid, nf, cast_next):
    if cast_next:
        nwi_ref, nwo_ref, o_ref, cwi_ref, cwo_ref, h_ref = rest
        cwi_ref[...] = nwi_ref[...].astype(_BF16)
        cwo_ref[...] = nwo_ref[...].astype(_BF16)
    else:
        o_ref, h_ref = rest
    j = pl.program_id(2)
    nj = pl.num_programs(2)
    tm = h_ref.shape[0]
    x2 = x_ref.at[0]
    o2 = o_ref.at[0]

    def sweep(chunk):
        start, size = chunk
        return [slice(q, q + ROW_CHUNK) for q in range(start, start + size, ROW_CHUNK)]

    def norm_rows(chunk, gm, shift):
        for rows in sweep(chunk):
            inv = _inv_rms(x2[rows, :])
            h_ref[rows, :] = (x2[rows, :] * inv * gm + shift).astype(_BF16)

    def residual_rows(chunk, gg):
        for rows in sweep(chunk):
            inv = _inv_rms(o2[rows, :])
            o2[rows, :] = x2[rows, :] + (o2[rows, :] * inv) * gg

    def step(first, last):
        size = rc if (first or last) else rc_mid
        chunks = [(start, size) for start in range(0, tm, size)]
        if first:
            gm = gpre_ref[...] * (1.0 + sc_ref[0])
            shift = sh_ref[0]
            norm_rows(chunks[0], gm, shift)
        if last:
            gg = (MACARON * gt_ref[0]) * gpost_ref[...]
        for r, (start, size) in enumerate(chunks):
            rows = slice(start, start + size)
            h = h_ref[rows, :]
            g = _dot(h, wg_ref[...])
            u = _dot(h, wu_ref[...])
            a = (g * jax.nn.sigmoid(g) * u).astype(_BF16)
            y = _dot(a, wo_ref[...])
            if first:
                o2[rows, :] = y
                if r + 1 < len(chunks):
                    norm_rows(chunks[r + 1], gm, shift)
            else:
                o2[rows, :] += y
            if last and r > 0:
                residual_rows(chunks[r - 1], gg)
        if last:
            residual_rows(chunks[-1], gg)

    if nf == 1:
        step(True, True)
    else:
        pl.when(j == 0)(lambda: step(True, False))
        pl.when(j == nj - 1)(lambda: step(False, True))
        if nf > 2:
            pl.when((j > 0) & (j < nj - 1))(lambda: step(False, False))


def _ffn(x, mods, row_of, k0, g_pre, g_post, w_in, w_out, next_w=None, tm=1024, tf=512, rc=512,
         rc_mid=1024):
    nb, L, d = x.shape
    f = w_out.shape[0]
    nf = f // tf
    nt = nb * (L // tm)
    mod_spec = lambda k: pl.BlockSpec((1, 1, d), lambda b, i, j: (row_of(b), 0, k))
    vec_spec = pl.BlockSpec((1, d), lambda b, i, j: (0, 0))
    in_specs = [pl.BlockSpec((1, tm, d), lambda b, i, j: (b, i, 0)),
                mod_spec(k0), mod_spec(k0 + 1), mod_spec(k0 + 2),
                vec_spec, vec_spec,
                pl.BlockSpec((d, tf), lambda b, i, j: (0, j)),
                pl.BlockSpec((d, tf), lambda b, i, j: (0, nf + j)),
                pl.BlockSpec((tf, d), lambda b, i, j: (j, 0))]
    args = [x, mods, mods, mods, g_pre.reshape(1, d), g_post.reshape(1, d), w_in, w_in, w_out]
    out_specs = [pl.BlockSpec((1, tm, d), lambda b, i, j: (b, i, 0))]
    out_shape = [jax.ShapeDtypeStruct((nb, L, d), _F32)]
    if next_w is not None:
        nw_in, nw_out, layer, sub = next_w
        assert d % nt == 0 and (2 * f) % nf == 0 and f % (nt * nf) == 0
        ri, ci, ro = d // nt, 2 * f // nf, f // (nt * nf)
        tile = lambda i: i * (L // tm)
        in_specs += [pl.BlockSpec((None, None, ri, ci),
                                  lambda b, i, j: (layer, sub, tile(b) + i, j)),
                     pl.BlockSpec((None, None, ro, d),
                                  lambda b, i, j: (layer, sub, (tile(b) + i) * nf + j, 0))]
        args += [nw_in, nw_out]
        out_specs += [pl.BlockSpec((ri, ci), lambda b, i, j: (tile(b) + i, j)),
                      pl.BlockSpec((ro, d), lambda b, i, j: ((tile(b) + i) * nf + j, 0))]
        out_shape += [jax.ShapeDtypeStruct((d, 2 * f), _BF16), jax.ShapeDtypeStruct((f, d), _BF16)]
    out = pl.pallas_call(
        functools.partial(_ffn_kernel, rc=rc, rc_mid=rc_mid, nf=nf,
                          cast_next=next_w is not None),
        grid=(nb, L // tm, nf),
        in_specs=in_specs, out_specs=out_specs, out_shape=out_shape,
        scratch_shapes=[pltpu.VMEM((tm, d), _BF16)],
        compiler_params=_params(("arbitrary", "arbitrary", "arbitrary"), 60),
        name="ffn",
    )(*args)
    return out if next_w is not None else out[0]


def _conv_rows(v, cw, cb, blocklen):
    tm = v.shape[0]
    pos = lax.broadcasted_iota(jnp.int32, (tm, 1), 0) % blocklen
    out = cb + cw[CONV_LEFT:CONV_LEFT + 1] * v
    for k in range(CONV_W):
        off = k - CONV_LEFT
        if off == 0:
            continue
        shifted = pltpu.roll(v, (-off) % tm, axis=0)
        valid = (pos + off >= 0) & (pos + off < blocklen)
        out = out + cw[k:k + 1] * jnp.where(valid, shifted, 0.0)
    return out


def _recin_kernel(x_ref, sh_ref, sc_ref, g_ref, wg_ref, wv_ref, cw_ref, cb_ref,
                  og_ref, ov_ref, h_ref, *, blocklen):
    n = pl.program_id(2)

    tm = h_ref.shape[0]
    rc = min(tm, REC_ROW_CHUNK)

    x2 = x_ref.at[0]
    nr = tm // rc

    def norm_rows(r, gm, shift):
        for q in range(rc // ROW_CHUNK):
            rows = slice(r * rc + q * ROW_CHUNK, r * rc + (q + 1) * ROW_CHUNK)
            inv = _inv_rms(x2[rows, :])
            h_ref[rows, :] = (x2[rows, :] * inv * gm + shift).astype(_BF16)

    def step(first):
        if first:
            gm = g_ref[...] * (1.0 + sc_ref[0])
            shift = sh_ref[0]
            norm_rows(0, gm, shift)
        for r in range(nr):
            rows = slice(r * rc, (r + 1) * rc)
            h = h_ref[rows, :]
            if og_ref is not None:
                og_ref[0, rows, :] = _gelu_tanh(_dot(h, wg_ref[...])).astype(_BF16)
            v = _dot(h, wv_ref[...])
            ov_ref[0, rows, :] = _conv_rows(v, cw_ref[...], cb_ref[...], blocklen).astype(_BF16)
            if first and r + 1 < nr:
                norm_rows(r + 1, gm, shift)

    pl.when(n == 0)(lambda: step(True))
    pl.when(n > 0)(lambda: step(False))


def _recin_nogate_kernel(x_ref, sh_ref, sc_ref, g_ref, wv_ref, cw_ref, cb_ref, ov_ref, h_ref,
                         *, blocklen):
    _recin_kernel(x_ref, sh_ref, sc_ref, g_ref, None, wv_ref, cw_ref, cb_ref, None, ov_ref,
                  h_ref, blocklen=blocklen)


def _recin(x, mods, row_of, k0, g, w_in, conv_w, conv_b, tm, blocklen, need_gate, tn=512):
    nb, L, d = x.shape
    r = w_in.shape[1] // 2
    nn = r // tn
    mod_spec = lambda k: pl.BlockSpec((1, 1, d), lambda b, i, n: (row_of(b), 0, k))
    x_spec = pl.BlockSpec((1, tm, d), lambda b, i, n: (b, i, 0))
    vec_spec = pl.BlockSpec((1, d), lambda b, i, n: (0, 0))
    wg_spec = pl.BlockSpec((d, tn), lambda b, i, n: (0, n))
    wv_spec = pl.BlockSpec((d, tn), lambda b, i, n: (0, nn + n))
    cw_spec = pl.BlockSpec((CONV_W, tn), lambda b, i, n: (0, n))
    cb_spec = pl.BlockSpec((1, tn), lambda b, i, n: (0, n))
    o_spec = pl.BlockSpec((1, tm, tn), lambda b, i, n: (b, i, n))
    o_shape = jax.ShapeDtypeStruct((nb, L, r), _BF16)
    common = dict(grid=(nb, L // tm, nn),
                  scratch_shapes=[pltpu.VMEM((tm, d), _BF16)],
                  compiler_params=_params(("arbitrary", "arbitrary", "arbitrary"), 56))
    g2 = g.reshape(1, d)
    cb2 = conv_b.reshape(1, r)
    if need_gate:
        return pl.pallas_call(
            functools.partial(_recin_kernel, blocklen=blocklen),
            in_specs=[x_spec, mod_spec(k0), mod_spec(k0 + 1), vec_spec, wg_spec, wv_spec,
                      cw_spec, cb_spec],
            out_specs=[o_spec, o_spec], out_shape=[o_shape, o_shape],
            name="rec_in", **common,
        )(x, mods, mods, g2, w_in, w_in, conv_w, cb2)
    return None, pl.pallas_call(
        functools.partial(_recin_nogate_kernel, blocklen=blocklen),
        in_specs=[x_spec, mod_spec(k0), mod_spec(k0 + 1), vec_spec, wv_spec, cw_spec, cb_spec],
        out_specs=o_spec, out_shape=o_shape,
        name="rec_in_ctx", **common,
    )(x, mods, mods, g2, w_in, conv_w, cb2)


def _gelu_tanh(x):
    c = math.sqrt(2.0 / math.pi)
    hx = 0.5 * x
    return hx + hx * jnp.tanh(x * (c + (0.044715 * c) * (x * x)))


def _scan_kernel(v_ref, vc_ref, g_ref, w_ref, bias_ref, lam_ref, o_ref,
                 a_sc, b_sc, hl_sc, al_sc, *, nh):
    lc = vc_ref.shape[1]
    s = v_ref.shape[1]
    nseg = V7X_SUBLANES
    seg, cseg = s // nseg, lc // nseg
    cb = s

    neg_lam = -lam_ref[...]
    sp = jnp.maximum(neg_lam, 0.0) + jnp.log(1.0 + jnp.exp(-jnp.abs(neg_lam)))
    q = (-0.5 * RG_C * LOG2E) * sp
    hb = 0.5 * bias_ref[...]

    def coeffs(vb, pre, hh, store):
        lanes = slice(hh * LRU_BLOCK, (hh + 1) * LRU_BLOCK)
        vh = 0.5 * vb.astype(_F32)
        for d in range(2):
            c0 = (2 * d) * LRU_BLOCK
            ta = jnp.tanh(pre[:, c0:c0 + LRU_BLOCK] + hb[2 * d:2 * d + 1, lanes])
            tx = jnp.tanh(pre[:, c0 + LRU_BLOCK:c0 + 2 * LRU_BLOCK]
                          + hb[2 * d + 1:2 * d + 2, lanes])
            qd = q[d:d + 1, lanes]
            a = jnp.exp2(qd + qd * ta)
            m2 = 1.0 - a * a
            mult = m2 * lax.rsqrt(jnp.maximum(m2, F32_TINY))
            store(d, a, (mult * vh) * (1.0 + tx))

    def lat_coeffs(m, carry):
        src = pl.ds(pl.multiple_of(m * seg, seg), seg)
        dst = pl.ds(m, seg, stride=nseg)
        for hh in range(nh):
            vb = v_ref[0, src, hh * LRU_BLOCK:(hh + 1) * LRU_BLOCK]

            def store(d, a, b, hh=hh):
                a_sc[d, hh, dst, :] = a
                b_sc[d, hh, dst, :] = b
            coeffs(vb, _dot(vb, w_ref[hh]), hh, store)
        return carry

    lax.fori_loop(0, nseg, lat_coeffs, 0)

    for hh in range(nh):
        vb = vc_ref[0, :, hh * LRU_BLOCK:(hh + 1) * LRU_BLOCK]

        def store(d, a, b, hh=hh):
            for m in range(nseg):
                dst = pl.ds(cb + m, cseg, stride=nseg)
                a_sc[d, hh, dst, :] = a[m * cseg:(m + 1) * cseg]
                b_sc[d, hh, dst, :] = b[m * cseg:(m + 1) * cseg]
        coeffs(vb, _dot(vb, w_ref[hh]), hh, store)

    vshape = (nseg, V7X_LANES)
    row = lax.broadcasted_iota(jnp.int32, vshape, 0)
    zeros = tuple(jnp.zeros(vshape, _F32) for _ in range(nh))
    ones = tuple(jnp.ones(vshape, _F32) for _ in range(nh))

    def sublane_scan(a, b, reverse):
        for k in (1, 2, 4):
            keep = (row < nseg - k) if reverse else (row >= k)
            shift = (nseg - k) if reverse else k
            a_sh = jnp.where(keep, pltpu.roll(a, shift, axis=0), 1.0)
            b_sh = jnp.where(keep, pltpu.roll(b, shift, axis=0), 0.0)
            b = a * b_sh + b
            a = a * a_sh
        return a, b

    def local_pass(base, length, save):
        def body(i, carry):
            hs, prods = carry
            new_h, new_p = [], []
            for d, k in ((0, i), (1, length - 1 - i)):
                t = pl.ds(pl.multiple_of(k * nseg, nseg), nseg)
                idx = pl.ds(pl.multiple_of(base + k * nseg, nseg), nseg)
                for hh in range(nh):
                    a = a_sc[d, hh, idx, :]
                    h = a * hs[d * nh + hh] + b_sc[d, hh, idx, :]
                    pr = prods[d * nh + hh] * a
                    if save:
                        hl_sc[d, hh, t, :] = h
                        al_sc[d, hh, t, :] = pr
                    new_h.append(h)
                    new_p.append(pr)
            return tuple(new_h), tuple(new_p)
        return lax.fori_loop(0, length, body, (zeros + zeros, ones + ones), unroll=8)

    def entry_states(h_fin, a_tot, g_in, reverse):
        acum, bcum = sublane_scan(a_tot, h_fin, reverse)
        e = acum * g_in + bcum
        if reverse:
            entry = jnp.where(row < nseg - 1, pltpu.roll(e, nseg - 1, axis=0), g_in)
            final = e[0:1, :]
        else:
            entry = jnp.where(row >= 1, pltpu.roll(e, 1, axis=0), g_in)
            final = e[nseg - 1:nseg, :]
        return entry, jnp.broadcast_to(final, vshape)

    hc, pc_tot = local_pass(cb, cseg, save=False)
    hl, pl_tot = local_pass(0, seg, save=True)
    entries = []
    for d, reverse in ((0, False), (1, True)):
        for hh in range(nh):
            n = d * nh + hh
            _, g_ctx = entry_states(hc[n], pc_tot[n], zeros[hh], reverse)
            entry, _ = entry_states(hl[n], pl_tot[n], g_ctx, reverse)
            entries.append(entry)

    def fix(k, carry):
        t = pl.ds(pl.multiple_of(k * nseg, nseg), nseg)
        for hh in range(nh):
            hl_sc[0, hh, t, :] = ((hl_sc[0, hh, t, :] + al_sc[0, hh, t, :] * entries[hh])
                                  + (hl_sc[1, hh, t, :] + al_sc[1, hh, t, :] * entries[nh + hh]))
        return carry
    lax.fori_loop(0, seg, fix, 0, unroll=8)

    def gate_out(m, carry):
        rows = pl.ds(pl.multiple_of(m * seg, seg), seg)
        for hh in range(nh):
            lanes = slice(hh * LRU_BLOCK, (hh + 1) * LRU_BLOCK)
            y = hl_sc[0, hh, pl.ds(m, seg, stride=nseg), :]
            o_ref[0, rows, lanes] = (g_ref[0, rows, lanes].astype(_F32) * y).astype(_BF16)
        return carry

    lax.fori_loop(0, nseg, gate_out, 0)


def _scan(v_lat, v_ctx, g_lat, w_cat, bias, lam, cc=512):
    nb, s, r = v_lat.shape
    lc = v_ctx.shape[1]
    nh = cc // LRU_BLOCK
    coef_rows = s + lc
    blk = lambda L: pl.BlockSpec((1, L, cc), lambda b, c: (b, 0, c))
    return pl.pallas_call(
        functools.partial(_scan_kernel, nh=nh),
        grid=(nb, r // cc),
        in_specs=[blk(s), blk(lc), blk(s),
                  pl.BlockSpec((nh, LRU_BLOCK, 4 * LRU_BLOCK), lambda b, c: (c, 0, 0)),
                  pl.BlockSpec((4, cc), lambda b, c: (0, c)),
                  pl.BlockSpec((2, cc), lambda b, c: (0, c))],
        out_specs=blk(s),
        out_shape=jax.ShapeDtypeStruct((nb, s, r), _BF16),
        scratch_shapes=[pltpu.VMEM((2, nh, coef_rows, V7X_LANES), _F32),
                        pltpu.VMEM((2, nh, coef_rows, V7X_LANES), _F32),
                        pltpu.VMEM((2, nh, s, V7X_LANES), _F32),
                        pltpu.VMEM((2, nh, s, V7X_LANES), _F32)],
        compiler_params=_params(("arbitrary", "arbitrary"), 56),
        name="rglru_scan",
    )(v_lat, v_ctx, g_lat, w_cat, bias, lam)


def _outproj_kernel(x_ref, *rest, rc, split):
    if split:
        alo_ref, ahi_ref, gt_ref, g_ref, w_ref, o_ref = rest
    else:
        alo_ref, gt_ref, g_ref, w_ref, o_ref = rest
    tm = o_ref.shape[1]
    x2 = x_ref.at[0]
    o2 = o_ref.at[0]

    def residual(r, gg):
        for q in range(rc // ROW_CHUNK):
            rows = slice(r * rc + q * ROW_CHUNK, r * rc + (q + 1) * ROW_CHUNK)
            inv = _inv_rms(o2[rows, :])
            o2[rows, :] = x2[rows, :] + (o2[rows, :] * inv) * gg

    def body(a_ref):
        gg = gt_ref[0] * g_ref[...]
        for r in range(tm // rc):
            rows = slice(r * rc, (r + 1) * rc)
            o2[rows, :] = _dot(a_ref[0, rows, :], w_ref[...])
            if r > 0:
                residual(r - 1, gg)
        residual(tm // rc - 1, gg)

    if split:
        i = pl.program_id(1)
        half = pl.num_programs(1) // 2
        pl.when(i < half)(lambda: body(alo_ref))
        pl.when(i >= half)(lambda: body(ahi_ref))
    else:
        body(alo_ref)


def _outproj(x, a, mods, k_gate, g, w, tm=512, rc=256):
    nb, L, d = x.shape
    split = isinstance(a, (tuple, list))
    r = w.shape[0]
    half = (L // tm) // 2
    if split:
        a_specs = [pl.BlockSpec((1, tm, r), lambda b, i: (b, jnp.minimum(i, half - 1), 0)),
                   pl.BlockSpec((1, tm, r), lambda b, i: (b, jnp.maximum(i - half, 0), 0))]
        a_args = list(a)
    else:
        a_specs = [pl.BlockSpec((1, tm, r), lambda b, i: (b, i, 0))]
        a_args = [a]
    return pl.pallas_call(
        functools.partial(_outproj_kernel, rc=rc, split=split),
        grid=(nb, L // tm),
        in_specs=[pl.BlockSpec((1, tm, d), lambda b, i: (b, i, 0))] + a_specs + [
            pl.BlockSpec((1, 1, d), lambda b, i: (b, 0, k_gate)),
            pl.BlockSpec((1, d), lambda b, i: (0, 0)),
            pl.BlockSpec((r, d), lambda b, i: (0, 0))],
        out_specs=pl.BlockSpec((1, tm, d), lambda b, i: (b, i, 0)),
        out_shape=jax.ShapeDtypeStruct((nb, L, d), _F32),
        compiler_params=_params(("arbitrary", "arbitrary"), 48),
        name="mixer_out",
    )(x, *a_args, mods, g.reshape(1, d), w)


def _dft_tables(n):
    k = np.arange(n, dtype=np.int64)
    ang = 2.0 * np.pi * ((k[:, None] * k[None, :]) % n).astype(np.float64) / n
    return np.cos(ang), np.sin(ang)


def _chdft_kernel(x_ref, sh_ref, sc_ref, g_ref, cs_ref, o_ref, h_ref):
    _norm_mod_into(x_ref.at[0], h_ref, g_ref[...], sh_ref[0], sc_ref[0])
    d = h_ref.shape[1]
    for gi in range(d // FGROUP):
        lanes = slice(gi * FGROUP, (gi + 1) * FGROUP)
        t = _dot(h_ref[:, lanes], cs_ref[...])
        o_ref[0, 0, :, lanes] = t[:, :FGROUP].astype(_BF16)
        o_ref[0, 1, :, lanes] = t[:, FGROUP:].astype(_BF16)


def _chdft(x, mods, k0, g, cs, tm=512):
    nb, L, d = x.shape
    mod_spec = lambda k: pl.BlockSpec((1, 1, d), lambda b, i: (b, 0, k))
    return pl.pallas_call(
        _chdft_kernel,
        grid=(nb, L // tm),
        in_specs=[pl.BlockSpec((1, tm, d), lambda b, i: (b, i, 0)),
                  mod_spec(k0), mod_spec(k0 + 1),
                  pl.BlockSpec((1, d), lambda b, i: (0, 0)),
                  pl.BlockSpec((FGROUP, 2 * FGROUP), lambda b, i: (0, 0))],
        out_specs=pl.BlockSpec((1, 2, tm, d), lambda b, i: (b, 0, i, 0)),
        out_shape=jax.ShapeDtypeStruct((nb, 2, L, d), _BF16),
        scratch_shapes=[pltpu.VMEM((tm, d), _BF16)],
        compiler_params=_params(("arbitrary", "arbitrary"), 40),
        name="fourier_channel_dft",
    )(x, mods, mods, g.reshape(1, d), cs)


def _seqdft_kernel(ac_ref, as_ref, ctop_ref, r_ref, bc_ref, bs_ref, lo_ref, hi_ref, h_sc,
                   *, scale, rc):
    ii = pl.program_id(2)
    tm = ac_ref.shape[0]
    mb = MIRROR_BLOCK

    @pl.when(ii == 0)
    def _():
        h_sc[tm:tm + mb, :] = (_dot(ctop_ref[...], bc_ref[...]) * scale).astype(_BF16)

    @pl.when(ii > 0)
    def _():
        h_sc[tm:tm + mb, :] = h_sc[0:mb, :]

    for r in range(tm // rc):
        rows = slice(r * rc, (r + 1) * rc)
        p = _dot(ac_ref[rows, :], bc_ref[...])
        q = _dot(as_ref[rows, :], bs_ref[...])
        lo_ref[0, rows, :] = ((p - q) * scale).astype(_BF16)
        h_sc[rows, :] = ((p + q) * scale).astype(_BF16)
    nblk = tm // mb
    for qb in range(nblk):
        src = h_sc[(nblk - 1 - qb) * mb:(nblk + 1 - qb) * mb, :]
        hi_ref[0, qb * mb:(qb + 1) * mb, :] = _dot(r_ref[...], src).astype(_BF16)


def _seqdft(xcs, scale, tm=512, tn=1024, rc=512):
    nb, _, L, d = xcs.shape
    half = L // 2
    ni = half // tm
    mb = MIRROR_BLOCK
    cos_s, sin_s = _dft_tables(L)
    a_cos = jnp.asarray(cos_s[:half], _BF16)
    a_sin = jnp.asarray(sin_s[:half], _BF16)
    c_top = jnp.asarray(cos_s[half:half + mb], _BF16)
    mirror = np.zeros((mb, 2 * mb), np.float32)
    mirror[np.arange(mb), mb - np.arange(mb)] = 1.0
    b_spec = lambda part: pl.BlockSpec((None, None, L, tn), lambda b, n, i: (b, part, 0, n))
    o_shape = jax.ShapeDtypeStruct((nb, half, d), _BF16)
    return pl.pallas_call(
        functools.partial(_seqdft_kernel, scale=scale, rc=rc),
        grid=(nb, d // tn, ni),
        in_specs=[pl.BlockSpec((tm, L), lambda b, n, i: (ni - 1 - i, 0)),
                  pl.BlockSpec((tm, L), lambda b, n, i: (ni - 1 - i, 0)),
                  pl.BlockSpec((mb, L), lambda b, n, i: (0, 0)),
                  pl.BlockSpec((mb, 2 * mb), lambda b, n, i: (0, 0)),
                  b_spec(0), b_spec(1)],
        out_specs=[pl.BlockSpec((1, tm, tn), lambda b, n, i: (b, ni - 1 - i, n)),
                   pl.BlockSpec((1, tm, tn), lambda b, n, i: (b, i, n))],
        out_shape=[o_shape, o_shape],
        scratch_shapes=[pltpu.VMEM((tm + mb, tn), _BF16)],
        compiler_params=_params(("arbitrary", "arbitrary", "arbitrary"), 48),
        name="fourier_seq_dft",
    )(a_cos, a_sin, c_top, jnp.asarray(mirror, _BF16), xcs, xcs)


def kernel(x, c, ctx, c_ctx, mod_w, mod_b, norm_g, ffn_w_in, ffn_w_out, rec_w_in, rec_conv_w,
           rec_conv_b, rec_gate_w, rec_gate_b, rec_lam, rec_w_out, fou_w_out):
    nb, s, d = x.shape
    lc = ctx.shape[1]
    depth = mod_w.shape[0]
    n_mixers = 2
    last_rec = ((depth - 1) // n_mixers) * n_mixers
    assert last_rec == 0, "context stream is only carried up to the first RG-LRU mixer"

    rows = 2 * V7X_SUBLANES
    cc = jnp.concatenate([c, c_ctx[None], jnp.zeros((rows - nb - 1, d), _F32)], axis=0)
    mods_all = _adaln(cc, mod_w, mod_b).reshape(depth, rows, 1, N_MOD * d)

    w_ffn = (ffn_w_in[0, 0].astype(_BF16), ffn_w_out[0, 0].astype(_BF16))
    order = [(i, sub) for i in range(depth) for sub in range(2)]

    def ffn_lat(x, mods, k0, g_pre, g_post, layer, sub):
        nonlocal w_ffn
        nxt = order.index((layer, sub)) + 1
        next_w = (ffn_w_in, ffn_w_out) + order[nxt] if nxt < len(order) else None
        out = _ffn(x, mods, lat_row, k0, g_pre, g_post, *w_ffn, next_w=next_w)
        if next_w is None:
            return out
        w_ffn = (out[1], out[2])
        return out[0]

    lat_row = lambda b: b
    ctx_row = lambda b: nb

    for i in range(depth):
        is_rec = (i % n_mixers) == 0
        j = i // n_mixers
        mods = mods_all[i]
        g = norm_g[i]
        w_pre = w_ffn
        x = ffn_lat(x, mods, 0, g[0], g[1], i, 0)
        if is_rec:
            ctx1 = _ffn(ctx.reshape(1, nb * lc, d), mods, ctx_row, 0, g[0], g[1],
                        *w_pre).reshape(nb, lc, d)
            w_rec = rec_w_in[j].astype(_BF16)
            g_lat, v_lat = _recin(x, mods, lat_row, 3, g[2], w_rec, rec_conv_w[j], rec_conv_b[j],
                                  tm=1024, blocklen=GRID_W, need_gate=True)
            _, v_ctx = _recin(ctx1, mods, ctx_row, 3, g[2], w_rec, rec_conv_w[j], rec_conv_b[j],
                              tm=lc, blocklen=lc, need_gate=False)
            gw = rec_gate_w[j]
            nh = gw.shape[2]
            w_cat = jnp.transpose(gw, (2, 3, 0, 1, 4)).reshape(nh, LRU_BLOCK, 4 * LRU_BLOCK)
            a = _scan(v_lat, v_ctx, g_lat, (0.5 * w_cat).astype(_BF16),
                      rec_gate_b[j].reshape(4, -1), rec_lam[j])
            w_mix = rec_w_out[j].astype(_BF16)
        else:
            cos_c, sin_c = _dft_tables(FGROUP)
            cs = jnp.asarray(np.concatenate([cos_c, sin_c], axis=1), _BF16)
            xcs = _chdft(x, mods, 3, g[2], cs)
            a = _seqdft(xcs, 1.0 / math.sqrt(s * FGROUP))
            w_mix = fou_w_out[j].astype(_BF16)
        x = _outproj(x, a, mods, 5, g[3], w_mix)
        x = ffn_lat(x, mods, 6, g[4], g[5], i, 1)
    return x
```

```python
import functools
import math

import jax
import jax.numpy as jnp
import numpy as np
from jax import lax
from jax.experimental import pallas as pl
from jax.experimental.pallas import tpu as pltpu

N_SUB = 3
N_MOD = 3 * N_SUB
MACARON = 0.5
GRID_W = 64
CONV_W = 4
CONV_LEFT = CONV_W // 2
LRU_BLOCK = 128
RG_C = 8.0
LOG2E = math.log2(math.e)
F32_TINY = float(np.finfo(np.float32).tiny)
FGROUP = 256
MIRROR_BLOCK = 128
EPS = 1e-6

V7X_SUBLANES = 8
ROW_CHUNK = 64
REC_ROW_CHUNK = 256
V7X_LANES = 128
MIB = 1024 * 1024

_BF16 = jnp.bfloat16
_F32 = jnp.float32


def _params(semantics, vmem_mib):
    return pltpu.CompilerParams(dimension_semantics=semantics,
                                vmem_limit_bytes=vmem_mib * MIB)


def _dot(a, b):
    return jnp.dot(a, b, preferred_element_type=_F32)


def _for_row_chunks(n_rows, fn, rn=ROW_CHUNK, unroll=2):
    def body(r, carry):
        fn(pl.ds(pl.multiple_of(r * rn, rn), rn))
        return carry
    lax.fori_loop(0, n_rows // rn, body, 0, unroll=unroll)


def _inv_rms(x):
    return lax.rsqrt(jnp.mean(x * x, axis=-1, keepdims=True) + EPS)


def _norm_mod_into(x_ref, h_ref, g, shift, scale):
    gm = g * (1.0 + scale)

    def fn(rows):
        inv = _inv_rms(x_ref[rows, :])
        h_ref[rows, :] = (x_ref[rows, :] * inv * gm + shift).astype(h_ref.dtype)
    _for_row_chunks(x_ref.shape[0], fn)


def _adaln_kernel(c_ref, w_ref, b_ref, o_ref):
    c = c_ref[...]
    a = (c * jax.nn.sigmoid(c)).astype(_BF16)
    o_ref[0] = _dot(a, w_ref[0].astype(_BF16)) + b_ref[0]


def _adaln(cc, mod_w, mod_b, tn=2048):
    depth, d, n = mod_w.shape
    rows = cc.shape[0]
    return pl.pallas_call(
        _adaln_kernel,
        grid=(depth, n // tn),
        in_specs=[pl.BlockSpec((rows, d), lambda l, j: (0, 0)),
                  pl.BlockSpec((1, d, tn), lambda l, j: (l, 0, j)),
                  pl.BlockSpec((1, 1, tn), lambda l, j: (l, 0, j))],
        out_specs=pl.BlockSpec((1, rows, tn), lambda l, j: (l, 0, j)),
        out_shape=jax.ShapeDtypeStruct((depth, rows, n), _F32),
        compiler_params=_params(("arbitrary", "arbitrary"), 48),
        name="adaln",
    )(cc, mod_w, mod_b.reshape(depth, 1, n))


def _ffn_kernel(x_ref, sh_ref, sc_ref, gt_ref, gpre_ref, gpost_ref, wg_ref, wu_ref, wo_ref,
                *rest, rc, rc_mid, nf, cast_next):
    if cast_next:
        nwi_ref, nwo_ref, o_ref, cwi_ref, cwo_ref, h_ref = rest
        cwi_ref[...] = nwi_ref[...].astype(_BF16)
        cwo_ref[...] = nwo_ref[...].astype(_BF16)
    else:
        o_ref, h_ref = rest
    j = pl.program_id(2)
    nj = pl.num_programs(2)
    tm = h_ref.shape[0]
    x2 = x_ref.at[0]
    o2 = o_ref.at[0]

    def sweep(chunk):
        start, size = chunk
        return [slice(q, q + ROW_CHUNK) for q in range(start, start + size, ROW_CHUNK)]

    def norm_rows(chunk, gm, shift):
        for rows in sweep(chunk):
            inv = _inv_rms(x2[rows, :])
            h_ref[rows, :] = (x2[rows, :] * inv * gm + shift).astype(_BF16)

    def residual_rows(chunk, gg):
        for rows in sweep(chunk):
            inv = _inv_rms(o2[rows, :])
            o2[rows, :] = x2[rows, :] + (o2[rows, :] * inv) * gg

    def step(first, last):
        size = rc if (first or last) else rc_mid
        chunks = [(start, size) for start in range(0, tm, size)]
        if first:
            gm = gpre_ref[...] * (1.0 + sc_ref[0])
            shift = sh_ref[0]
            norm_rows(chunks[0], gm, shift)
        if last:
            gg = (MACARON * gt_ref[0]) * gpost_ref[...]
        for r, (start, size) in enumerate(chunks):
            rows = slice(start, start + size)
            h = h_ref[rows, :]
            g = _dot(h, wg_ref[...])
            u = _dot(h, wu_ref[...])
            hg = 0.5 * g
            a = ((hg + hg * jnp.tanh(hg)) * u).astype(_BF16)
            y = _dot(a, wo_ref[...])
            if first:
                o2[rows, :] = y
                if r + 1 < len(chunks):
                    norm_rows(chunks[r + 1], gm, shift)
            else:
                o2[rows, :] += y
            if last and r > 0:
                residual_rows(chunks[r - 1], gg)
        if last:
            residual_rows(chunks[-1], gg)

    if nf == 1:
        step(True, True)
    else:
        pl.when(j == 0)(lambda: step(True, False))
        pl.when(j == nj - 1)(lambda: step(False, True))
        if nf > 2:
            pl.when((j > 0) & (j < nj - 1))(lambda: step(False, False))


def _ffn(x, mods, row_of, k0, g_pre, g_post, w_in, w_out, next_w=None, tm=1024, tf=512, rc=512,
         rc_mid=1024):
    nb, L, d = x.shape
    f = w_out.shape[0]
    nf = f // tf
    nt = nb * (L // tm)
    mod_spec = lambda k: pl.BlockSpec((1, 1, d), lambda b, i, j: (row_of(b), 0, k))
    vec_spec = pl.BlockSpec((1, d), lambda b, i, j: (0, 0))
    in_specs = [pl.BlockSpec((1, tm, d), lambda b, i, j: (b, i, 0)),
                mod_spec(k0), mod_spec(k0 + 1), mod_spec(k0 + 2),
                vec_spec, vec_spec,
                pl.BlockSpec((d, tf), lambda b, i, j: (0, j)),
                pl.BlockSpec((d, tf), lambda b, i, j: (0, nf + j)),
                pl.BlockSpec((tf, d), lambda b, i, j: (j, 0))]
    args = [x, mods, mods, mods, g_pre.reshape(1, d), g_post.reshape(1, d), w_in, w_in, w_out]
    out_specs = [pl.BlockSpec((1, tm, d), lambda b, i, j: (b, i, 0))]
    out_shape = [jax.ShapeDtypeStruct((nb, L, d), _F32)]
    if next_w is not None:
        nw_in, nw_out, layer, sub = next_w
        assert d % nt == 0 and (2 * f) % nf == 0 and f % (nt * nf) == 0
        ri, ci, ro = d // nt, 2 * f // nf, f // (nt * nf)
        tile = lambda i: i * (L // tm)
        in_specs += [pl.BlockSpec((None, None, ri, ci),
                                  lambda b, i, j: (layer, sub, tile(b) + i, j)),
                     pl.BlockSpec((None, None, ro, d),
                                  lambda b, i, j: (layer, sub, (tile(b) + i) * nf + j, 0))]
        args += [nw_in, nw_out]
        out_specs += [pl.BlockSpec((ri, ci), lambda b, i, j: (tile(b) + i, j)),
                      pl.BlockSpec((ro, d), lambda b, i, j: ((tile(b) + i) * nf + j, 0))]
        out_shape += [jax.ShapeDtypeStruct((d, 2 * f), _BF16), jax.ShapeDtypeStruct((f, d), _BF16)]
    out = pl.pallas_call(
        functools.partial(_ffn_kernel, rc=rc, rc_mid=rc_mid, nf=nf,
                          cast_next=next_w is not None),
        grid=(nb, L // tm, nf),
        in_specs=in_specs, out_specs=out_specs, out_shape=out_shape,
        scratch_shapes=[pltpu.VMEM((tm, d), _BF16)],
        compiler_params=_params(("arbitrary", "arbitrary", "arbitrary"), 60),
        name="ffn",
    )(*args)
    return out if next_w is not None else out[0]


def _conv_rows(v, cw, cb, blocklen):
    tm = v.shape[0]
    pos = lax.broadcasted_iota(jnp.int32, (tm, 1), 0) % blocklen
    out = cb + cw[CONV_LEFT:CONV_LEFT + 1] * v
    for k in range(CONV_W):
        off = k - CONV_LEFT
        if off == 0:
            continue
        shifted = pltpu.roll(v, (-off) % tm, axis=0)
        valid = (pos + off >= 0) & (pos + off < blocklen)
        out = out + cw[k:k + 1] * jnp.where(valid, shifted, 0.0)
    return out


def _recin_kernel(x_ref, sh_ref, sc_ref, g_ref, wg_ref, wv_ref, cw_ref, cb_ref,
                  og_ref, ov_ref, h_ref, *, blocklen):
    n = pl.program_id(2)

    tm = h_ref.shape[0]
    rc = min(tm, REC_ROW_CHUNK)
    assert rc % blocklen == 0 and tm % rc == 0

    x2 = x_ref.at[0]
    nr = tm // rc

    def norm_rows(r, gm, shift):
        for q in range(rc // ROW_CHUNK):
            rows = slice(r * rc + q * ROW_CHUNK, r * rc + (q + 1) * ROW_CHUNK)
            inv = _inv_rms(x2[rows, :])
            h_ref[rows, :] = (x2[rows, :] * inv * gm + shift).astype(_BF16)

    def step(first):
        if first:
            gm = g_ref[...] * (1.0 + sc_ref[0])
            shift = sh_ref[0]
            norm_rows(0, gm, shift)
        for r in range(nr):
            rows = slice(r * rc, (r + 1) * rc)
            h = h_ref[rows, :]
            if og_ref is not None:
                og_ref[0, rows, :] = _gelu_tanh(_dot(h, wg_ref[...])).astype(_BF16)
            v = _dot(h, wv_ref[...])
            ov_ref[0, rows, :] = _conv_rows(v, cw_ref[...], cb_ref[...], blocklen).astype(_BF16)
            if first and r + 1 < nr:
                norm_rows(r + 1, gm, shift)

    pl.when(n == 0)(lambda: step(True))
    pl.when(n > 0)(lambda: step(False))


def _recin_nogate_kernel(x_ref, sh_ref, sc_ref, g_ref, wv_ref, cw_ref, cb_ref, ov_ref, h_ref,
                         *, blocklen):
    _recin_kernel(x_ref, sh_ref, sc_ref, g_ref, None, wv_ref, cw_ref, cb_ref, None, ov_ref,
                  h_ref, blocklen=blocklen)


def _recin(x, mods, row_of, k0, g, w_in, conv_w, conv_b, tm, blocklen, need_gate, tn=512):
    nb, L, d = x.shape
    r = w_in.shape[1] // 2
    nn = r // tn
    mod_spec = lambda k: pl.BlockSpec((1, 1, d), lambda b, i, n: (row_of(b), 0, k))
    x_spec = pl.BlockSpec((1, tm, d), lambda b, i, n: (b, i, 0))
    vec_spec = pl.BlockSpec((1, d), lambda b, i, n: (0, 0))
    wg_spec = pl.BlockSpec((d, tn), lambda b, i, n: (0, n))
    wv_spec = pl.BlockSpec((d, tn), lambda b, i, n: (0, nn + n))
    cw_spec = pl.BlockSpec((CONV_W, tn), lambda b, i, n: (0, n))
    cb_spec = pl.BlockSpec((1, tn), lambda b, i, n: (0, n))
    o_spec = pl.BlockSpec((1, tm, tn), lambda b, i, n: (b, i, n))
    o_shape = jax.ShapeDtypeStruct((nb, L, r), _BF16)
    common = dict(grid=(nb, L // tm, nn),
                  scratch_shapes=[pltpu.VMEM((tm, d), _BF16)],
                  compiler_params=_params(("arbitrary", "arbitrary", "arbitrary"), 56))
    g2 = g.reshape(1, d)
    cb2 = conv_b.reshape(1, r)
    if need_gate:
        return pl.pallas_call(
            functools.partial(_recin_kernel, blocklen=blocklen),
            in_specs=[x_spec, mod_spec(k0), mod_spec(k0 + 1), vec_spec, wg_spec, wv_spec,
                      cw_spec, cb_spec],
            out_specs=[o_spec, o_spec], out_shape=[o_shape, o_shape],
            name="rec_in", **common,
        )(x, mods, mods, g2, w_in, w_in, conv_w, cb2)
    return None, pl.pallas_call(
        functools.partial(_recin_nogate_kernel, blocklen=blocklen),
        in_specs=[x_spec, mod_spec(k0), mod_spec(k0 + 1), vec_spec, wv_spec, cw_spec, cb_spec],
        out_specs=o_spec, out_shape=o_shape,
        name="rec_in_ctx", **common,
    )(x, mods, mods, g2, w_in, conv_w, cb2)


def _gelu_tanh(x):
    c = math.sqrt(2.0 / math.pi)
    hx = 0.5 * x
    return hx + hx * jnp.tanh(x * (c + (0.044715 * c) * (x * x)))


def _scan_kernel(v_ref, vc_ref, g_ref, w_ref, bias_ref, lam_ref, o_ref,
                 a_sc, b_sc, hl_sc, al_sc, *, nh):
    lc = vc_ref.shape[1]
    s = v_ref.shape[1]
    nseg = V7X_SUBLANES
    seg, cseg = s // nseg, lc // nseg
    cb = s

    neg_lam = -lam_ref[...]
    sp = jnp.maximum(neg_lam, 0.0) + jnp.log(1.0 + jnp.exp(-jnp.abs(neg_lam)))
    q = (-0.5 * RG_C * LOG2E) * sp
    hb = 0.5 * bias_ref[...]

    def coeffs(vb, pre, hh, store):
        lanes = slice(hh * LRU_BLOCK, (hh + 1) * LRU_BLOCK)
        vh = 0.5 * vb.astype(_F32)
        for d in range(2):
            c0 = (2 * d) * LRU_BLOCK
            ta = jnp.tanh(pre[:, c0:c0 + LRU_BLOCK] + hb[2 * d:2 * d + 1, lanes])
            tx = jnp.tanh(pre[:, c0 + LRU_BLOCK:c0 + 2 * LRU_BLOCK]
                          + hb[2 * d + 1:2 * d + 2, lanes])
            qd = q[d:d + 1, lanes]
            a = jnp.exp2(qd + qd * ta)
            m2 = 1.0 - a * a
            mult = m2 * lax.rsqrt(jnp.maximum(m2, F32_TINY))
            store(d, a, (mult * vh) * (1.0 + tx))

    def lat_coeffs(m, carry):
        src = pl.ds(pl.multiple_of(m * seg, seg), seg)
        dst = pl.ds(m, seg, stride=nseg)
        for hh in range(nh):
            vb = v_ref[0, src, hh * LRU_BLOCK:(hh + 1) * LRU_BLOCK]

            def store(d, a, b, hh=hh):
                a_sc[d, hh, dst, :] = a
                b_sc[d, hh, dst, :] = b
            coeffs(vb, _dot(vb, w_ref[hh]), hh, store)
        return carry

    lax.fori_loop(0, nseg, lat_coeffs, 0)

    for hh in range(nh):
        vb = vc_ref[0, :, hh * LRU_BLOCK:(hh + 1) * LRU_BLOCK]

        def store(d, a, b, hh=hh):
            for m in range(nseg):
                dst = pl.ds(cb + m, cseg, stride=nseg)
                a_sc[d, hh, dst, :] = a[m * cseg:(m + 1) * cseg]
                b_sc[d, hh, dst, :] = b[m * cseg:(m + 1) * cseg]
        coeffs(vb, _dot(vb, w_ref[hh]), hh, store)

    vshape = (nseg, V7X_LANES)
    row = lax.broadcasted_iota(jnp.int32, vshape, 0)
    zeros = tuple(jnp.zeros(vshape, _F32) for _ in range(nh))
    ones = tuple(jnp.ones(vshape, _F32) for _ in range(nh))

    def sublane_scan(a, b, reverse):
        for k in (1, 2, 4):
            keep = (row < nseg - k) if reverse else (row >= k)
            shift = (nseg - k) if reverse else k
            a_sh = jnp.where(keep, pltpu.roll(a, shift, axis=0), 1.0)
            b_sh = jnp.where(keep, pltpu.roll(b, shift, axis=0), 0.0)
            b = a * b_sh + b
            a = a * a_sh
        return a, b

    def local_pass(base, length, save):
        def body(i, carry):
            hs, prods = carry
            new_h, new_p = [], []
            for d, k in ((0, i), (1, length - 1 - i)):
                t = pl.ds(pl.multiple_of(k * nseg, nseg), nseg)
                idx = pl.ds(pl.multiple_of(base + k * nseg, nseg), nseg)
                for hh in range(nh):
                    a = a_sc[d, hh, idx, :]
                    h = a * hs[d * nh + hh] + b_sc[d, hh, idx, :]
                    pr = prods[d * nh + hh] * a
                    if save:
                        hl_sc[d, hh, t, :] = h
                        al_sc[d, hh, t, :] = pr
                    new_h.append(h)
                    new_p.append(pr)
            return tuple(new_h), tuple(new_p)
        return lax.fori_loop(0, length, body, (zeros + zeros, ones + ones), unroll=8)

    def entry_states(h_fin, a_tot, g_in, reverse):
        acum, bcum = sublane_scan(a_tot, h_fin, reverse)
        e = acum * g_in + bcum
        if reverse:
            entry = jnp.where(row < nseg - 1, pltpu.roll(e, nseg - 1, axis=0), g_in)
            final = e[0:1, :]
        else:
            entry = jnp.where(row >= 1, pltpu.roll(e, 1, axis=0), g_in)
            final = e[nseg - 1:nseg, :]
        return entry, jnp.broadcast_to(final, vshape)

    hc, pc_tot = local_pass(cb, cseg, save=False)
    hl, pl_tot = local_pass(0, seg, save=True)
    entries = []
    for d, reverse in ((0, False), (1, True)):
        for hh in range(nh):
            n = d * nh + hh
            _, g_ctx = entry_states(hc[n], pc_tot[n], zeros[hh], reverse)
            entry, _ = entry_states(hl[n], pl_tot[n], g_ctx, reverse)
            entries.append(entry)

    def fix(k, carry):
        t = pl.ds(pl.multiple_of(k * nseg, nseg), nseg)
        for hh in range(nh):
            hl_sc[0, hh, t, :] = ((hl_sc[0, hh, t, :] + al_sc[0, hh, t, :] * entries[hh])
                                  + (hl_sc[1, hh, t, :] + al_sc[1, hh, t, :] * entries[nh + hh]))
        return carry
    lax.fori_loop(0, seg, fix, 0, unroll=8)

    def gate_out(m, carry):
        rows = pl.ds(pl.multiple_of(m * seg, seg), seg)
        for hh in range(nh):
            lanes = slice(hh * LRU_BLOCK, (hh + 1) * LRU_BLOCK)
            y = hl_sc[0, hh, pl.ds(m, seg, stride=nseg), :]
            o_ref[0, rows, lanes] = (g_ref[0, rows, lanes].astype(_F32) * y).astype(_BF16)
        return carry

    lax.fori_loop(0, nseg, gate_out, 0)


def _scan(v_lat, v_ctx, g_lat, w_cat, bias, lam, cc=512):
    nb, s, r = v_lat.shape
    lc = v_ctx.shape[1]
    nh = cc // LRU_BLOCK
    coef_rows = s + lc
    blk = lambda L: pl.BlockSpec((1, L, cc), lambda b, c: (b, 0, c))
    return pl.pallas_call(
        functools.partial(_scan_kernel, nh=nh),
        grid=(nb, r // cc),
        in_specs=[blk(s), blk(lc), blk(s),
                  pl.BlockSpec((nh, LRU_BLOCK, 4 * LRU_BLOCK), lambda b, c: (c, 0, 0)),
                  pl.BlockSpec((4, cc), lambda b, c: (0, c)),
                  pl.BlockSpec((2, cc), lambda b, c: (0, c))],
        out_specs=blk(s),
        out_shape=jax.ShapeDtypeStruct((nb, s, r), _BF16),
        scratch_shapes=[pltpu.VMEM((2, nh, coef_rows, V7X_LANES), _F32),
                        pltpu.VMEM((2, nh, coef_rows, V7X_LANES), _F32),
                        pltpu.VMEM((2, nh, s, V7X_LANES), _F32),
                        pltpu.VMEM((2, nh, s, V7X_LANES), _F32)],
        compiler_params=_params(("arbitrary", "arbitrary"), 56),
        name="rglru_scan",
    )(v_lat, v_ctx, g_lat, w_cat, bias, lam)


def _outproj_kernel(x_ref, a_ref, gt_ref, g_ref, w_ref, o_ref, *, rc):
    tm = o_ref.shape[1]
    x2 = x_ref.at[0]
    o2 = o_ref.at[0]
    gg = gt_ref[0] * g_ref[...]

    def residual(r):
        for q in range(r * rc, (r + 1) * rc, ROW_CHUNK):
            rows = slice(q, q + ROW_CHUNK)
            inv = _inv_rms(o2[rows, :])
            o2[rows, :] = x2[rows, :] + (o2[rows, :] * inv) * gg

    for r in range(tm // rc):
        rows = slice(r * rc, (r + 1) * rc)
        o2[rows, :] = _dot(a_ref[0, rows, :], w_ref[...])
        if r > 0:
            residual(r - 1)
    residual(tm // rc - 1)


def _outproj(x, a, mods, k_gate, g, w, tm=1024, rc=256):
    nb, L, d = x.shape
    r = w.shape[0]
    return pl.pallas_call(
        functools.partial(_outproj_kernel, rc=rc),
        grid=(nb, L // tm),
        in_specs=[pl.BlockSpec((1, tm, d), lambda b, i: (b, i, 0)),
                  pl.BlockSpec((1, tm, r), lambda b, i: (b, i, 0)),
                  pl.BlockSpec((1, 1, d), lambda b, i: (b, 0, k_gate)),
                  pl.BlockSpec((1, d), lambda b, i: (0, 0)),
                  pl.BlockSpec((r, d), lambda b, i: (0, 0), pipeline_mode=pl.Buffered(1))],
        out_specs=pl.BlockSpec((1, tm, d), lambda b, i: (b, i, 0)),
        out_shape=jax.ShapeDtypeStruct((nb, L, d), _F32),
        compiler_params=_params(("arbitrary", "arbitrary"), 56),
        name="mixer_out",
    )(x, a, mods, g.reshape(1, d), w)


def _dft_tables(n):
    k = np.arange(n, dtype=np.int64)
    ang = 2.0 * np.pi * ((k[:, None] * k[None, :]) % n).astype(np.float64) / n
    return np.cos(ang), np.sin(ang)


def _chdft_kernel(x_ref, sh_ref, sc_ref, g_ref, cs_ref, o_ref, h_ref):
    _norm_mod_into(x_ref.at[0], h_ref, g_ref[...], sh_ref[0], sc_ref[0])
    d = h_ref.shape[1]
    for gi in range(d // FGROUP):
        lanes = slice(gi * FGROUP, (gi + 1) * FGROUP)
        t = _dot(h_ref[:, lanes], cs_ref[...])
        o_ref[0, 0, :, lanes] = t[:, :FGROUP].astype(_BF16)
        o_ref[0, 1, :, lanes] = t[:, FGROUP:].astype(_BF16)


def _chdft(x, mods, k0, g, cs, tm=1024):
    nb, L, d = x.shape
    mod_spec = lambda k: pl.BlockSpec((1, 1, d), lambda b, i: (b, 0, k))
    return pl.pallas_call(
        _chdft_kernel,
        grid=(nb, L // tm),
        in_specs=[pl.BlockSpec((1, tm, d), lambda b, i: (b, i, 0)),
                  mod_spec(k0), mod_spec(k0 + 1),
                  pl.BlockSpec((1, d), lambda b, i: (0, 0)),
                  pl.BlockSpec((FGROUP, 2 * FGROUP), lambda b, i: (0, 0))],
        out_specs=pl.BlockSpec((1, 2, tm, d), lambda b, i: (b, 0, i, 0)),
        out_shape=jax.ShapeDtypeStruct((nb, 2, L, d), _BF16),
        scratch_shapes=[pltpu.VMEM((tm, d), _BF16)],
        compiler_params=_params(("arbitrary", "arbitrary"), 48),
        name="fourier_channel_dft",
    )(x, mods, mods, g.reshape(1, d), cs)


def _seqdft_kernel(ac_ref, as_ref, ctop_ref, r_ref, bc_ref, bs_ref, o_ref, h_sc, *, scale, rc):
    half = ac_ref.shape[0]
    mb = MIRROR_BLOCK
    h_sc[half:half + mb, :] = (_dot(ctop_ref[...], bc_ref[...]) * scale).astype(_BF16)
    for r in range(half // rc):
        rows = slice(r * rc, (r + 1) * rc)
        p = _dot(ac_ref[rows, :], bc_ref[...])
        q = _dot(as_ref[rows, :], bs_ref[...])
        o_ref[0, 0, rows, :] = ((p - q) * scale).astype(_BF16)
        h_sc[rows, :] = ((p + q) * scale).astype(_BF16)
    nblk = half // mb
    for qb in range(nblk):
        src = h_sc[(nblk - 1 - qb) * mb:(nblk + 1 - qb) * mb, :]
        o_ref[0, 1, qb * mb:(qb + 1) * mb, :] = _dot(r_ref[...], src).astype(_BF16)


def _seqdft(xcs, scale, tn=1024, rc=512):
    nb, _, L, d = xcs.shape
    half = L // 2
    mb = MIRROR_BLOCK
    cos_s, sin_s = _dft_tables(L)
    a_cos = jnp.asarray(cos_s[:half], _BF16)
    a_sin = jnp.asarray(sin_s[:half], _BF16)
    c_top = jnp.asarray(cos_s[half:half + mb], _BF16)
    mirror = np.zeros((mb, 2 * mb), np.float32)
    mirror[np.arange(mb), mb - np.arange(mb)] = 1.0
    whole = lambda rows, cols: pl.BlockSpec((rows, cols), lambda b, n: (0, 0))
    b_spec = lambda part: pl.BlockSpec((None, None, L, tn), lambda b, n: (b, part, 0, n))
    return pl.pallas_call(
        functools.partial(_seqdft_kernel, scale=scale, rc=rc),
        grid=(nb, d // tn),
        in_specs=[whole(half, L), whole(half, L), whole(mb, L), whole(mb, 2 * mb),
                  b_spec(0), b_spec(1)],
        out_specs=pl.BlockSpec((1, 2, half, tn), lambda b, n: (b, 0, 0, n)),
        out_shape=jax.ShapeDtypeStruct((nb, 2, half, d), _BF16),
        scratch_shapes=[pltpu.VMEM((half + mb, tn), _BF16)],
        compiler_params=_params(("arbitrary", "arbitrary"), 48),
        name="fourier_seq_dft",
    )(a_cos, a_sin, c_top, jnp.asarray(mirror, _BF16), xcs, xcs)


def kernel(x, c, ctx, c_ctx, mod_w, mod_b, norm_g, ffn_w_in, ffn_w_out, rec_w_in, rec_conv_w,
           rec_conv_b, rec_gate_w, rec_gate_b, rec_lam, rec_w_out, fou_w_out):
    nb, s, d = x.shape
    lc = ctx.shape[1]
    depth = mod_w.shape[0]
    n_mixers = 2
    last_rec = ((depth - 1) // n_mixers) * n_mixers
    assert last_rec == 0, "context stream is only carried up to the first RG-LRU mixer"

    rows = 2 * V7X_SUBLANES
    cc = jnp.concatenate([c, c_ctx[None], jnp.zeros((rows - nb - 1, d), _F32)], axis=0)
    mods_all = _adaln(cc, mod_w, mod_b).reshape(depth, rows, 1, N_MOD * d)

    w_ffn = (ffn_w_in[0, 0].astype(_BF16), ffn_w_out[0, 0].astype(_BF16))
    order = [(i, sub) for i in range(depth) for sub in range(2)]

    def ffn_lat(x, mods, k0, g_pre, g_post, layer, sub):
        nonlocal w_ffn
        nxt = order.index((layer, sub)) + 1
        next_w = (ffn_w_in, ffn_w_out) + order[nxt] if nxt < len(order) else None
        out = _ffn(x, mods, lat_row, k0, g_pre, g_post, *w_ffn, next_w=next_w)
        if next_w is None:
            return out
        w_ffn = (out[1], out[2])
        return out[0]

    lat_row = lambda b: b
    ctx_row = lambda b: nb

    for i in range(depth):
        is_rec = (i % n_mixers) == 0
        j = i // n_mixers
        mods = mods_all[i]
        g = norm_g[i]
        w_pre = w_ffn
        x = ffn_lat(x, mods, 0, g[0], g[1], i, 0)
        if is_rec:
            ctx1 = _ffn(ctx.reshape(1, nb * lc, d), mods, ctx_row, 0, g[0], g[1], *w_pre)
            w_rec = rec_w_in[j].astype(_BF16)
            g_lat, v_lat = _recin(x, mods, lat_row, 3, g[2], w_rec, rec_conv_w[j], rec_conv_b[j],
                                  tm=1024, blocklen=GRID_W, need_gate=True)
            _, v_ctx = _recin(ctx1, mods, ctx_row, 3, g[2], w_rec, rec_conv_w[j], rec_conv_b[j],
                              tm=1024, blocklen=lc, need_gate=False)
            v_ctx = v_ctx.reshape(nb, lc, -1)
            gw = rec_gate_w[j]
            nh = gw.shape[2]
            w_cat = jnp.transpose(gw, (2, 3, 0, 1, 4)).reshape(nh, LRU_BLOCK, 4 * LRU_BLOCK)
            a = _scan(v_lat, v_ctx, g_lat, (0.5 * w_cat).astype(_BF16),
                      rec_gate_b[j].reshape(4, -1), rec_lam[j])
            w_mix = rec_w_out[j].astype(_BF16)
        else:
            cos_c, sin_c = _dft_tables(FGROUP)
            cs = jnp.asarray(np.concatenate([cos_c, sin_c], axis=1), _BF16)
            xcs = _chdft(x, mods, 3, g[2], cs)
            a = _seqdft(xcs, 1.0 / math.sqrt(s * FGROUP)).reshape(nb, s, d)
            w_mix = fou_w_out[j].astype(_BF16)
        x = _outproj(x, a, mods, 5, g[3], w_mix)
        x = ffn_lat(x, mods, 6, g[4], g[5], i, 1)
    return x
```

```python
import functools
import math

import jax
import jax.numpy as jnp
import numpy as np
from jax import lax
from jax.experimental import pallas as pl
from jax.experimental.pallas import tpu as pltpu

N_SUB = 3
N_MOD = 3 * N_SUB
MACARON = 0.5
GRID_W = 64
CONV_W = 4
CONV_LEFT = CONV_W // 2
LRU_BLOCK = 128
RG_C = 8.0
LOG2E = math.log2(math.e)
F32_TINY = float(np.finfo(np.float32).tiny)
FGROUP = 256
MIRROR_BLOCK = 128
EPS = 1e-6

V7X_SUBLANES = 8
ROW_CHUNK = 64
REC_ROW_CHUNK = 256
V7X_LANES = 128
MIB = 1024 * 1024

_BF16 = jnp.bfloat16
_F32 = jnp.float32


def _params(semantics, vmem_mib):
    return pltpu.CompilerParams(dimension_semantics=semantics,
                                vmem_limit_bytes=vmem_mib * MIB)


def _dot(a, b):
    return jnp.dot(a, b, preferred_element_type=_F32)


def _for_row_chunks(n_rows, fn, rn=ROW_CHUNK, unroll=2):
    def body(r, carry):
        fn(pl.ds(pl.multiple_of(r * rn, rn), rn))
        return carry
    lax.fori_loop(0, n_rows // rn, body, 0, unroll=unroll)


def _inv_rms(x):
    return lax.rsqrt(jnp.mean(x * x, axis=-1, keepdims=True) + EPS)


def _norm_mod_into(x_ref, h_ref, g, shift, scale):
    gm = g * (1.0 + scale)

    def fn(rows):
        inv = _inv_rms(x_ref[rows, :])
        h_ref[rows, :] = (x_ref[rows, :] * inv * gm + shift).astype(h_ref.dtype)
    _for_row_chunks(x_ref.shape[0], fn)


def _adaln_kernel(c_ref, w_ref, b_ref, o_ref):
    c = c_ref[...]
    a = (c * jax.nn.sigmoid(c)).astype(_BF16)
    o_ref[0] = _dot(a, w_ref[0].astype(_BF16)) + b_ref[0]


def _adaln(cc, mod_w, mod_b, tn=2048):
    depth, d, n = mod_w.shape
    rows = cc.shape[0]
    return pl.pallas_call(
        _adaln_kernel,
        grid=(depth, n // tn),
        in_specs=[pl.BlockSpec((rows, d), lambda l, j: (0, 0)),
                  pl.BlockSpec((1, d, tn), lambda l, j: (l, 0, j)),
                  pl.BlockSpec((1, 1, tn), lambda l, j: (l, 0, j))],
        out_specs=pl.BlockSpec((1, rows, tn), lambda l, j: (l, 0, j)),
        out_shape=jax.ShapeDtypeStruct((depth, rows, n), _F32),
        compiler_params=_params(("arbitrary", "arbitrary"), 48),
        name="adaln",
    )(cc, mod_w, mod_b.reshape(depth, 1, n))


def _ffn_kernel(x_ref, sh_ref, sc_ref, gt_ref, gpre_ref, gpost_ref, wg_ref, wu_ref, wo_ref,
                *rest, rc, rc_mid, nf, cast_next):
    if cast_next:
        nwi_ref, nwo_ref, o_ref, cwi_ref, cwo_ref, h_ref = rest
        cwi_ref[...] = nwi_ref[...].astype(_BF16)
        cwo_ref[...] = nwo_ref[...].astype(_BF16)
    else:
        o_ref, h_ref = rest
    j = pl.program_id(2)
    nj = pl.num_programs(2)
    tm = h_ref.shape[0]
    x2 = x_ref.at[0]
    o2 = o_ref.at[0]

    def sweep(chunk):
        start, size = chunk
        return [slice(q, q + ROW_CHUNK) for q in range(start, start + size, ROW_CHUNK)]

    def norm_rows(chunk, gm, shift):
        for rows in sweep(chunk):
            inv = _inv_rms(x2[rows, :])
            h_ref[rows, :] = (x2[rows, :] * inv * gm + shift).astype(_BF16)

    def residual_rows(chunk, gg):
        for rows in sweep(chunk):
            inv = _inv_rms(o2[rows, :])
            o2[rows, :] = x2[rows, :] + (o2[rows, :] * inv) * gg

    def step(first, last):
        size = rc if (first or last) else rc_mid
        chunks = [(start, size) for start in range(0, tm, size)]
        if first:
            gm = gpre_ref[...] * (1.0 + sc_ref[0])
            shift = sh_ref[0]
            norm_rows(chunks[0], gm, shift)
        if last:
            gg = (MACARON * gt_ref[0]) * gpost_ref[...]
        for r, (start, size) in enumerate(chunks):
            rows = slice(start, start + size)
            h = h_ref[rows, :]
            g = _dot(h, wg_ref[...])
            u = _dot(h, wu_ref[...])
            hg = 0.5 * g
            a = ((hg + hg * jnp.tanh(hg)) * u).astype(_BF16)
            y = _dot(a, wo_ref[...])
            if first:
                o2[rows, :] = y
                if r + 1 < len(chunks):
                    norm_rows(chunks[r + 1], gm, shift)
            else:
                o2[rows, :] += y
            if last and r > 0:
                residual_rows(chunks[r - 1], gg)
        if last:
            residual_rows(chunks[-1], gg)

    if nf == 1:
        step(True, True)
    else:
        pl.when(j == 0)(lambda: step(True, False))
        pl.when(j == nj - 1)(lambda: step(False, True))
        if nf > 2:
            pl.when((j > 0) & (j < nj - 1))(lambda: step(False, False))


def _ffn(x, mods, row_of, k0, g_pre, g_post, w_in, w_out, next_w=None, tm=1024, tf=512, rc=512,
         rc_mid=1024):
    nb, L, d = x.shape
    f = w_out.shape[0]
    nf = f // tf
    nt = nb * (L // tm)
    mod_spec = lambda k: pl.BlockSpec((1, 1, d), lambda b, i, j: (row_of(b), 0, k))
    vec_spec = pl.BlockSpec((1, d), lambda b, i, j: (0, 0))
    in_specs = [pl.BlockSpec((1, tm, d), lambda b, i, j: (b, i, 0)),
                mod_spec(k0), mod_spec(k0 + 1), mod_spec(k0 + 2),
                vec_spec, vec_spec,
                pl.BlockSpec((d, tf), lambda b, i, j: (0, j)),
                pl.BlockSpec((d, tf), lambda b, i, j: (0, nf + j)),
                pl.BlockSpec((tf, d), lambda b, i, j: (j, 0))]
    args = [x, mods, mods, mods, g_pre.reshape(1, d), g_post.reshape(1, d), w_in, w_in, w_out]
    out_specs = [pl.BlockSpec((1, tm, d), lambda b, i, j: (b, i, 0))]
    out_shape = [jax.ShapeDtypeStruct((nb, L, d), _F32)]
    if next_w is not None:
        nw_in, nw_out, layer, sub = next_w
        assert d % nt == 0 and (2 * f) % nf == 0 and f % (nt * nf) == 0
        ri, ci, ro = d // nt, 2 * f // nf, f // (nt * nf)
        tile = lambda i: i * (L // tm)
        in_specs += [pl.BlockSpec((None, None, ri, ci),
                                  lambda b, i, j: (layer, sub, tile(b) + i, j)),
                     pl.BlockSpec((None, None, ro, d),
                                  lambda b, i, j: (layer, sub, (tile(b) + i) * nf + j, 0))]
        args += [nw_in, nw_out]
        out_specs += [pl.BlockSpec((ri, ci), lambda b, i, j: (tile(b) + i, j)),
                      pl.BlockSpec((ro, d), lambda b, i, j: ((tile(b) + i) * nf + j, 0))]
        out_shape += [jax.ShapeDtypeStruct((d, 2 * f), _BF16), jax.ShapeDtypeStruct((f, d), _BF16)]
    out = pl.pallas_call(
        functools.partial(_ffn_kernel, rc=rc, rc_mid=rc_mid, nf=nf,
                          cast_next=next_w is not None),
        grid=(nb, L // tm, nf),
        in_specs=in_specs, out_specs=out_specs, out_shape=out_shape,
        scratch_shapes=[pltpu.VMEM((tm, d), _BF16)],
        compiler_params=_params(("arbitrary", "arbitrary", "arbitrary"), 60),
        name="ffn",
    )(*args)
    return out if next_w is not None else out[0]


def _conv_rows(v, cw, cb, blocklen):
    tm = v.shape[0]
    pos = lax.broadcasted_iota(jnp.int32, (tm, 1), 0) % blocklen
    out = cb + cw[CONV_LEFT:CONV_LEFT + 1] * v
    for k in range(CONV_W):
        off = k - CONV_LEFT
        if off == 0:
            continue
        shifted = pltpu.roll(v, (-off) % tm, axis=0)
        valid = (pos + off >= 0) & (pos + off < blocklen)
        out = out + cw[k:k + 1] * jnp.where(valid, shifted, 0.0)
    return out


def _recin_kernel(x_ref, sh_ref, sc_ref, g_ref, wg_ref, wv_ref, cw_ref, cb_ref,
                  og_ref, ov_ref, h_ref, *, blocklen):
    n = pl.program_id(2)

    tm = h_ref.shape[0]
    rc = min(tm, REC_ROW_CHUNK)
    assert rc % blocklen == 0 and tm % rc == 0

    x2 = x_ref.at[0]
    nr = tm // rc

    def norm_rows(r, gm, shift):
        for q in range(rc // ROW_CHUNK):
            rows = slice(r * rc + q * ROW_CHUNK, r * rc + (q + 1) * ROW_CHUNK)
            inv = _inv_rms(x2[rows, :])
            h_ref[rows, :] = (x2[rows, :] * inv * gm + shift).astype(_BF16)

    def step(first):
        if first:
            gm = g_ref[...] * (1.0 + sc_ref[0])
            shift = sh_ref[0]
            norm_rows(0, gm, shift)
        for r in range(nr):
            rows = slice(r * rc, (r + 1) * rc)
            h = h_ref[rows, :]
            v = _dot(h, wv_ref[...])
            ov_ref[0, rows, :] = _conv_rows(v, cw_ref[...], cb_ref[...], blocklen).astype(_BF16)
            if og_ref is not None:
                og_ref[0, rows, :] = _gelu_tanh(_dot(h, wg_ref[...])).astype(_BF16)
            if first and r + 1 < nr:
                norm_rows(r + 1, gm, shift)

    pl.when(n == 0)(lambda: step(True))
    pl.when(n > 0)(lambda: step(False))


def _recin_nogate_kernel(x_ref, sh_ref, sc_ref, g_ref, wv_ref, cw_ref, cb_ref, ov_ref, h_ref,
                         *, blocklen):
    _recin_kernel(x_ref, sh_ref, sc_ref, g_ref, None, wv_ref, cw_ref, cb_ref, None, ov_ref,
                  h_ref, blocklen=blocklen)


def _recin(x, mods, row_of, k0, g, w_in, conv_w, conv_b, tm, blocklen, need_gate, tn=512):
    nb, L, d = x.shape
    r = w_in.shape[1] // 2
    nn = r // tn
    mod_spec = lambda k: pl.BlockSpec((1, 1, d), lambda b, i, n: (row_of(b), 0, k))
    x_spec = pl.BlockSpec((1, tm, d), lambda b, i, n: (b, i, 0))
    vec_spec = pl.BlockSpec((1, d), lambda b, i, n: (0, 0))
    wg_spec = pl.BlockSpec((d, tn), lambda b, i, n: (0, n))
    wv_spec = pl.BlockSpec((d, tn), lambda b, i, n: (0, nn + n))
    cw_spec = pl.BlockSpec((CONV_W, tn), lambda b, i, n: (0, n))
    cb_spec = pl.BlockSpec((1, tn), lambda b, i, n: (0, n))
    o_spec = pl.BlockSpec((1, tm, tn), lambda b, i, n: (b, i, n))
    o_shape = jax.ShapeDtypeStruct((nb, L, r), _BF16)
    common = dict(grid=(nb, L // tm, nn),
                  scratch_shapes=[pltpu.VMEM((tm, d), _BF16)],
                  compiler_params=_params(("arbitrary", "arbitrary", "arbitrary"), 56))
    g2 = g.reshape(1, d)
    cb2 = conv_b.reshape(1, r)
    if need_gate:
        return pl.pallas_call(
            functools.partial(_recin_kernel, blocklen=blocklen),
            in_specs=[x_spec, mod_spec(k0), mod_spec(k0 + 1), vec_spec, wg_spec, wv_spec,
                      cw_spec, cb_spec],
            out_specs=[o_spec, o_spec], out_shape=[o_shape, o_shape],
            name="rec_in", **common,
        )(x, mods, mods, g2, w_in, w_in, conv_w, cb2)
    return None, pl.pallas_call(
        functools.partial(_recin_nogate_kernel, blocklen=blocklen),
        in_specs=[x_spec, mod_spec(k0), mod_spec(k0 + 1), vec_spec, wv_spec, cw_spec, cb_spec],
        out_specs=o_spec, out_shape=o_shape,
        name="rec_in_ctx", **common,
    )(x, mods, mods, g2, w_in, conv_w, cb2)


def _gelu_tanh(x):
    c = math.sqrt(2.0 / math.pi)
    hx = 0.5 * x
    return hx + hx * jnp.tanh(x * (c + (0.044715 * c) * (x * x)))


def _scan_kernel(v_ref, vc_ref, g_ref, w_ref, bias_ref, lam_ref, o_ref,
                 a_sc, b_sc, hl_sc, al_sc, *, nh):
    lc = vc_ref.shape[1]
    s = v_ref.shape[1]
    nseg = V7X_SUBLANES
    seg, cseg = s // nseg, lc // nseg
    cb = s

    neg_lam = -lam_ref[...]
    sp = jnp.maximum(neg_lam, 0.0) + jnp.log(1.0 + jnp.exp(-jnp.abs(neg_lam)))
    q = (-0.5 * RG_C * LOG2E) * sp
    hb = 0.5 * bias_ref[...]

    def coeffs(vb, pre, hh, store):
        lanes = slice(hh * LRU_BLOCK, (hh + 1) * LRU_BLOCK)
        vh = 0.5 * vb.astype(_F32)
        for d in range(2):
            c0 = (2 * d) * LRU_BLOCK
            ta = jnp.tanh(pre[:, c0:c0 + LRU_BLOCK] + hb[2 * d:2 * d + 1, lanes])
            tx = jnp.tanh(pre[:, c0 + LRU_BLOCK:c0 + 2 * LRU_BLOCK]
                          + hb[2 * d + 1:2 * d + 2, lanes])
            qd = q[d:d + 1, lanes]
            a = jnp.exp2(qd + qd * ta)
            m2 = 1.0 - a * a
            mult = m2 * lax.rsqrt(jnp.maximum(m2, F32_TINY))
            store(d, a, (mult * vh) * (1.0 + tx))

    def lat_coeffs(m, carry):
        src = pl.ds(pl.multiple_of(m * seg, seg), seg)
        dst = pl.ds(m, seg, stride=nseg)
        for hh in range(nh):
            vb = v_ref[0, src, hh * LRU_BLOCK:(hh + 1) * LRU_BLOCK]

            def store(d, a, b, hh=hh):
                a_sc[d, hh, dst, :] = a
                b_sc[d, hh, dst, :] = b
            coeffs(vb, _dot(vb, w_ref[hh]), hh, store)
        return carry

    lax.fori_loop(0, nseg, lat_coeffs, 0)

    for hh in range(nh):
        vb = vc_ref[0, :, hh * LRU_BLOCK:(hh + 1) * LRU_BLOCK]

        def store(d, a, b, hh=hh):
            for m in range(nseg):
                dst = pl.ds(cb + m, cseg, stride=nseg)
                a_sc[d, hh, dst, :] = a[m * cseg:(m + 1) * cseg]
                b_sc[d, hh, dst, :] = b[m * cseg:(m + 1) * cseg]
        coeffs(vb, _dot(vb, w_ref[hh]), hh, store)

    vshape = (nseg, V7X_LANES)
    row = lax.broadcasted_iota(jnp.int32, vshape, 0)
    zeros = tuple(jnp.zeros(vshape, _F32) for _ in range(nh))
    ones = tuple(jnp.ones(vshape, _F32) for _ in range(nh))

    def sublane_scan(a, b, reverse):
        for k in (1, 2, 4):
            keep = (row < nseg - k) if reverse else (row >= k)
            shift = (nseg - k) if reverse else k
            a_sh = jnp.where(keep, pltpu.roll(a, shift, axis=0), 1.0)
            b_sh = jnp.where(keep, pltpu.roll(b, shift, axis=0), 0.0)
            b = a * b_sh + b
            a = a * a_sh
        return a, b

    def local_pass(base, length, save):
        def body(i, carry):
            hs, prods = carry
            new_h, new_p = [], []
            for d, k in ((0, i), (1, length - 1 - i)):
                t = pl.ds(pl.multiple_of(k * nseg, nseg), nseg)
                idx = pl.ds(pl.multiple_of(base + k * nseg, nseg), nseg)
                for hh in range(nh):
                    a = a_sc[d, hh, idx, :]
                    h = a * hs[d * nh + hh] + b_sc[d, hh, idx, :]
                    pr = prods[d * nh + hh] * a
                    if save:
                        hl_sc[d, hh, t, :] = h
                        al_sc[d, hh, t, :] = pr
                    new_h.append(h)
                    new_p.append(pr)
            return tuple(new_h), tuple(new_p)
        return lax.fori_loop(0, length, body, (zeros + zeros, ones + ones), unroll=8)

    def entry_states(h_fin, a_tot, g_in, reverse):
        acum, bcum = sublane_scan(a_tot, h_fin, reverse)
        e = acum * g_in + bcum
        if reverse:
            entry = jnp.where(row < nseg - 1, pltpu.roll(e, nseg - 1, axis=0), g_in)
            final = e[0:1, :]
        else:
            entry = jnp.where(row >= 1, pltpu.roll(e, 1, axis=0), g_in)
            final = e[nseg - 1:nseg, :]
        return entry, jnp.broadcast_to(final, vshape)

    hc, pc_tot = local_pass(cb, cseg, save=False)
    hl, pl_tot = local_pass(0, seg, save=True)
    entries = []
    for d, reverse in ((0, False), (1, True)):
        for hh in range(nh):
            n = d * nh + hh
            _, g_ctx = entry_states(hc[n], pc_tot[n], zeros[hh], reverse)
            entry, _ = entry_states(hl[n], pl_tot[n], g_ctx, reverse)
            entries.append(entry)

    def fix(k, carry):
        t = pl.ds(pl.multiple_of(k * nseg, nseg), nseg)
        for hh in range(nh):
            hl_sc[0, hh, t, :] = ((hl_sc[0, hh, t, :] + al_sc[0, hh, t, :] * entries[hh])
                                  + (hl_sc[1, hh, t, :] + al_sc[1, hh, t, :] * entries[nh + hh]))
        return carry
    lax.fori_loop(0, seg, fix, 0, unroll=8)

    def gate_out(m, carry):
        rows = pl.ds(pl.multiple_of(m * seg, seg), seg)
        for hh in range(nh):
            lanes = slice(hh * LRU_BLOCK, (hh + 1) * LRU_BLOCK)
            y = hl_sc[0, hh, pl.ds(m, seg, stride=nseg), :]
            o_ref[0, rows, lanes] = (g_ref[0, rows, lanes].astype(_F32) * y).astype(_BF16)
        return carry

    lax.fori_loop(0, nseg, gate_out, 0)


def _scan(v_lat, v_ctx, g_lat, w_cat, bias, lam, cc=512):
    nb, s, r = v_lat.shape
    lc = v_ctx.shape[1]
    nh = cc // LRU_BLOCK
    coef_rows = s + lc
    blk = lambda L: pl.BlockSpec((1, L, cc), lambda b, c: (b, 0, c))
    return pl.pallas_call(
        functools.partial(_scan_kernel, nh=nh),
        grid=(nb, r // cc),
        in_specs=[blk(s), blk(lc), blk(s),
                  pl.BlockSpec((nh, LRU_BLOCK, 4 * LRU_BLOCK), lambda b, c: (c, 0, 0)),
                  pl.BlockSpec((4, cc), lambda b, c: (0, c)),
                  pl.BlockSpec((2, cc), lambda b, c: (0, c))],
        out_specs=blk(s),
        out_shape=jax.ShapeDtypeStruct((nb, s, r), _BF16),
        scratch_shapes=[pltpu.VMEM((2, nh, coef_rows, V7X_LANES), _F32),
                        pltpu.VMEM((2, nh, coef_rows, V7X_LANES), _F32),
                        pltpu.VMEM((2, nh, s, V7X_LANES), _F32),
                        pltpu.VMEM((2, nh, s, V7X_LANES), _F32)],
        compiler_params=_params(("arbitrary", "arbitrary"), 56),
        name="rglru_scan",
    )(v_lat, v_ctx, g_lat, w_cat, bias, lam)


def _outproj_kernel(x_ref, a_ref, gt_ref, g_ref, w_ref, o_ref, *, rc):
    tm = o_ref.shape[1]
    x2 = x_ref.at[0]
    o2 = o_ref.at[0]
    gg = gt_ref[0] * g_ref[...]

    def residual(r):
        for q in range(r * rc, (r + 1) * rc, ROW_CHUNK):
            rows = slice(q, q + ROW_CHUNK)
            inv = _inv_rms(o2[rows, :])
            o2[rows, :] = x2[rows, :] + (o2[rows, :] * inv) * gg

    for r in range(tm // rc):
        rows = slice(r * rc, (r + 1) * rc)
        o2[rows, :] = _dot(a_ref[0, rows, :], w_ref[...])
        if r > 0:
            residual(r - 1)
    residual(tm // rc - 1)


def _outproj(x, a, mods, k_gate, g, w, tm=1024, rc=256):
    nb, L, d = x.shape
    r = w.shape[0]
    return pl.pallas_call(
        functools.partial(_outproj_kernel, rc=rc),
        grid=(nb, L // tm),
        in_specs=[pl.BlockSpec((1, tm, d), lambda b, i: (b, i, 0)),
                  pl.BlockSpec((1, tm, r), lambda b, i: (b, i, 0)),
                  pl.BlockSpec((1, 1, d), lambda b, i: (b, 0, k_gate)),
                  pl.BlockSpec((1, d), lambda b, i: (0, 0)),
                  pl.BlockSpec((r, d), lambda b, i: (0, 0), pipeline_mode=pl.Buffered(1))],
        out_specs=pl.BlockSpec((1, tm, d), lambda b, i: (b, i, 0)),
        out_shape=jax.ShapeDtypeStruct((nb, L, d), _F32),
        compiler_params=_params(("arbitrary", "arbitrary"), 56),
        name="mixer_out",
    )(x, a, mods, g.reshape(1, d), w)


def _dft_tables(n):
    k = np.arange(n, dtype=np.int64)
    ang = 2.0 * np.pi * ((k[:, None] * k[None, :]) % n).astype(np.float64) / n
    return np.cos(ang), np.sin(ang)


def _chdft_kernel(x_ref, sh_ref, sc_ref, g_ref, cs_ref, o_ref, h_ref):
    _norm_mod_into(x_ref.at[0], h_ref, g_ref[...], sh_ref[0], sc_ref[0])
    d = h_ref.shape[1]
    for gi in range(d // FGROUP):
        lanes = slice(gi * FGROUP, (gi + 1) * FGROUP)
        t = _dot(h_ref[:, lanes], cs_ref[...])
        o_ref[0, 0, :, lanes] = t[:, :FGROUP].astype(_BF16)
        o_ref[0, 1, :, lanes] = t[:, FGROUP:].astype(_BF16)


def _chdft(x, mods, k0, g, cs, tm=1024):
    nb, L, d = x.shape
    mod_spec = lambda k: pl.BlockSpec((1, 1, d), lambda b, i: (b, 0, k))
    return pl.pallas_call(
        _chdft_kernel,
        grid=(nb, L // tm),
        in_specs=[pl.BlockSpec((1, tm, d), lambda b, i: (b, i, 0)),
                  mod_spec(k0), mod_spec(k0 + 1),
                  pl.BlockSpec((1, d), lambda b, i: (0, 0)),
                  pl.BlockSpec((FGROUP, 2 * FGROUP), lambda b, i: (0, 0))],
        out_specs=pl.BlockSpec((1, 2, tm, d), lambda b, i: (b, 0, i, 0)),
        out_shape=jax.ShapeDtypeStruct((nb, 2, L, d), _BF16),
        scratch_shapes=[pltpu.VMEM((tm, d), _BF16)],
        compiler_params=_params(("arbitrary", "arbitrary"), 48),
        name="fourier_channel_dft",
    )(x, mods, mods, g.reshape(1, d), cs)


def _seqdft_kernel(ac_ref, as_ref, ctop_ref, r_ref, bc_ref, bs_ref, o_ref, fc_sc, fs_sc, h_sc,
                   *, scale, rc):
    half = as_ref.shape[0]
    n = 2 * half
    mb = MIRROR_BLOCK
    nblk = half // mb
    for qb in range(nblk):
        rows = slice(qb * mb, (qb + 1) * mb)
        if qb == 0:
            rm, src = r_ref[:, :mb], slice(n - mb, n)
        else:
            rm, src = r_ref[...], slice(n - (qb + 1) * mb, n - (qb - 1) * mb)
        fc_sc[rows, :] = (bc_ref[rows, :].astype(_F32) + _dot(rm, bc_ref[src, :])).astype(_BF16)
        fs_sc[rows, :] = (bs_ref[rows, :].astype(_F32) - _dot(rm, bs_ref[src, :])).astype(_BF16)
    fc_sc[half:half + mb, :] = bc_ref[half:half + mb, :]

    h_sc[half:half + mb, :] = (_dot(ctop_ref[...], fc_sc[...]) * scale).astype(_BF16)
    for r in range(half // rc):
        rows = slice(r * rc, (r + 1) * rc)
        p = _dot(ac_ref[rows, :], fc_sc[...])
        q = _dot(as_ref[rows, :], fs_sc[...])
        o_ref[0, 0, rows, :] = ((p - q) * scale).astype(_BF16)
        h_sc[rows, :] = ((p + q) * scale).astype(_BF16)
    for qb in range(nblk):
        src = h_sc[(nblk - 1 - qb) * mb:(nblk + 1 - qb) * mb, :]
        o_ref[0, 1, qb * mb:(qb + 1) * mb, :] = _dot(r_ref[...], src).astype(_BF16)


def _seqdft(xcs, scale, tn=1024, rc=512):
    nb, _, L, d = xcs.shape
    half = L // 2
    mb = MIRROR_BLOCK
    cos_s, sin_s = _dft_tables(L)
    kc = half + mb
    cos_f = np.zeros((half + mb, kc))
    cos_f[:, :half + 1] = cos_s[:half + mb, :half + 1]
    a_cos = jnp.asarray(cos_f[:half], _BF16)
    a_sin = jnp.asarray(sin_s[:half, :half], _BF16)
    c_top = jnp.asarray(cos_f[half:], _BF16)
    mirror = np.zeros((mb, 2 * mb), np.float32)
    mirror[np.arange(mb), mb - np.arange(mb)] = 1.0
    whole = lambda rows, cols: pl.BlockSpec((rows, cols), lambda b, n: (0, 0))
    b_spec = lambda part: pl.BlockSpec((None, None, L, tn), lambda b, n: (b, part, 0, n))
    return pl.pallas_call(
        functools.partial(_seqdft_kernel, scale=scale, rc=rc),
        grid=(nb, d // tn),
        in_specs=[whole(half, kc), whole(half, half), whole(mb, kc), whole(mb, 2 * mb),
                  b_spec(0), b_spec(1)],
        out_specs=pl.BlockSpec((1, 2, half, tn), lambda b, n: (b, 0, 0, n)),
        out_shape=jax.ShapeDtypeStruct((nb, 2, half, d), _BF16),
        scratch_shapes=[pltpu.VMEM((kc, tn), _BF16), pltpu.VMEM((half, tn), _BF16),
                        pltpu.VMEM((half + mb, tn), _BF16)],
        compiler_params=_params(("arbitrary", "arbitrary"), 48),
        name="fourier_seq_dft",
    )(a_cos, a_sin, c_top, jnp.asarray(mirror, _BF16), xcs, xcs)


def kernel(x, c, ctx, c_ctx, mod_w, mod_b, norm_g, ffn_w_in, ffn_w_out, rec_w_in, rec_conv_w,
           rec_conv_b, rec_gate_w, rec_gate_b, rec_lam, rec_w_out, fou_w_out):
    nb, s, d = x.shape
    lc = ctx.shape[1]
    depth = mod_w.shape[0]
    n_mixers = 2
    last_rec = ((depth - 1) // n_mixers) * n_mixers
    assert last_rec == 0, "context stream is only carried up to the first RG-LRU mixer"

    rows = 2 * V7X_SUBLANES
    cc = jnp.concatenate([c, c_ctx[None], jnp.zeros((rows - nb - 1, d), _F32)], axis=0)
    mods_all = _adaln(cc, mod_w, mod_b).reshape(depth, rows, 1, N_MOD * d)

    w_ffn = (ffn_w_in[0, 0].astype(_BF16), ffn_w_out[0, 0].astype(_BF16))
    order = [(i, sub) for i in range(depth) for sub in range(2)]

    def ffn_lat(x, mods, k0, g_pre, g_post, layer, sub):
        nonlocal w_ffn
        nxt = order.index((layer, sub)) + 1
        next_w = (ffn_w_in, ffn_w_out) + order[nxt] if nxt < len(order) else None
        out = _ffn(x, mods, lat_row, k0, g_pre, g_post, *w_ffn, next_w=next_w)
        if next_w is None:
            return out
        w_ffn = (out[1], out[2])
        return out[0]

    lat_row = lambda b: b
    ctx_row = lambda b: nb

    for i in range(depth):
        is_rec = (i % n_mixers) == 0
        j = i // n_mixers
        mods = mods_all[i]
        g = norm_g[i]
        w_pre = w_ffn
        x = ffn_lat(x, mods, 0, g[0], g[1], i, 0)
        if is_rec:
            ctx1 = _ffn(ctx.reshape(1, nb * lc, d), mods, ctx_row, 0, g[0], g[1], *w_pre)
            w_rec = rec_w_in[j].astype(_BF16)
            g_lat, v_lat = _recin(x, mods, lat_row, 3, g[2], w_rec, rec_conv_w[j], rec_conv_b[j],
                                  tm=1024, blocklen=GRID_W, need_gate=True)
            _, v_ctx = _recin(ctx1, mods, ctx_row, 3, g[2], w_rec, rec_conv_w[j], rec_conv_b[j],
                              tm=1024, blocklen=lc, need_gate=False)
            v_ctx = v_ctx.reshape(nb, lc, -1)
            gw = rec_gate_w[j]
            nh = gw.shape[2]
            w_cat = jnp.transpose(gw, (2, 3, 0, 1, 4)).reshape(nh, LRU_BLOCK, 4 * LRU_BLOCK)
            a = _scan(v_lat, v_ctx, g_lat, (0.5 * w_cat).astype(_BF16),
                      rec_gate_b[j].reshape(4, -1), rec_lam[j])
            w_mix = rec_w_out[j].astype(_BF16)
        else:
            cos_c, sin_c = _dft_tables(FGROUP)
            cs = jnp.asarray(np.concatenate([cos_c, sin_c], axis=1), _BF16)
            xcs = _chdft(x, mods, 3, g[2], cs)
            a = _seqdft(xcs, 1.0 / math.sqrt(s * FGROUP)).reshape(nb, s, d)
            w_mix = fou_w_out[j].astype(_BF16)
        x = _outproj(x, a, mods, 5, g[3], w_mix)
        x = ffn_lat(x, mods, 6, g[4], g[5], i, 1)
    return x
```

```python
import functools
import math

import jax
import jax.numpy as jnp
import numpy as np
from jax import lax
from jax.experimental import pallas as pl
from jax.experimental.pallas import tpu as pltpu

N_SUB = 3
N_MOD = 3 * N_SUB
MACARON = 0.5
GRID_W = 64
CONV_W = 4
CONV_LEFT = CONV_W // 2
LRU_BLOCK = 128
RG_C = 8.0
LOG2E = math.log2(math.e)
F32_TINY = float(np.finfo(np.float32).tiny)
FGROUP = 256
MIRROR_BLOCK = 128
EPS = 1e-6

V7X_SUBLANES = 8
ROW_CHUNK = 64
REC_ROW_CHUNK = 256
V7X_LANES = 128
MIB = 1024 * 1024

_BF16 = jnp.bfloat16
_F32 = jnp.float32


def _params(semantics, vmem_mib):
    return pltpu.CompilerParams(dimension_semantics=semantics,
                                vmem_limit_bytes=vmem_mib * MIB)


def _dot(a, b):
    return jnp.dot(a, b, preferred_element_type=_F32)


def _for_row_chunks(n_rows, fn, rn=ROW_CHUNK, unroll=2):
    def body(r, carry):
        fn(pl.ds(pl.multiple_of(r * rn, rn), rn))
        return carry
    lax.fori_loop(0, n_rows // rn, body, 0, unroll=unroll)


def _inv_rms(x):
    return lax.rsqrt(jnp.mean(x * x, axis=-1, keepdims=True) + EPS)


def _norm_mod_into(x_ref, h_ref, g, shift, scale):
    gm = g * (1.0 + scale)

    def fn(rows):
        inv = _inv_rms(x_ref[rows, :])
        h_ref[rows, :] = (x_ref[rows, :] * inv * gm + shift).astype(h_ref.dtype)
    _for_row_chunks(x_ref.shape[0], fn)


def _adaln_kernel(c_ref, w_ref, b_ref, o_ref):
    c = c_ref[...]
    a = (c * jax.nn.sigmoid(c)).astype(_BF16)
    o_ref[0] = _dot(a, w_ref[0].astype(_BF16)) + b_ref[0]


def _adaln(cc, mod_w, mod_b, tn=2048):
    depth, d, n = mod_w.shape
    rows = cc.shape[0]
    return pl.pallas_call(
        _adaln_kernel,
        grid=(depth, n // tn),
        in_specs=[pl.BlockSpec((rows, d), lambda l, j: (0, 0)),
                  pl.BlockSpec((1, d, tn), lambda l, j: (l, 0, j)),
                  pl.BlockSpec((1, 1, tn), lambda l, j: (l, 0, j))],
        out_specs=pl.BlockSpec((1, rows, tn), lambda l, j: (l, 0, j)),
        out_shape=jax.ShapeDtypeStruct((depth, rows, n), _F32),
        compiler_params=_params(("arbitrary", "arbitrary"), 48),
        name="adaln",
    )(cc, mod_w, mod_b.reshape(depth, 1, n))


def _ffn_kernel(x_ref, sh_ref, sc_ref, gt_ref, gpre_ref, gpost_ref, wg_ref, wu_ref, wo_ref,
                *rest, rc, rc_mid, nf, cast_next):
    if cast_next:
        nwi_ref, nwo_ref, o_ref, cwi_ref, cwo_ref, h_ref = rest
        cwi_ref[...] = nwi_ref[...].astype(_BF16)
        cwo_ref[...] = nwo_ref[...].astype(_BF16)
    else:
        o_ref, h_ref = rest
    j = pl.program_id(2)
    nj = pl.num_programs(2)
    tm = h_ref.shape[0]
    x2 = x_ref.at[0]
    o2 = o_ref.at[0]

    def sweep(chunk):
        start, size = chunk
        return [slice(q, q + ROW_CHUNK) for q in range(start, start + size, ROW_CHUNK)]

    def norm_rows(chunk, gm, shift):
        for rows in sweep(chunk):
            inv = _inv_rms(x2[rows, :])
            h_ref[rows, :] = (x2[rows, :] * inv * gm + shift).astype(_BF16)

    def residual_rows(chunk, gg):
        for rows in sweep(chunk):
            inv = _inv_rms(o2[rows, :])
            o2[rows, :] = x2[rows, :] + (o2[rows, :] * inv) * gg

    def step(first, last):
        size = rc if (first or last) else rc_mid
        chunks = [(start, size) for start in range(0, tm, size)]
        if first:
            gm = gpre_ref[...] * (1.0 + sc_ref[0])
            shift = sh_ref[0]
            norm_rows(chunks[0], gm, shift)
        if last:
            gg = (MACARON * gt_ref[0]) * gpost_ref[...]
        for r, (start, size) in enumerate(chunks):
            rows = slice(start, start + size)
            h = h_ref[rows, :]
            g = _dot(h, wg_ref[...])
            u = _dot(h, wu_ref[...])
            hg = 0.5 * g
            a = ((hg + hg * jnp.tanh(hg)) * u).astype(_BF16)
            y = _dot(a, wo_ref[...])
            if first:
                o2[rows, :] = y
                if r + 1 < len(chunks):
                    norm_rows(chunks[r + 1], gm, shift)
            else:
                o2[rows, :] += y
            if last and r > 0:
                residual_rows(chunks[r - 1], gg)
        if last:
            residual_rows(chunks[-1], gg)

    if nf == 1:
        step(True, True)
    else:
        pl.when(j == 0)(lambda: step(True, False))
        pl.when(j == nj - 1)(lambda: step(False, True))
        if nf > 2:
            pl.when((j > 0) & (j < nj - 1))(lambda: step(False, False))


def _ffn(x, mods, row_of, k0, g_pre, g_post, w_in, w_out, next_w=None, tm=1024, tf=512, rc=512,
         rc_mid=1024):
    nb, L, d = x.shape
    f = w_out.shape[0]
    nf = f // tf
    nt = nb * (L // tm)
    mod_spec = lambda k: pl.BlockSpec((1, 1, d), lambda b, i, j: (row_of(b), 0, k))
    vec_spec = pl.BlockSpec((1, d), lambda b, i, j: (0, 0))
    in_specs = [pl.BlockSpec((1, tm, d), lambda b, i, j: (b, i, 0)),
                mod_spec(k0), mod_spec(k0 + 1), mod_spec(k0 + 2),
                vec_spec, vec_spec,
                pl.BlockSpec((d, tf), lambda b, i, j: (0, j)),
                pl.BlockSpec((d, tf), lambda b, i, j: (0, nf + j)),
                pl.BlockSpec((tf, d), lambda b, i, j: (j, 0))]
    args = [x, mods, mods, mods, g_pre.reshape(1, d), g_post.reshape(1, d), w_in, w_in, w_out]
    out_specs = [pl.BlockSpec((1, tm, d), lambda b, i, j: (b, i, 0))]
    out_shape = [jax.ShapeDtypeStruct((nb, L, d), _F32)]
    if next_w is not None:
        nw_in, nw_out, layer, sub = next_w
        assert d % nt == 0 and (2 * f) % nf == 0 and f % (nt * nf) == 0
        ri, ci, ro = d // nt, 2 * f // nf, f // (nt * nf)
        tile = lambda i: i * (L // tm)
        in_specs += [pl.BlockSpec((None, None, ri, ci),
                                  lambda b, i, j: (layer, sub, tile(b) + i, j)),
                     pl.BlockSpec((None, None, ro, d),
                                  lambda b, i, j: (layer, sub, (tile(b) + i) * nf + j, 0))]
        args += [nw_in, nw_out]
        out_specs += [pl.BlockSpec((ri, ci), lambda b, i, j: (tile(b) + i, j)),
                      pl.BlockSpec((ro, d), lambda b, i, j: ((tile(b) + i) * nf + j, 0))]
        out_shape += [jax.ShapeDtypeStruct((d, 2 * f), _BF16), jax.ShapeDtypeStruct((f, d), _BF16)]
    out = pl.pallas_call(
        functools.partial(_ffn_kernel, rc=rc, rc_mid=rc_mid, nf=nf,
                          cast_next=next_w is not None),
        grid=(nb, L // tm, nf),
        in_specs=in_specs, out_specs=out_specs, out_shape=out_shape,
        scratch_shapes=[pltpu.VMEM((tm, d), _BF16)],
        compiler_params=_params(("arbitrary", "arbitrary", "arbitrary"), 60),
        name="ffn",
    )(*args)
    return out if next_w is not None else out[0]


def _conv_rows(v, cw, cb, blocklen):
    tm = v.shape[0]
    pos = lax.broadcasted_iota(jnp.int32, (tm, 1), 0) % blocklen
    out = cb + cw[CONV_LEFT:CONV_LEFT + 1] * v
    for k in range(CONV_W):
        off = k - CONV_LEFT
        if off == 0:
            continue
        shifted = pltpu.roll(v, (-off) % tm, axis=0)
        valid = (pos + off >= 0) & (pos + off < blocklen)
        out = out + cw[k:k + 1] * jnp.where(valid, shifted, 0.0)
    return out


def _recin_kernel(x_ref, sh_ref, sc_ref, g_ref, wg_ref, wv_ref, cw_ref, cb_ref,
                  og_ref, ov_ref, h_ref, *, blocklen):
    n = pl.program_id(2)

    tm = h_ref.shape[0]
    rc = min(tm, REC_ROW_CHUNK)
    assert rc % blocklen == 0 and tm % rc == 0

    x2 = x_ref.at[0]
    nr = tm // rc

    def norm_rows(r, gm, shift):
        for q in range(rc // ROW_CHUNK):
            rows = slice(r * rc + q * ROW_CHUNK, r * rc + (q + 1) * ROW_CHUNK)
            inv = _inv_rms(x2[rows, :])
            h_ref[rows, :] = (x2[rows, :] * inv * gm + shift).astype(_BF16)

    def step(first):
        if first:
            gm = g_ref[...] * (1.0 + sc_ref[0])
            shift = sh_ref[0]
            norm_rows(0, gm, shift)
        for r in range(nr):
            rows = slice(r * rc, (r + 1) * rc)
            h = h_ref[rows, :]
            if og_ref is not None:
                og_ref[0, rows, :] = _gelu_tanh(_dot(h, wg_ref[...])).astype(_BF16)
            v = _dot(h, wv_ref[...])
            ov_ref[0, rows, :] = _conv_rows(v, cw_ref[...], cb_ref[...], blocklen).astype(_BF16)
            if first and r + 1 < nr:
                norm_rows(r + 1, gm, shift)

    pl.when(n == 0)(lambda: step(True))
    pl.when(n > 0)(lambda: step(False))


def _recin_nogate_kernel(x_ref, sh_ref, sc_ref, g_ref, wv_ref, cw_ref, cb_ref, ov_ref, h_ref,
                         *, blocklen):
    _recin_kernel(x_ref, sh_ref, sc_ref, g_ref, None, wv_ref, cw_ref, cb_ref, None, ov_ref,
                  h_ref, blocklen=blocklen)


def _recin(x, mods, row_of, k0, g, w_in, conv_w, conv_b, tm, blocklen, need_gate, tn=512):
    nb, L, d = x.shape
    r = w_in.shape[1] // 2
    nn = r // tn
    mod_spec = lambda k: pl.BlockSpec((1, 1, d), lambda b, i, n: (row_of(b), 0, k))
    x_spec = pl.BlockSpec((1, tm, d), lambda b, i, n: (b, i, 0))
    vec_spec = pl.BlockSpec((1, d), lambda b, i, n: (0, 0))
    wg_spec = pl.BlockSpec((d, tn), lambda b, i, n: (0, n))
    wv_spec = pl.BlockSpec((d, tn), lambda b, i, n: (0, nn + n))
    cw_spec = pl.BlockSpec((CONV_W, tn), lambda b, i, n: (0, n))
    cb_spec = pl.BlockSpec((1, tn), lambda b, i, n: (0, n))
    o_spec = pl.BlockSpec((1, tm, tn), lambda b, i, n: (b, i, n))
    o_shape = jax.ShapeDtypeStruct((nb, L, r), _BF16)
    common = dict(grid=(nb, L // tm, nn),
                  scratch_shapes=[pltpu.VMEM((tm, d), _BF16)],
                  compiler_params=_params(("arbitrary", "arbitrary", "arbitrary"), 56))
    g2 = g.reshape(1, d)
    cb2 = conv_b.reshape(1, r)
    if need_gate:
        return pl.pallas_call(
            functools.partial(_recin_kernel, blocklen=blocklen),
            in_specs=[x_spec, mod_spec(k0), mod_spec(k0 + 1), vec_spec, wg_spec, wv_spec,
                      cw_spec, cb_spec],
            out_specs=[o_spec, o_spec], out_shape=[o_shape, o_shape],
            name="rec_in", **common,
        )(x, mods, mods, g2, w_in, w_in, conv_w, cb2)
    return None, pl.pallas_call(
        functools.partial(_recin_nogate_kernel, blocklen=blocklen),
        in_specs=[x_spec, mod_spec(k0), mod_spec(k0 + 1), vec_spec, wv_spec, cw_spec, cb_spec],
        out_specs=o_spec, out_shape=o_shape,
        name="rec_in_ctx", **common,
    )(x, mods, mods, g2, w_in, conv_w, cb2)


def _gelu_tanh(x):
    c = math.sqrt(2.0 / math.pi)
    hx = 0.5 * x
    return hx + hx * jnp.tanh(x * (c + (0.044715 * c) * (x * x)))


def _scan_kernel(v_ref, vc_ref, g_ref, w_ref, bias_ref, lam_ref, o_ref,
                 a_sc, b_sc, hl_sc, al_sc, *, nh):
    lc = vc_ref.shape[1]
    s = v_ref.shape[1]
    nseg = V7X_SUBLANES
    seg, cseg = s // nseg, lc // nseg
    cb = s

    neg_lam = -lam_ref[...]
    sp = jnp.maximum(neg_lam, 0.0) + jnp.log(1.0 + jnp.exp(-jnp.abs(neg_lam)))
    q = (-0.5 * RG_C * LOG2E) * sp
    hb = 0.5 * bias_ref[...]

    def coeffs(vb, pre, hh, store):
        lanes = slice(hh * LRU_BLOCK, (hh + 1) * LRU_BLOCK)
        vh = 0.5 * vb.astype(_F32)
        for d in range(2):
            c0 = (2 * d) * LRU_BLOCK
            ta = jnp.tanh(pre[:, c0:c0 + LRU_BLOCK] + hb[2 * d:2 * d + 1, lanes])
            tx = jnp.tanh(pre[:, c0 + LRU_BLOCK:c0 + 2 * LRU_BLOCK]
                          + hb[2 * d + 1:2 * d + 2, lanes])
            qd = q[d:d + 1, lanes]
            a = jnp.exp2(qd + qd * ta)
            m2 = 1.0 - a * a
            mult = m2 * lax.rsqrt(jnp.maximum(m2, F32_TINY))
            store(d, a, (mult * vh) * (1.0 + tx))

    def lat_coeffs(m, carry):
        src = pl.ds(pl.multiple_of(m * seg, seg), seg)
        dst = pl.ds(m, seg, stride=nseg)
        for hh in range(nh):
            vb = v_ref[0, src, hh * LRU_BLOCK:(hh + 1) * LRU_BLOCK]

            def store(d, a, b, hh=hh):
                a_sc[d, hh, dst, :] = a
                b_sc[d, hh, dst, :] = b
            coeffs(vb, _dot(vb, w_ref[hh]), hh, store)
        return carry

    lax.fori_loop(0, nseg, lat_coeffs, 0)

    for hh in range(nh):
        vb = vc_ref[0, :, hh * LRU_BLOCK:(hh + 1) * LRU_BLOCK]

        def store(d, a, b, hh=hh):
            for m in range(nseg):
                dst = pl.ds(cb + m, cseg, stride=nseg)
                a_sc[d, hh, dst, :] = a[m * cseg:(m + 1) * cseg]
                b_sc[d, hh, dst, :] = b[m * cseg:(m + 1) * cseg]
        coeffs(vb, _dot(vb, w_ref[hh]), hh, store)

    vshape = (nseg, V7X_LANES)
    row = lax.broadcasted_iota(jnp.int32, vshape, 0)
    zeros = tuple(jnp.zeros(vshape, _F32) for _ in range(nh))
    ones = tuple(jnp.ones(vshape, _F32) for _ in range(nh))

    def sublane_scan(a, b, reverse):
        for k in (1, 2, 4):
            keep = (row < nseg - k) if reverse else (row >= k)
            shift = (nseg - k) if reverse else k
            a_sh = jnp.where(keep, pltpu.roll(a, shift, axis=0), 1.0)
            b_sh = jnp.where(keep, pltpu.roll(b, shift, axis=0), 0.0)
            b = a * b_sh + b
            a = a * a_sh
        return a, b

    def local_pass(base, length, save):
        def body(i, carry):
            hs, prods = carry
            new_h, new_p = [], []
            for d, k in ((0, i), (1, length - 1 - i)):
                t = pl.ds(pl.multiple_of(k * nseg, nseg), nseg)
                idx = pl.ds(pl.multiple_of(base + k * nseg, nseg), nseg)
                for hh in range(nh):
                    a = a_sc[d, hh, idx, :]
                    h = a * hs[d * nh + hh] + b_sc[d, hh, idx, :]
                    pr = prods[d * nh + hh] * a
                    if save:
                        hl_sc[d, hh, t, :] = h
                        al_sc[d, hh, t, :] = pr
                    new_h.append(h)
                    new_p.append(pr)
            return tuple(new_h), tuple(new_p)
        return lax.fori_loop(0, length, body, (zeros + zeros, ones + ones), unroll=8)

    def entry_states(h_fin, a_tot, g_in, reverse):
        acum, bcum = sublane_scan(a_tot, h_fin, reverse)
        e = acum * g_in + bcum
        if reverse:
            entry = jnp.where(row < nseg - 1, pltpu.roll(e, nseg - 1, axis=0), g_in)
            final = e[0:1, :]
        else:
            entry = jnp.where(row >= 1, pltpu.roll(e, 1, axis=0), g_in)
            final = e[nseg - 1:nseg, :]
        return entry, jnp.broadcast_to(final, vshape)

    hc, pc_tot = local_pass(cb, cseg, save=False)
    hl, pl_tot = local_pass(0, seg, save=True)
    entries = []
    for d, reverse in ((0, False), (1, True)):
        for hh in range(nh):
            n = d * nh + hh
            _, g_ctx = entry_states(hc[n], pc_tot[n], zeros[hh], reverse)
            entry, _ = entry_states(hl[n], pl_tot[n], g_ctx, reverse)
            entries.append(entry)

    def fix(k, carry):
        t = pl.ds(pl.multiple_of(k * nseg, nseg), nseg)
        for hh in range(nh):
            hl_sc[0, hh, t, :] = ((hl_sc[0, hh, t, :] + al_sc[0, hh, t, :] * entries[hh])
                                  + (hl_sc[1, hh, t, :] + al_sc[1, hh, t, :] * entries[nh + hh]))
        return carry
    lax.fori_loop(0, seg, fix, 0, unroll=8)

    def gate_out(m, carry):
        rows = pl.ds(pl.multiple_of(m * seg, seg), seg)
        for hh in range(nh):
            lanes = slice(hh * LRU_BLOCK, (hh + 1) * LRU_BLOCK)
            y = hl_sc[0, hh, pl.ds(m, seg, stride=nseg), :]
            o_ref[0, rows, lanes] = (g_ref[0, rows, lanes].astype(_F32) * y).astype(_BF16)
        return carry

    lax.fori_loop(0, nseg, gate_out, 0)


def _scan(v_lat, v_ctx, g_lat, w_cat, bias, lam, cc=512):
    nb, s, r = v_lat.shape
    lc = v_ctx.shape[1]
    nh = cc // LRU_BLOCK
    coef_rows = s + lc
    blk = lambda L: pl.BlockSpec((1, L, cc), lambda b, c: (b, 0, c))
    return pl.pallas_call(
        functools.partial(_scan_kernel, nh=nh),
        grid=(nb, r // cc),
        in_specs=[blk(s), blk(lc), blk(s),
                  pl.BlockSpec((nh, LRU_BLOCK, 4 * LRU_BLOCK), lambda b, c: (c, 0, 0)),
                  pl.BlockSpec((4, cc), lambda b, c: (0, c)),
                  pl.BlockSpec((2, cc), lambda b, c: (0, c))],
        out_specs=blk(s),
        out_shape=jax.ShapeDtypeStruct((nb, s, r), _BF16),
        scratch_shapes=[pltpu.VMEM((2, nh, coef_rows, V7X_LANES), _F32),
                        pltpu.VMEM((2, nh, coef_rows, V7X_LANES), _F32),
                        pltpu.VMEM((2, nh, s, V7X_LANES), _F32),
                        pltpu.VMEM((2, nh, s, V7X_LANES), _F32)],
        compiler_params=_params(("arbitrary", "arbitrary"), 56),
        name="rglru_scan",
    )(v_lat, v_ctx, g_lat, w_cat, bias, lam)


def _outproj_kernel(x_ref, a_ref, gt_ref, g_ref, w_ref, o_ref, *, rc):
    tm = o_ref.shape[1]
    x2 = x_ref.at[0]
    o2 = o_ref.at[0]
    gg = gt_ref[0] * g_ref[...]

    def residual(r):
        for q in range(r * rc, (r + 1) * rc, ROW_CHUNK):
            rows = slice(q, q + ROW_CHUNK)
            inv = _inv_rms(o2[rows, :])
            o2[rows, :] = x2[rows, :] + (o2[rows, :] * inv) * gg

    for r in range(tm // rc):
        rows = slice(r * rc, (r + 1) * rc)
        o2[rows, :] = _dot(a_ref[0, rows, :], w_ref[...])
        if r > 0:
            residual(r - 1)
    residual(tm // rc - 1)


def _outproj(x, a, mods, k_gate, g, w, tm=1024, rc=256):
    nb, L, d = x.shape
    r = w.shape[0]
    return pl.pallas_call(
        functools.partial(_outproj_kernel, rc=rc),
        grid=(nb, L // tm),
        in_specs=[pl.BlockSpec((1, tm, d), lambda b, i: (b, i, 0)),
                  pl.BlockSpec((1, tm, r), lambda b, i: (b, i, 0)),
                  pl.BlockSpec((1, 1, d), lambda b, i: (b, 0, k_gate)),
                  pl.BlockSpec((1, d), lambda b, i: (0, 0)),
                  pl.BlockSpec((r, d), lambda b, i: (0, 0), pipeline_mode=pl.Buffered(1))],
        out_specs=pl.BlockSpec((1, tm, d), lambda b, i: (b, i, 0)),
        out_shape=jax.ShapeDtypeStruct((nb, L, d), _F32),
        compiler_params=_params(("arbitrary", "arbitrary"), 56),
        name="mixer_out",
    )(x, a, mods, g.reshape(1, d), w)


def _dft_tables(n):
    k = np.arange(n, dtype=np.int64)
    ang = 2.0 * np.pi * ((k[:, None] * k[None, :]) % n).astype(np.float64) / n
    return np.cos(ang), np.sin(ang)


def _chdft_kernel(x_ref, sh_ref, sc_ref, g_ref, cs_ref, o_ref, h_ref):
    _norm_mod_into(x_ref.at[0], h_ref, g_ref[...], sh_ref[0], sc_ref[0])
    d = h_ref.shape[1]
    for gi in range(d // FGROUP):
        lanes = slice(gi * FGROUP, (gi + 1) * FGROUP)
        t = _dot(h_ref[:, lanes], cs_ref[...])
        o_ref[0, 0, :, lanes] = t[:, :FGROUP].astype(_BF16)
        o_ref[0, 1, :, lanes] = t[:, FGROUP:].astype(_BF16)


def _chdft(x, mods, k0, g, cs, tm=1024):
    nb, L, d = x.shape
    mod_spec = lambda k: pl.BlockSpec((1, 1, d), lambda b, i: (b, 0, k))
    return pl.pallas_call(
        _chdft_kernel,
        grid=(nb, L // tm),
        in_specs=[pl.BlockSpec((1, tm, d), lambda b, i: (b, i, 0)),
                  mod_spec(k0), mod_spec(k0 + 1),
                  pl.BlockSpec((1, d), lambda b, i: (0, 0)),
                  pl.BlockSpec((FGROUP, 2 * FGROUP), lambda b, i: (0, 0))],
        out_specs=pl.BlockSpec((1, 2, tm, d), lambda b, i: (b, 0, i, 0)),
        out_shape=jax.ShapeDtypeStruct((nb, 2, L, d), _BF16),
        scratch_shapes=[pltpu.VMEM((tm, d), _BF16)],
        compiler_params=_params(("arbitrary", "arbitrary"), 48),
        name="fourier_channel_dft",
    )(x, mods, mods, g.reshape(1, d), cs)


def _seqdft_kernel(ac_ref, as_ref, ctop_ref, r_ref, bc_ref, bs_ref, o_ref, fc_sc, fs_sc, h_sc,
                   *, scale, rc):
    half = as_ref.shape[0]
    n = 2 * half
    mb = MIRROR_BLOCK
    nblk = half // mb
    for qb in range(nblk):
        rows = slice(qb * mb, (qb + 1) * mb)
        if qb == 0:
            rm, src = r_ref[:, :mb], slice(n - mb, n)
        else:
            rm, src = r_ref[...], slice(n - (qb + 1) * mb, n - (qb - 1) * mb)
        fc_sc[rows, :] = (bc_ref[rows, :].astype(_F32) + _dot(rm, bc_ref[src, :])).astype(_BF16)
        fs_sc[rows, :] = (bs_ref[rows, :].astype(_F32) - _dot(rm, bs_ref[src, :])).astype(_BF16)
    fc_sc[half:half + mb, :] = bc_ref[half:half + mb, :]

    h_sc[half:half + mb, :] = (_dot(ctop_ref[...], fc_sc[...]) * scale).astype(_BF16)
    for r in range(half // rc):
        rows = slice(r * rc, (r + 1) * rc)
        p = _dot(ac_ref[rows, :], fc_sc[...])
        q = _dot(as_ref[rows, :], fs_sc[...])
        o_ref[0, 0, rows, :] = ((p - q) * scale).astype(_BF16)
        h_sc[rows, :] = ((p + q) * scale).astype(_BF16)
    for qb in range(nblk):
        src = h_sc[(nblk - 1 - qb) * mb:(nblk + 1 - qb) * mb, :]
        o_ref[0, 1, qb * mb:(qb + 1) * mb, :] = _dot(r_ref[...], src).astype(_BF16)


def _seqdft(xcs, scale, tn=1024, rc=512):
    nb, _, L, d = xcs.shape
    half = L // 2
    mb = MIRROR_BLOCK
    cos_s, sin_s = _dft_tables(L)
    kc = half + mb
    cos_f = np.zeros((half + mb, kc))
    cos_f[:, :half + 1] = cos_s[:half + mb, :half + 1]
    a_cos = jnp.asarray(cos_f[:half], _BF16)
    a_sin = jnp.asarray(sin_s[:half, :half], _BF16)
    c_top = jnp.asarray(cos_f[half:], _BF16)
    mirror = np.zeros((mb, 2 * mb), np.float32)
    mirror[np.arange(mb), mb - np.arange(mb)] = 1.0
    whole = lambda rows, cols: pl.BlockSpec((rows, cols), lambda b, n: (0, 0))
    b_spec = lambda part: pl.BlockSpec((None, None, L, tn), lambda b, n: (b, part, 0, n))
    return pl.pallas_call(
        functools.partial(_seqdft_kernel, scale=scale, rc=rc),
        grid=(nb, d // tn),
        in_specs=[whole(half, kc), whole(half, half), whole(mb, kc), whole(mb, 2 * mb),
                  b_spec(0), b_spec(1)],
        out_specs=pl.BlockSpec((1, 2, half, tn), lambda b, n: (b, 0, 0, n)),
        out_shape=jax.ShapeDtypeStruct((nb, 2, half, d), _BF16),
        scratch_shapes=[pltpu.VMEM((kc, tn), _BF16), pltpu.VMEM((half, tn), _BF16),
                        pltpu.VMEM((half + mb, tn), _BF16)],
        compiler_params=_params(("arbitrary", "arbitrary"), 48),
        name="fourier_seq_dft",
    )(a_cos, a_sin, c_top, jnp.asarray(mirror, _BF16), xcs, xcs)


def kernel(x, c, ctx, c_ctx, mod_w, mod_b, norm_g, ffn_w_in, ffn_w_out, rec_w_in, rec_conv_w,
           rec_conv_b, rec_gate_w, rec_gate_b, rec_lam, rec_w_out, fou_w_out):
    nb, s, d = x.shape
    lc = ctx.shape[1]
    depth = mod_w.shape[0]
    n_mixers = 2
    last_rec = ((depth - 1) // n_mixers) * n_mixers
    assert last_rec == 0, "context stream is only carried up to the first RG-LRU mixer"

    rows = 2 * V7X_SUBLANES
    cc = jnp.concatenate([c, c_ctx[None], jnp.zeros((rows - nb - 1, d), _F32)], axis=0)
    mods_all = _adaln(cc, mod_w, mod_b).reshape(depth, rows, 1, N_MOD * d)

    w_ffn = (ffn_w_in[0, 0].astype(_BF16), ffn_w_out[0, 0].astype(_BF16))
    order = [(i, sub) for i in range(depth) for sub in range(2)]

    def ffn_lat(x, mods, k0, g_pre, g_post, layer, sub):
        nonlocal w_ffn
        nxt = order.index((layer, sub)) + 1
        next_w = (ffn_w_in, ffn_w_out) + order[nxt] if nxt < len(order) else None
        out = _ffn(x, mods, lat_row, k0, g_pre, g_post, *w_ffn, next_w=next_w)
        if next_w is None:
            return out
        w_ffn = (out[1], out[2])
        return out[0]

    lat_row = lambda b: b
    ctx_row = lambda b: nb

    for i in range(depth):
        is_rec = (i % n_mixers) == 0
        j = i // n_mixers
        mods = mods_all[i]
        g = norm_g[i]
        w_pre = w_ffn
        x = ffn_lat(x, mods, 0, g[0], g[1], i, 0)
        if is_rec:
            ctx1 = _ffn(ctx.reshape(1, nb * lc, d), mods, ctx_row, 0, g[0], g[1], *w_pre)
            w_rec = rec_w_in[j].astype(_BF16)
            g_lat, v_lat = _recin(x, mods, lat_row, 3, g[2], w_rec, rec_conv_w[j], rec_conv_b[j],
                                  tm=1024, blocklen=GRID_W, need_gate=True)
            _, v_ctx = _recin(ctx1, mods, ctx_row, 3, g[2], w_rec, rec_conv_w[j], rec_conv_b[j],
                              tm=1024, blocklen=lc, need_gate=False)
            v_ctx = v_ctx.reshape(nb, lc, -1)
            gw = rec_gate_w[j]
            nh = gw.shape[2]
            w_cat = jnp.transpose(gw, (2, 3, 0, 1, 4)).reshape(nh, LRU_BLOCK, 4 * LRU_BLOCK)
            a = _scan(v_lat, v_ctx, g_lat, (0.5 * w_cat).astype(_BF16),
                      rec_gate_b[j].reshape(4, -1), rec_lam[j])
            w_mix = rec_w_out[j].astype(_BF16)
        else:
            cos_c, sin_c = _dft_tables(FGROUP)
            cs = jnp.asarray(np.concatenate([cos_c, sin_c], axis=1), _BF16)
            xcs = _chdft(x, mods, 3, g[2], cs)
            a = _seqdft(xcs, 1.0 / math.sqrt(s * FGROUP)).reshape(nb, s, d)
            w_mix = fou_w_out[j].astype(_BF16)
        x = _outproj(x, a, mods, 5, g[3], w_mix)
        x = ffn_lat(x, mods, 6, g[4], g[5], i, 1)
    return x
```

```python
import functools
import math

import jax
import jax.numpy as jnp
import numpy as np
from jax import lax
from jax.experimental import pallas as pl
from jax.experimental.pallas import tpu as pltpu

N_SUB = 3
N_MOD = 3 * N_SUB
MACARON = 0.5
GRID_W = 64
CONV_W = 4
CONV_LEFT = CONV_W // 2
LRU_BLOCK = 128
RG_C = 8.0
LOG2E = math.log2(math.e)
F32_TINY = float(np.finfo(np.float32).tiny)
FGROUP = 256
MIRROR_BLOCK = 128
EPS = 1e-6

V7X_SUBLANES = 8
ROW_CHUNK = 64
REC_ROW_CHUNK = 256
REC_COL_CHUNK = 256
V7X_LANES = 128
MIB = 1024 * 1024

_BF16 = jnp.bfloat16
_F32 = jnp.float32


def _params(semantics, vmem_mib):
    return pltpu.CompilerParams(dimension_semantics=semantics,
                                vmem_limit_bytes=vmem_mib * MIB)


def _dot(a, b):
    return jnp.dot(a, b, preferred_element_type=_F32)


def _for_row_chunks(n_rows, fn, rn=ROW_CHUNK, unroll=2):
    def body(r, carry):
        fn(pl.ds(pl.multiple_of(r * rn, rn), rn))
        return carry
    lax.fori_loop(0, n_rows // rn, body, 0, unroll=unroll)


def _inv_rms(x):
    return lax.rsqrt(jnp.mean(x * x, axis=-1, keepdims=True) + EPS)


def _norm_mod_into(x_ref, h_ref, g, shift, scale):
    gm = g * (1.0 + scale)

    def fn(rows):
        inv = _inv_rms(x_ref[rows, :])
        h_ref[rows, :] = (x_ref[rows, :] * inv * gm + shift).astype(h_ref.dtype)
    _for_row_chunks(x_ref.shape[0], fn)


def _adaln_kernel(c_ref, w_ref, b_ref, o_ref):
    c = c_ref[...]
    a = (c * jax.nn.sigmoid(c)).astype(_BF16)
    o_ref[0] = _dot(a, w_ref[0].astype(_BF16)) + b_ref[0]


def _adaln(cc, mod_w, mod_b, tn=2048):
    depth, d, n = mod_w.shape
    rows = cc.shape[0]
    return pl.pallas_call(
        _adaln_kernel,
        grid=(depth, n // tn),
        in_specs=[pl.BlockSpec((rows, d), lambda l, j: (0, 0)),
                  pl.BlockSpec((1, d, tn), lambda l, j: (l, 0, j)),
                  pl.BlockSpec((1, 1, tn), lambda l, j: (l, 0, j))],
        out_specs=pl.BlockSpec((1, rows, tn), lambda l, j: (l, 0, j)),
        out_shape=jax.ShapeDtypeStruct((depth, rows, n), _F32),
        compiler_params=_params(("arbitrary", "arbitrary"), 48),
        name="adaln",
    )(cc, mod_w, mod_b.reshape(depth, 1, n))


def _ffn_kernel(x_ref, sh_ref, sc_ref, gt_ref, gpre_ref, gpost_ref, wg_ref, wu_ref, wo_ref,
                *rest, rc, rc_mid, nf, cast_next):
    if cast_next:
        nwi_ref, nwo_ref, o_ref, cwi_ref, cwo_ref, h_ref = rest
        cwi_ref[...] = nwi_ref[...].astype(_BF16)
        cwo_ref[...] = nwo_ref[...].astype(_BF16)
    else:
        o_ref, h_ref = rest
    j = pl.program_id(2)
    nj = pl.num_programs(2)
    tm = h_ref.shape[0]
    x2 = x_ref.at[0]
    o2 = o_ref.at[0]

    def sweep(chunk):
        start, size = chunk
        return [slice(q, q + ROW_CHUNK) for q in range(start, start + size, ROW_CHUNK)]

    def norm_rows(chunk, gm, shift):
        for rows in sweep(chunk):
            inv = _inv_rms(x2[rows, :])
            h_ref[rows, :] = (x2[rows, :] * inv * gm + shift).astype(_BF16)

    def residual_rows(chunk, gg):
        for rows in sweep(chunk):
            inv = _inv_rms(o2[rows, :])
            o2[rows, :] = x2[rows, :] + (o2[rows, :] * inv) * gg

    def step(first, last):
        size = rc if (first or last) else rc_mid
        chunks = [(start, size) for start in range(0, tm, size)]
        if first:
            gm = gpre_ref[...] * (1.0 + sc_ref[0])
            shift = sh_ref[0]
            norm_rows(chunks[0], gm, shift)
        if last:
            gg = (MACARON * gt_ref[0]) * gpost_ref[...]
        for r, (start, size) in enumerate(chunks):
            rows = slice(start, start + size)
            h = h_ref[rows, :]
            g = _dot(h, wg_ref[...])
            u = _dot(h, wu_ref[...])
            hg = 0.5 * g
            a = ((hg + hg * jnp.tanh(hg)) * u).astype(_BF16)
            y = _dot(a, wo_ref[...])
            if first:
                o2[rows, :] = y
                if r + 1 < len(chunks):
                    norm_rows(chunks[r + 1], gm, shift)
            else:
                o2[rows, :] += y
            if last and r > 0:
                residual_rows(chunks[r - 1], gg)
        if last:
            residual_rows(chunks[-1], gg)

    if nf == 1:
        step(True, True)
    else:
        pl.when(j == 0)(lambda: step(True, False))
        pl.when(j == nj - 1)(lambda: step(False, True))
        if nf > 2:
            pl.when((j > 0) & (j < nj - 1))(lambda: step(False, False))


def _ffn(x, mods, row_of, k0, g_pre, g_post, w_in, w_out, next_w=None, tm=1024, tf=512, rc=512,
         rc_mid=1024):
    nb, L, d = x.shape
    f = w_out.shape[0]
    nf = f // tf
    nt = nb * (L // tm)
    mod_spec = lambda k: pl.BlockSpec((1, 1, d), lambda b, i, j: (row_of(b), 0, k))
    vec_spec = pl.BlockSpec((1, d), lambda b, i, j: (0, 0))
    in_specs = [pl.BlockSpec((1, tm, d), lambda b, i, j: (b, i, 0)),
                mod_spec(k0), mod_spec(k0 + 1), mod_spec(k0 + 2),
                vec_spec, vec_spec,
                pl.BlockSpec((d, tf), lambda b, i, j: (0, j)),
                pl.BlockSpec((d, tf), lambda b, i, j: (0, nf + j)),
                pl.BlockSpec((tf, d), lambda b, i, j: (j, 0))]
    args = [x, mods, mods, mods, g_pre.reshape(1, d), g_post.reshape(1, d), w_in, w_in, w_out]
    out_specs = [pl.BlockSpec((1, tm, d), lambda b, i, j: (b, i, 0))]
    out_shape = [jax.ShapeDtypeStruct((nb, L, d), _F32)]
    if next_w is not None:
        nw_in, nw_out, layer, sub = next_w
        assert d % nt == 0 and (2 * f) % nf == 0 and f % (nt * nf) == 0
        ri, ci, ro = d // nt, 2 * f // nf, f // (nt * nf)
        tile = lambda i: i * (L // tm)
        in_specs += [pl.BlockSpec((None, None, ri, ci),
                                  lambda b, i, j: (layer, sub, tile(b) + i, j)),
                     pl.BlockSpec((None, None, ro, d),
                                  lambda b, i, j: (layer, sub, (tile(b) + i) * nf + j, 0))]
        args += [nw_in, nw_out]
        out_specs += [pl.BlockSpec((ri, ci), lambda b, i, j: (tile(b) + i, j)),
                      pl.BlockSpec((ro, d), lambda b, i, j: ((tile(b) + i) * nf + j, 0))]
        out_shape += [jax.ShapeDtypeStruct((d, 2 * f), _BF16), jax.ShapeDtypeStruct((f, d), _BF16)]
    out = pl.pallas_call(
        functools.partial(_ffn_kernel, rc=rc, rc_mid=rc_mid, nf=nf,
                          cast_next=next_w is not None),
        grid=(nb, L // tm, nf),
        in_specs=in_specs, out_specs=out_specs, out_shape=out_shape,
        scratch_shapes=[pltpu.VMEM((tm, d), _BF16)],
        compiler_params=_params(("arbitrary", "arbitrary", "arbitrary"), 60),
        name="ffn",
    )(*args)
    return out if next_w is not None else out[0]


def _conv_rows(v, cw, cb, blocklen):
    tm = v.shape[0]
    pos = lax.broadcasted_iota(jnp.int32, (tm, 1), 0) % blocklen
    out = cb + cw[CONV_LEFT:CONV_LEFT + 1] * v
    for k in range(CONV_W):
        off = k - CONV_LEFT
        if off == 0:
            continue
        shifted = pltpu.roll(v, (-off) % tm, axis=0)
        valid = (pos + off >= 0) & (pos + off < blocklen)
        out = out + cw[k:k + 1] * jnp.where(valid, shifted, 0.0)
    return out


def _recin_kernel(x_ref, sh_ref, sc_ref, g_ref, wg_ref, wv_ref, cw_ref, cb_ref,
                  og_ref, ov_ref, h_ref, *, blocklen):
    n = pl.program_id(2)

    tm = h_ref.shape[0]
    rc = min(tm, REC_ROW_CHUNK)
    assert rc % blocklen == 0 and tm % rc == 0

    x2 = x_ref.at[0]
    nr = tm // rc

    def norm_rows(r, gm, shift):
        for q in range(rc // ROW_CHUNK):
            rows = slice(r * rc + q * ROW_CHUNK, r * rc + (q + 1) * ROW_CHUNK)
            inv = _inv_rms(x2[rows, :])
            h_ref[rows, :] = (x2[rows, :] * inv * gm + shift).astype(_BF16)

    def step(first):
        if not first:
            h = h_ref[...]
            for c0 in range(0, ov_ref.shape[2], REC_COL_CHUNK):
                cols = slice(c0, c0 + REC_COL_CHUNK)
                if og_ref is not None:
                    og_ref[0, :, cols] = _gelu_tanh(_dot(h, wg_ref[:, cols])).astype(_BF16)
                v = _dot(h, wv_ref[:, cols])
                ov_ref[0, :, cols] = _conv_rows(v, cw_ref[:, cols], cb_ref[:, cols],
                                                blocklen).astype(_BF16)
            return
        if first:
            gm = g_ref[...] * (1.0 + sc_ref[0])
            shift = sh_ref[0]
            norm_rows(0, gm, shift)
        for r in range(nr):
            rows = slice(r * rc, (r + 1) * rc)
            h = h_ref[rows, :]
            if og_ref is not None:
                og_ref[0, rows, :] = _gelu_tanh(_dot(h, wg_ref[...])).astype(_BF16)
            v = _dot(h, wv_ref[...])
            ov_ref[0, rows, :] = _conv_rows(v, cw_ref[...], cb_ref[...], blocklen).astype(_BF16)
            if first and r + 1 < nr:
                norm_rows(r + 1, gm, shift)

    pl.when(n == 0)(lambda: step(True))
    pl.when(n > 0)(lambda: step(False))


def _recin_nogate_kernel(x_ref, sh_ref, sc_ref, g_ref, wv_ref, cw_ref, cb_ref, ov_ref, h_ref,
                         *, blocklen):
    _recin_kernel(x_ref, sh_ref, sc_ref, g_ref, None, wv_ref, cw_ref, cb_ref, None, ov_ref,
                  h_ref, blocklen=blocklen)


def _recin(x, mods, row_of, k0, g, w_in, conv_w, conv_b, tm, blocklen, need_gate, tn=512):
    nb, L, d = x.shape
    r = w_in.shape[1] // 2
    nn = r // tn
    mod_spec = lambda k: pl.BlockSpec((1, 1, d), lambda b, i, n: (row_of(b), 0, k))
    x_spec = pl.BlockSpec((1, tm, d), lambda b, i, n: (b, i, 0))
    vec_spec = pl.BlockSpec((1, d), lambda b, i, n: (0, 0))
    wg_spec = pl.BlockSpec((d, tn), lambda b, i, n: (0, n))
    wv_spec = pl.BlockSpec((d, tn), lambda b, i, n: (0, nn + n))
    cw_spec = pl.BlockSpec((CONV_W, tn), lambda b, i, n: (0, n))
    cb_spec = pl.BlockSpec((1, tn), lambda b, i, n: (0, n))
    o_spec = pl.BlockSpec((1, tm, tn), lambda b, i, n: (b, i, n))
    o_shape = jax.ShapeDtypeStruct((nb, L, r), _BF16)
    common = dict(grid=(nb, L // tm, nn),
                  scratch_shapes=[pltpu.VMEM((tm, d), _BF16)],
                  compiler_params=_params(("arbitrary", "arbitrary", "arbitrary"), 56))
    g2 = g.reshape(1, d)
    cb2 = conv_b.reshape(1, r)
    if need_gate:
        return pl.pallas_call(
            functools.partial(_recin_kernel, blocklen=blocklen),
            in_specs=[x_spec, mod_spec(k0), mod_spec(k0 + 1), vec_spec, wg_spec, wv_spec,
                      cw_spec, cb_spec],
            out_specs=[o_spec, o_spec], out_shape=[o_shape, o_shape],
            name="rec_in", **common,
        )(x, mods, mods, g2, w_in, w_in, conv_w, cb2)
    return None, pl.pallas_call(
        functools.partial(_recin_nogate_kernel, blocklen=blocklen),
        in_specs=[x_spec, mod_spec(k0), mod_spec(k0 + 1), vec_spec, wv_spec, cw_spec, cb_spec],
        out_specs=o_spec, out_shape=o_shape,
        name="rec_in_ctx", **common,
    )(x, mods, mods, g2, w_in, conv_w, cb2)


def _gelu_tanh(x):
    c = math.sqrt(2.0 / math.pi)
    hx = 0.5 * x
    return hx + hx * jnp.tanh(x * (c + (0.044715 * c) * (x * x)))


def _scan_kernel(v_ref, vc_ref, g_ref, w_ref, bias_ref, lam_ref, o_ref,
                 a_sc, b_sc, hl_sc, al_sc, *, nh):
    lc = vc_ref.shape[1]
    s = v_ref.shape[1]
    nseg = V7X_SUBLANES
    seg, cseg = s // nseg, lc // nseg
    cb = s

    neg_lam = -lam_ref[...]
    sp = jnp.maximum(neg_lam, 0.0) + jnp.log(1.0 + jnp.exp(-jnp.abs(neg_lam)))
    q = (-0.5 * RG_C * LOG2E) * sp
    hb = 0.5 * bias_ref[...]

    def coeffs(vb, pre, hh, store):
        lanes = slice(hh * LRU_BLOCK, (hh + 1) * LRU_BLOCK)
        vh = 0.5 * vb.astype(_F32)
        for d in range(2):
            c0 = (2 * d) * LRU_BLOCK
            ta = jnp.tanh(pre[:, c0:c0 + LRU_BLOCK] + hb[2 * d:2 * d + 1, lanes])
            tx = jnp.tanh(pre[:, c0 + LRU_BLOCK:c0 + 2 * LRU_BLOCK]
                          + hb[2 * d + 1:2 * d + 2, lanes])
            qd = q[d:d + 1, lanes]
            a = jnp.exp2(qd + qd * ta)
            m2 = 1.0 - a * a
            mult = m2 * lax.rsqrt(jnp.maximum(m2, F32_TINY))
            store(d, a, (mult * vh) * (1.0 + tx))

    def lat_coeffs(m, carry):
        src = pl.ds(pl.multiple_of(m * seg, seg), seg)
        dst = pl.ds(m, seg, stride=nseg)
        for hh in range(nh):
            vb = v_ref[0, src, hh * LRU_BLOCK:(hh + 1) * LRU_BLOCK]

            def store(d, a, b, hh=hh):
                a_sc[d, hh, dst, :] = a
                b_sc[d, hh, dst, :] = b
            coeffs(vb, _dot(vb, w_ref[hh]), hh, store)
        return carry

    lax.fori_loop(0, nseg, lat_coeffs, 0)

    for hh in range(nh):
        vb = vc_ref[0, :, hh * LRU_BLOCK:(hh + 1) * LRU_BLOCK]

        def store(d, a, b, hh=hh):
            for m in range(nseg):
                dst = pl.ds(cb + m, cseg, stride=nseg)
                a_sc[d, hh, dst, :] = a[m * cseg:(m + 1) * cseg]
                b_sc[d, hh, dst, :] = b[m * cseg:(m + 1) * cseg]
        coeffs(vb, _dot(vb, w_ref[hh]), hh, store)

    vshape = (nseg, V7X_LANES)
    row = lax.broadcasted_iota(jnp.int32, vshape, 0)
    zeros = tuple(jnp.zeros(vshape, _F32) for _ in range(nh))
    ones = tuple(jnp.ones(vshape, _F32) for _ in range(nh))

    def sublane_scan(a, b, reverse):
        for k in (1, 2, 4):
            keep = (row < nseg - k) if reverse else (row >= k)
            shift = (nseg - k) if reverse else k
            a_sh = jnp.where(keep, pltpu.roll(a, shift, axis=0), 1.0)
            b_sh = jnp.where(keep, pltpu.roll(b, shift, axis=0), 0.0)
            b = a * b_sh + b
            a = a * a_sh
        return a, b

    def local_pass(base, length, save):
        def body(i, carry):
            hs, prods = carry
            new_h, new_p = [], []
            for d, k in ((0, i), (1, length - 1 - i)):
                t = pl.ds(pl.multiple_of(k * nseg, nseg), nseg)
                idx = pl.ds(pl.multiple_of(base + k * nseg, nseg), nseg)
                for hh in range(nh):
                    a = a_sc[d, hh, idx, :]
                    h = a * hs[d * nh + hh] + b_sc[d, hh, idx, :]
                    pr = prods[d * nh + hh] * a
                    if save:
                        hl_sc[d, hh, t, :] = h
                        al_sc[d, hh, t, :] = pr
                    new_h.append(h)
                    new_p.append(pr)
            return tuple(new_h), tuple(new_p)
        return lax.fori_loop(0, length, body, (zeros + zeros, ones + ones), unroll=8)

    def entry_states(h_fin, a_tot, g_in, reverse):
        acum, bcum = sublane_scan(a_tot, h_fin, reverse)
        e = acum * g_in + bcum
        if reverse:
            entry = jnp.where(row < nseg - 1, pltpu.roll(e, nseg - 1, axis=0), g_in)
            final = e[0:1, :]
        else:
            entry = jnp.where(row >= 1, pltpu.roll(e, 1, axis=0), g_in)
            final = e[nseg - 1:nseg, :]
        return entry, jnp.broadcast_to(final, vshape)

    hc, pc_tot = local_pass(cb, cseg, save=False)
    hl, pl_tot = local_pass(0, seg, save=True)
    entries = []
    for d, reverse in ((0, False), (1, True)):
        for hh in range(nh):
            n = d * nh + hh
            _, g_ctx = entry_states(hc[n], pc_tot[n], zeros[hh], reverse)
            entry, _ = entry_states(hl[n], pl_tot[n], g_ctx, reverse)
            entries.append(entry)

    def fix(k, carry):
        t = pl.ds(pl.multiple_of(k * nseg, nseg), nseg)
        for hh in range(nh):
            hl_sc[0, hh, t, :] = ((hl_sc[0, hh, t, :] + al_sc[0, hh, t, :] * entries[hh])
                                  + (hl_sc[1, hh, t, :] + al_sc[1, hh, t, :] * entries[nh + hh]))
        return carry
    lax.fori_loop(0, seg, fix, 0, unroll=8)

    def gate_out(m, carry):
        rows = pl.ds(pl.multiple_of(m * seg, seg), seg)
        for hh in range(nh):
            lanes = slice(hh * LRU_BLOCK, (hh + 1) * LRU_BLOCK)
            y = hl_sc[0, hh, pl.ds(m, seg, stride=nseg), :]
            o_ref[0, rows, lanes] = (g_ref[0, rows, lanes].astype(_F32) * y).astype(_BF16)
        return carry

    lax.fori_loop(0, nseg, gate_out, 0)


def _scan(v_lat, v_ctx, g_lat, w_cat, bias, lam, cc=512):
    nb, s, r = v_lat.shape
    lc = v_ctx.shape[1]
    nh = cc // LRU_BLOCK
    coef_rows = s + lc
    blk = lambda L: pl.BlockSpec((1, L, cc), lambda b, c: (b, 0, c))
    return pl.pallas_call(
        functools.partial(_scan_kernel, nh=nh),
        grid=(nb, r // cc),
        in_specs=[blk(s), blk(lc), blk(s),
                  pl.BlockSpec((nh, LRU_BLOCK, 4 * LRU_BLOCK), lambda b, c: (c, 0, 0)),
                  pl.BlockSpec((4, cc), lambda b, c: (0, c)),
                  pl.BlockSpec((2, cc), lambda b, c: (0, c))],
        out_specs=blk(s),
        out_shape=jax.ShapeDtypeStruct((nb, s, r), _BF16),
        scratch_shapes=[pltpu.VMEM((2, nh, coef_rows, V7X_LANES), _F32),
                        pltpu.VMEM((2, nh, coef_rows, V7X_LANES), _F32),
                        pltpu.VMEM((2, nh, s, V7X_LANES), _F32),
                        pltpu.VMEM((2, nh, s, V7X_LANES), _F32)],
        compiler_params=_params(("arbitrary", "arbitrary"), 56),
        name="rglru_scan",
    )(v_lat, v_ctx, g_lat, w_cat, bias, lam)


def _outproj_kernel(x_ref, a_ref, gt_ref, g_ref, w_ref, o_ref, *, rc):
    tm = o_ref.shape[1]
    x2 = x_ref.at[0]
    o2 = o_ref.at[0]
    gg = gt_ref[0] * g_ref[...]

    def residual(r):
        for q in range(r * rc, (r + 1) * rc, ROW_CHUNK):
            rows = slice(q, q + ROW_CHUNK)
            inv = _inv_rms(o2[rows, :])
            o2[rows, :] = x2[rows, :] + (o2[rows, :] * inv) * gg

    for r in range(tm // rc):
        rows = slice(r * rc, (r + 1) * rc)
        o2[rows, :] = _dot(a_ref[0, rows, :], w_ref[...])
        if r > 0:
            residual(r - 1)
    residual(tm // rc - 1)


def _outproj(x, a, mods, k_gate, g, w, tm=1024, rc=256):
    nb, L, d = x.shape
    r = w.shape[0]
    return pl.pallas_call(
        functools.partial(_outproj_kernel, rc=rc),
        grid=(nb, L // tm),
        in_specs=[pl.BlockSpec((1, tm, d), lambda b, i: (b, i, 0)),
                  pl.BlockSpec((1, tm, r), lambda b, i: (b, i, 0)),
                  pl.BlockSpec((1, 1, d), lambda b, i: (b, 0, k_gate)),
                  pl.BlockSpec((1, d), lambda b, i: (0, 0)),
                  pl.BlockSpec((r, d), lambda b, i: (0, 0), pipeline_mode=pl.Buffered(1))],
        out_specs=pl.BlockSpec((1, tm, d), lambda b, i: (b, i, 0)),
        out_shape=jax.ShapeDtypeStruct((nb, L, d), _F32),
        compiler_params=_params(("arbitrary", "arbitrary"), 56),
        name="mixer_out",
    )(x, a, mods, g.reshape(1, d), w)


def _dft_tables(n):
    k = np.arange(n, dtype=np.int64)
    ang = 2.0 * np.pi * ((k[:, None] * k[None, :]) % n).astype(np.float64) / n
    return np.cos(ang), np.sin(ang)


def _chdft_kernel(x_ref, sh_ref, sc_ref, g_ref, cs_ref, o_ref, h_ref):
    _norm_mod_into(x_ref.at[0], h_ref, g_ref[...], sh_ref[0], sc_ref[0])
    d = h_ref.shape[1]
    for gi in range(d // FGROUP):
        lanes = slice(gi * FGROUP, (gi + 1) * FGROUP)
        t = _dot(h_ref[:, lanes], cs_ref[...])
        o_ref[0, 0, :, lanes] = t[:, :FGROUP].astype(_BF16)
        o_ref[0, 1, :, lanes] = t[:, FGROUP:].astype(_BF16)


def _chdft(x, mods, k0, g, cs, tm=1024):
    nb, L, d = x.shape
    mod_spec = lambda k: pl.BlockSpec((1, 1, d), lambda b, i: (b, 0, k))
    return pl.pallas_call(
        _chdft_kernel,
        grid=(nb, L // tm),
        in_specs=[pl.BlockSpec((1, tm, d), lambda b, i: (b, i, 0)),
                  mod_spec(k0), mod_spec(k0 + 1),
                  pl.BlockSpec((1, d), lambda b, i: (0, 0)),
                  pl.BlockSpec((FGROUP, 2 * FGROUP), lambda b, i: (0, 0))],
        out_specs=pl.BlockSpec((1, 2, tm, d), lambda b, i: (b, 0, i, 0)),
        out_shape=jax.ShapeDtypeStruct((nb, 2, L, d), _BF16),
        scratch_shapes=[pltpu.VMEM((tm, d), _BF16)],
        compiler_params=_params(("arbitrary", "arbitrary"), 48),
        name="fourier_channel_dft",
    )(x, mods, mods, g.reshape(1, d), cs)


def _seqdft_kernel(ac_ref, as_ref, ctop_ref, r_ref, bc_ref, bs_ref, o_ref, fc_sc, fs_sc, h_sc,
                   *, scale, rc):
    half = as_ref.shape[0]
    n = 2 * half
    mb = MIRROR_BLOCK
    nblk = half // mb
    for qb in range(nblk):
        rows = slice(qb * mb, (qb + 1) * mb)
        if qb == 0:
            rm, src = r_ref[:, :mb], slice(n - mb, n)
        else:
            rm, src = r_ref[...], slice(n - (qb + 1) * mb, n - (qb - 1) * mb)
        fc_sc[rows, :] = (bc_ref[rows, :].astype(_F32) + _dot(rm, bc_ref[src, :])).astype(_BF16)
        fs_sc[rows, :] = (bs_ref[rows, :].astype(_F32) - _dot(rm, bs_ref[src, :])).astype(_BF16)
    fc_sc[half:half + mb, :] = bc_ref[half:half + mb, :]

    h_sc[half:half + mb, :] = (_dot(ctop_ref[...], fc_sc[...]) * scale).astype(_BF16)
    for r in range(half // rc):
        rows = slice(r * rc, (r + 1) * rc)
        p = _dot(ac_ref[rows, :], fc_sc[...])
        q = _dot(as_ref[rows, :], fs_sc[...])
        o_ref[0, 0, rows, :] = ((p - q) * scale).astype(_BF16)
        h_sc[rows, :] = ((p + q) * scale).astype(_BF16)
    for qb in range(nblk):
        src = h_sc[(nblk - 1 - qb) * mb:(nblk + 1 - qb) * mb, :]
        o_ref[0, 1, qb * mb:(qb + 1) * mb, :] = _dot(r_ref[...], src).astype(_BF16)


def _seqdft(xcs, scale, tn=1024, rc=512):
    nb, _, L, d = xcs.shape
    half = L // 2
    mb = MIRROR_BLOCK
    cos_s, sin_s = _dft_tables(L)
    kc = half + mb
    cos_f = np.zeros((half + mb, kc))
    cos_f[:, :half + 1] = cos_s[:half + mb, :half + 1]
    a_cos = jnp.asarray(cos_f[:half], _BF16)
    a_sin = jnp.asarray(sin_s[:half, :half], _BF16)
    c_top = jnp.asarray(cos_f[half:], _BF16)
    mirror = np.zeros((mb, 2 * mb), np.float32)
    mirror[np.arange(mb), mb - np.arange(mb)] = 1.0
    whole = lambda rows, cols: pl.BlockSpec((rows, cols), lambda b, n: (0, 0))
    b_spec = lambda part: pl.BlockSpec((None, None, L, tn), lambda b, n: (b, part, 0, n))
    return pl.pallas_call(
        functools.partial(_seqdft_kernel, scale=scale, rc=rc),
        grid=(nb, d // tn),
        in_specs=[whole(half, kc), whole(half, half), whole(mb, kc), whole(mb, 2 * mb),
                  b_spec(0), b_spec(1)],
        out_specs=pl.BlockSpec((1, 2, half, tn), lambda b, n: (b, 0, 0, n)),
        out_shape=jax.ShapeDtypeStruct((nb, 2, half, d), _BF16),
        scratch_shapes=[pltpu.VMEM((kc, tn), _BF16), pltpu.VMEM((half, tn), _BF16),
                        pltpu.VMEM((half + mb, tn), _BF16)],
        compiler_params=_params(("arbitrary", "arbitrary"), 48),
        name="fourier_seq_dft",
    )(a_cos, a_sin, c_top, jnp.asarray(mirror, _BF16), xcs, xcs)


def kernel(x, c, ctx, c_ctx, mod_w, mod_b, norm_g, ffn_w_in, ffn_w_out, rec_w_in, rec_conv_w,
           rec_conv_b, rec_gate_w, rec_gate_b, rec_lam, rec_w_out, fou_w_out):
    nb, s, d = x.shape
    lc = ctx.shape[1]
    depth = mod_w.shape[0]
    n_mixers = 2
    last_rec = ((depth - 1) // n_mixers) * n_mixers
    assert last_rec == 0, "context stream is only carried up to the first RG-LRU mixer"

    rows = 2 * V7X_SUBLANES
    cc = jnp.concatenate([c, c_ctx[None], jnp.zeros((rows - nb - 1, d), _F32)], axis=0)
    mods_all = _adaln(cc, mod_w, mod_b).reshape(depth, rows, 1, N_MOD * d)

    w_ffn = (ffn_w_in[0, 0].astype(_BF16), ffn_w_out[0, 0].astype(_BF16))
    order = [(i, sub) for i in range(depth) for sub in range(2)]

    def ffn_lat(x, mods, k0, g_pre, g_post, layer, sub):
        nonlocal w_ffn
        nxt = order.index((layer, sub)) + 1
        next_w = (ffn_w_in, ffn_w_out) + order[nxt] if nxt < len(order) else None
        out = _ffn(x, mods, lat_row, k0, g_pre, g_post, *w_ffn, next_w=next_w)
        if next_w is None:
            return out
        w_ffn = (out[1], out[2])
        return out[0]

    lat_row = lambda b: b
    ctx_row = lambda b: nb

    for i in range(depth):
        is_rec = (i % n_mixers) == 0
        j = i // n_mixers
        mods = mods_all[i]
        g = norm_g[i]
        w_pre = w_ffn
        x = ffn_lat(x, mods, 0, g[0], g[1], i, 0)
        if is_rec:
            ctx1 = _ffn(ctx.reshape(1, nb * lc, d), mods, ctx_row, 0, g[0], g[1], *w_pre)
            w_rec = rec_w_in[j].astype(_BF16)
            g_lat, v_lat = _recin(x, mods, lat_row, 3, g[2], w_rec, rec_conv_w[j], rec_conv_b[j],
                                  tm=1024, blocklen=GRID_W, need_gate=True)
            _, v_ctx = _recin(ctx1, mods, ctx_row, 3, g[2], w_rec, rec_conv_w[j], rec_conv_b[j],
                              tm=1024, blocklen=lc, need_gate=False)
            v_ctx = v_ctx.reshape(nb, lc, -1)
            gw = rec_gate_w[j]
            nh = gw.shape[2]
            w_cat = jnp.transpose(gw, (2, 3, 0, 1, 4)).reshape(nh, LRU_BLOCK, 4 * LRU_BLOCK)
            a = _scan(v_lat, v_ctx, g_lat, (0.5 * w_cat).astype(_BF16),
                      rec_gate_b[j].reshape(4, -1), rec_lam[j])
            w_mix = rec_w_out[j].astype(_BF16)
        else:
            cos_c, sin_c = _dft_tables(FGROUP)
            cs = jnp.asarray(np.concatenate([cos_c, sin_c], axis=1), _BF16)
            xcs = _chdft(x, mods, 3, g[2], cs)
            a = _seqdft(xcs, 1.0 / math.sqrt(s * FGROUP)).reshape(nb, s, d)
            w_mix = fou_w_out[j].astype(_BF16)
        x = _outproj(x, a, mods, 5, g[3], w_mix)
        x = ffn_lat(x, mods, 6, g[4], g[5], i, 1)
    return x
```

```python
import functools
import math

import jax
import jax.numpy as jnp
import numpy as np
from jax import lax
from jax.experimental import pallas as pl
from jax.experimental.pallas import tpu as pltpu

N_SUB = 3
N_MOD = 3 * N_SUB
MACARON = 0.5
GRID_W = 64
CONV_W = 4
CONV_LEFT = CONV_W // 2
LRU_BLOCK = 128
RG_C = 8.0
LOG2E = math.log2(math.e)
F32_TINY = float(np.finfo(np.float32).tiny)
FGROUP = 256
MIRROR_BLOCK = 128
EPS = 1e-6

V7X_SUBLANES = 8
ROW_CHUNK = 64
REC_ROW_CHUNK = 256
REC_COL_CHUNK = 256
V7X_LANES = 128
MIB = 1024 * 1024

_BF16 = jnp.bfloat16
_F32 = jnp.float32


def _params(semantics, vmem_mib):
    return pltpu.CompilerParams(dimension_semantics=semantics,
                                vmem_limit_bytes=vmem_mib * MIB)


def _dot(a, b):
    return jnp.dot(a, b, preferred_element_type=_F32)


def _for_row_chunks(n_rows, fn, rn=ROW_CHUNK, unroll=2):
    def body(r, carry):
        fn(pl.ds(pl.multiple_of(r * rn, rn), rn))
        return carry
    lax.fori_loop(0, n_rows // rn, body, 0, unroll=unroll)


def _inv_rms(x):
    return lax.rsqrt(jnp.mean(x * x, axis=-1, keepdims=True) + EPS)


def _norm_mod_into(x_ref, h_ref, g, shift, scale):
    gm = g * (1.0 + scale)

    def fn(rows):
        inv = _inv_rms(x_ref[rows, :])
        h_ref[rows, :] = (x_ref[rows, :] * inv * gm + shift).astype(h_ref.dtype)
    _for_row_chunks(x_ref.shape[0], fn)


def _adaln_kernel(c_ref, w_ref, b_ref, o_ref):
    c = c_ref[...]
    a = (c * jax.nn.sigmoid(c)).astype(_BF16)
    o_ref[0] = _dot(a, w_ref[0].astype(_BF16)) + b_ref[0]


def _adaln(cc, mod_w, mod_b, tn=2048):
    depth, d, n = mod_w.shape
    rows = cc.shape[0]
    return pl.pallas_call(
        _adaln_kernel,
        grid=(depth, n // tn),
        in_specs=[pl.BlockSpec((rows, d), lambda l, j: (0, 0)),
                  pl.BlockSpec((1, d, tn), lambda l, j: (l, 0, j)),
                  pl.BlockSpec((1, 1, tn), lambda l, j: (l, 0, j))],
        out_specs=pl.BlockSpec((1, rows, tn), lambda l, j: (l, 0, j)),
        out_shape=jax.ShapeDtypeStruct((depth, rows, n), _F32),
        compiler_params=_params(("arbitrary", "arbitrary"), 48),
        name="adaln",
    )(cc, mod_w, mod_b.reshape(depth, 1, n))


def _ffn_kernel(x_ref, sh_ref, sc_ref, gt_ref, gpre_ref, gpost_ref, wg_ref, wu_ref, wo_ref,
                *rest, rc, rc_mid, nf, cast_next):
    if cast_next:
        nwi_ref, nwo_ref, o_ref, cwi_ref, cwo_ref, h_ref = rest
        cwi_ref[...] = nwi_ref[...].astype(_BF16)
        cwo_ref[...] = nwo_ref[...].astype(_BF16)
    else:
        o_ref, h_ref = rest
    j = pl.program_id(2)
    nj = pl.num_programs(2)
    tm = h_ref.shape[0]
    x2 = x_ref.at[0]
    o2 = o_ref.at[0]

    def sweep(chunk):
        start, size = chunk
        return [slice(q, q + ROW_CHUNK) for q in range(start, start + size, ROW_CHUNK)]

    def norm_rows(chunk, gm, shift):
        for rows in sweep(chunk):
            inv = _inv_rms(x2[rows, :])
            h_ref[rows, :] = (x2[rows, :] * inv * gm + shift).astype(_BF16)

    def residual_rows(chunk, gg):
        for rows in sweep(chunk):
            inv = _inv_rms(o2[rows, :])
            o2[rows, :] = x2[rows, :] + (o2[rows, :] * inv) * gg

    def step(first, last):
        size = rc if (first or last) else rc_mid
        chunks = [(start, size) for start in range(0, tm, size)]
        if first:
            gm = gpre_ref[...] * (1.0 + sc_ref[0])
            shift = sh_ref[0]
            norm_rows(chunks[0], gm, shift)
        if last:
            gg = (MACARON * gt_ref[0]) * gpost_ref[...]
        for r, (start, size) in enumerate(chunks):
            rows = slice(start, start + size)
            h = h_ref[rows, :]
            g = _dot(h, wg_ref[...])
            u = _dot(h, wu_ref[...])
            hg = 0.5 * g
            a = ((hg + hg * jnp.tanh(hg)) * u).astype(_BF16)
            y = _dot(a, wo_ref[...])
            if first:
                o2[rows, :] = y
                if r + 1 < len(chunks):
                    norm_rows(chunks[r + 1], gm, shift)
            else:
                o2[rows, :] += y
            if last and r > 0:
                residual_rows(chunks[r - 1], gg)
        if last:
            residual_rows(chunks[-1], gg)

    if nf == 1:
        step(True, True)
    else:
        pl.when(j == 0)(lambda: step(True, False))
        pl.when(j == nj - 1)(lambda: step(False, True))
        if nf > 2:
            pl.when((j > 0) & (j < nj - 1))(lambda: step(False, False))


def _ffn(x, mods, row_of, k0, g_pre, g_post, w_in, w_out, next_w=None, tm=1024, tf=512, rc=512,
         rc_mid=1024):
    nb, L, d = x.shape
    f = w_out.shape[0]
    nf = f // tf
    nt = nb * (L // tm)
    mod_spec = lambda k: pl.BlockSpec((1, 1, d), lambda b, i, j: (row_of(b), 0, k))
    vec_spec = pl.BlockSpec((1, d), lambda b, i, j: (0, 0))
    in_specs = [pl.BlockSpec((1, tm, d), lambda b, i, j: (b, i, 0)),
                mod_spec(k0), mod_spec(k0 + 1), mod_spec(k0 + 2),
                vec_spec, vec_spec,
                pl.BlockSpec((d, tf), lambda b, i, j: (0, j)),
                pl.BlockSpec((d, tf), lambda b, i, j: (0, nf + j)),
                pl.BlockSpec((tf, d), lambda b, i, j: (j, 0))]
    args = [x, mods, mods, mods, g_pre.reshape(1, d), g_post.reshape(1, d), w_in, w_in, w_out]
    out_specs = [pl.BlockSpec((1, tm, d), lambda b, i, j: (b, i, 0))]
    out_shape = [jax.ShapeDtypeStruct((nb, L, d), _F32)]
    if next_w is not None:
        nw_in, nw_out, layer, sub = next_w
        assert d % nt == 0 and (2 * f) % nf == 0 and f % (nt * nf) == 0
        ri, ci, ro = d // nt, 2 * f // nf, f // (nt * nf)
        tile = lambda i: i * (L // tm)
        in_specs += [pl.BlockSpec((None, None, ri, ci),
                                  lambda b, i, j: (layer, sub, tile(b) + i, j)),
                     pl.BlockSpec((None, None, ro, d),
                                  lambda b, i, j: (layer, sub, (tile(b) + i) * nf + j, 0))]
        args += [nw_in, nw_out]
        out_specs += [pl.BlockSpec((ri, ci), lambda b, i, j: (tile(b) + i, j)),
                      pl.BlockSpec((ro, d), lambda b, i, j: ((tile(b) + i) * nf + j, 0))]
        out_shape += [jax.ShapeDtypeStruct((d, 2 * f), _BF16), jax.ShapeDtypeStruct((f, d), _BF16)]
    out = pl.pallas_call(
        functools.partial(_ffn_kernel, rc=rc, rc_mid=rc_mid, nf=nf,
                          cast_next=next_w is not None),
        grid=(nb, L // tm, nf),
        in_specs=in_specs, out_specs=out_specs, out_shape=out_shape,
        scratch_shapes=[pltpu.VMEM((tm, d), _BF16)],
        compiler_params=_params(("arbitrary", "arbitrary", "arbitrary"), 60),
        name="ffn",
    )(*args)
    return out if next_w is not None else out[0]


def _conv_rows(v, cw, cb, blocklen):
    tm = v.shape[0]
    pos = lax.broadcasted_iota(jnp.int32, (tm, 1), 0) % blocklen
    out = cb + cw[CONV_LEFT:CONV_LEFT + 1] * v
    for k in range(CONV_W):
        off = k - CONV_LEFT
        if off == 0:
            continue
        shifted = pltpu.roll(v, (-off) % tm, axis=0)
        valid = (pos + off >= 0) & (pos + off < blocklen)
        out = out + cw[k:k + 1] * jnp.where(valid, shifted, 0.0)
    return out


def _recin_kernel(x_ref, sh_ref, sc_ref, g_ref, wg_ref, wv_ref, cw_ref, cb_ref,
                  og_ref, ov_ref, h_ref, *, blocklen):
    n = pl.program_id(2)

    tm = h_ref.shape[0]
    rc = min(tm, REC_ROW_CHUNK)
    assert rc % blocklen == 0 and tm % rc == 0

    x2 = x_ref.at[0]
    nr = tm // rc

    def norm_rows(r, gm, shift):
        for q in range(rc // ROW_CHUNK):
            rows = slice(r * rc + q * ROW_CHUNK, r * rc + (q + 1) * ROW_CHUNK)
            inv = _inv_rms(x2[rows, :])
            h_ref[rows, :] = (x2[rows, :] * inv * gm + shift).astype(_BF16)

    def step(first):
        if not first:
            h = h_ref[...]
            for c0 in range(0, ov_ref.shape[2], REC_COL_CHUNK):
                cols = slice(c0, c0 + REC_COL_CHUNK)
                if og_ref is not None:
                    og_ref[0, :, cols] = _gelu_tanh(_dot(h, wg_ref[:, cols])).astype(_BF16)
                v = _dot(h, wv_ref[:, cols])
                ov_ref[0, :, cols] = _conv_rows(v, cw_ref[:, cols], cb_ref[:, cols],
                                                blocklen).astype(_BF16)
            return
        if first:
            gm = g_ref[...] * (1.0 + sc_ref[0])
            shift = sh_ref[0]
            norm_rows(0, gm, shift)
        for r in range(nr):
            rows = slice(r * rc, (r + 1) * rc)
            h = h_ref[rows, :]
            if og_ref is not None:
                og_ref[0, rows, :] = _gelu_tanh(_dot(h, wg_ref[...])).astype(_BF16)
            v = _dot(h, wv_ref[...])
            ov_ref[0, rows, :] = _conv_rows(v, cw_ref[...], cb_ref[...], blocklen).astype(_BF16)
            if first and r + 1 < nr:
                norm_rows(r + 1, gm, shift)

    pl.when(n == 0)(lambda: step(True))
    pl.when(n > 0)(lambda: step(False))


def _recin_nogate_kernel(x_ref, sh_ref, sc_ref, g_ref, wv_ref, cw_ref, cb_ref, ov_ref, h_ref,
                         *, blocklen):
    _recin_kernel(x_ref, sh_ref, sc_ref, g_ref, None, wv_ref, cw_ref, cb_ref, None, ov_ref,
                  h_ref, blocklen=blocklen)


def _recin(x, mods, row_of, k0, g, w_in, conv_w, conv_b, tm, blocklen, need_gate, tn=512):
    nb, L, d = x.shape
    r = w_in.shape[1] // 2
    nn = r // tn
    mod_spec = lambda k: pl.BlockSpec((1, 1, d), lambda b, i, n: (row_of(b), 0, k))
    x_spec = pl.BlockSpec((1, tm, d), lambda b, i, n: (b, i, 0))
    vec_spec = pl.BlockSpec((1, d), lambda b, i, n: (0, 0))
    wg_spec = pl.BlockSpec((d, tn), lambda b, i, n: (0, n))
    wv_spec = pl.BlockSpec((d, tn), lambda b, i, n: (0, nn + n))
    cw_spec = pl.BlockSpec((CONV_W, tn), lambda b, i, n: (0, n))
    cb_spec = pl.BlockSpec((1, tn), lambda b, i, n: (0, n))
    o_spec = pl.BlockSpec((1, tm, tn), lambda b, i, n: (b, i, n))
    o_shape = jax.ShapeDtypeStruct((nb, L, r), _BF16)
    common = dict(grid=(nb, L // tm, nn),
                  scratch_shapes=[pltpu.VMEM((tm, d), _BF16)],
                  compiler_params=_params(("arbitrary", "arbitrary", "arbitrary"), 56))
    g2 = g.reshape(1, d)
    cb2 = conv_b.reshape(1, r)
    if need_gate:
        return pl.pallas_call(
            functools.partial(_recin_kernel, blocklen=blocklen),
            in_specs=[x_spec, mod_spec(k0), mod_spec(k0 + 1), vec_spec, wg_spec, wv_spec,
                      cw_spec, cb_spec],
            out_specs=[o_spec, o_spec], out_shape=[o_shape, o_shape],
            name="rec_in", **common,
        )(x, mods, mods, g2, w_in, w_in, conv_w, cb2)
    return None, pl.pallas_call(
        functools.partial(_recin_nogate_kernel, blocklen=blocklen),
        in_specs=[x_spec, mod_spec(k0), mod_spec(k0 + 1), vec_spec, wv_spec, cw_spec, cb_spec],
        out_specs=o_spec, out_shape=o_shape,
        name="rec_in_ctx", **common,
    )(x, mods, mods, g2, w_in, conv_w, cb2)


def _gelu_tanh(x):
    c = math.sqrt(2.0 / math.pi)
    hx = 0.5 * x
    return hx + hx * jnp.tanh(x * (c + (0.044715 * c) * (x * x)))


def _scan_kernel(v_ref, vc_ref, g_ref, w_ref, bias_ref, lam_ref, o_ref,
                 a_sc, b_sc, hl_sc, al_sc, *, nh):
    lc = vc_ref.shape[1]
    s = v_ref.shape[1]
    nseg = V7X_SUBLANES
    seg, cseg = s // nseg, lc // nseg
    cb = s

    neg_lam = -lam_ref[...]
    sp = jnp.maximum(neg_lam, 0.0) + jnp.log(1.0 + jnp.exp(-jnp.abs(neg_lam)))
    q = (-0.5 * RG_C * LOG2E) * sp
    hb = 0.5 * bias_ref[...]

    def coeffs(vb, pre, hh, store):
        lanes = slice(hh * LRU_BLOCK, (hh + 1) * LRU_BLOCK)
        vh = 0.5 * vb.astype(_F32)
        for d in range(2):
            c0 = (2 * d) * LRU_BLOCK
            ta = jnp.tanh(pre[:, c0:c0 + LRU_BLOCK] + hb[2 * d:2 * d + 1, lanes])
            tx = jnp.tanh(pre[:, c0 + LRU_BLOCK:c0 + 2 * LRU_BLOCK]
                          + hb[2 * d + 1:2 * d + 2, lanes])
            qd = q[d:d + 1, lanes]
            a = jnp.exp2(qd + qd * ta)
            m2 = 1.0 - a * a
            mult = m2 * lax.rsqrt(jnp.maximum(m2, F32_TINY))
            store(d, a, (mult * vh) * (1.0 + tx))

    def lat_coeffs(m, carry):
        src = pl.ds(pl.multiple_of(m * seg, seg), seg)
        dst = pl.ds(m, seg, stride=nseg)
        for hh in range(nh):
            vb = v_ref[0, src, hh * LRU_BLOCK:(hh + 1) * LRU_BLOCK]

            def store(d, a, b, hh=hh):
                a_sc[d, hh, dst, :] = a
                b_sc[d, hh, dst, :] = b
            coeffs(vb, _dot(vb, w_ref[hh]), hh, store)
        return carry

    lax.fori_loop(0, nseg, lat_coeffs, 0)

    for hh in range(nh):
        vb = vc_ref[0, :, hh * LRU_BLOCK:(hh + 1) * LRU_BLOCK]

        def store(d, a, b, hh=hh):
            for m in range(nseg):
                dst = pl.ds(cb + m, cseg, stride=nseg)
                a_sc[d, hh, dst, :] = a[m * cseg:(m + 1) * cseg]
                b_sc[d, hh, dst, :] = b[m * cseg:(m + 1) * cseg]
        coeffs(vb, _dot(vb, w_ref[hh]), hh, store)

    vshape = (nseg, V7X_LANES)
    row = lax.broadcasted_iota(jnp.int32, vshape, 0)
    zeros = tuple(jnp.zeros(vshape, _F32) for _ in range(nh))
    ones = tuple(jnp.ones(vshape, _F32) for _ in range(nh))

    def sublane_scan(a, b, reverse):
        for k in (1, 2, 4):
            keep = (row < nseg - k) if reverse else (row >= k)
            shift = (nseg - k) if reverse else k
            a_sh = jnp.where(keep, pltpu.roll(a, shift, axis=0), 1.0)
            b_sh = jnp.where(keep, pltpu.roll(b, shift, axis=0), 0.0)
            b = a * b_sh + b
            a = a * a_sh
        return a, b

    def local_pass(base, length, save):
        def body(i, carry):
            hs, prods = carry
            new_h, new_p = [], []
            for d, k in ((0, i), (1, length - 1 - i)):
                t = pl.ds(pl.multiple_of(k * nseg, nseg), nseg)
                idx = pl.ds(pl.multiple_of(base + k * nseg, nseg), nseg)
                for hh in range(nh):
                    a = a_sc[d, hh, idx, :]
                    h = a * hs[d * nh + hh] + b_sc[d, hh, idx, :]
                    pr = prods[d * nh + hh] * a
                    if save:
                        hl_sc[d, hh, t, :] = h
                        al_sc[d, hh, t, :] = pr
                    new_h.append(h)
                    new_p.append(pr)
            return tuple(new_h), tuple(new_p)
        return lax.fori_loop(0, length, body, (zeros + zeros, ones + ones), unroll=8)

    def entry_states(h_fin, a_tot, g_in, reverse):
        acum, bcum = sublane_scan(a_tot, h_fin, reverse)
        e = acum * g_in + bcum
        if reverse:
            entry = jnp.where(row < nseg - 1, pltpu.roll(e, nseg - 1, axis=0), g_in)
            final = e[0:1, :]
        else:
            entry = jnp.where(row >= 1, pltpu.roll(e, 1, axis=0), g_in)
            final = e[nseg - 1:nseg, :]
        return entry, jnp.broadcast_to(final, vshape)

    hc, pc_tot = local_pass(cb, cseg, save=False)
    hl, pl_tot = local_pass(0, seg, save=True)
    entries = []
    for d, reverse in ((0, False), (1, True)):
        for hh in range(nh):
            n = d * nh + hh
            _, g_ctx = entry_states(hc[n], pc_tot[n], zeros[hh], reverse)
            entry, _ = entry_states(hl[n], pl_tot[n], g_ctx, reverse)
            entries.append(entry)

    def fix(k, carry):
        t = pl.ds(pl.multiple_of(k * nseg, nseg), nseg)
        for hh in range(nh):
            hl_sc[0, hh, t, :] = ((hl_sc[0, hh, t, :] + al_sc[0, hh, t, :] * entries[hh])
                                  + (hl_sc[1, hh, t, :] + al_sc[1, hh, t, :] * entries[nh + hh]))
        return carry
    lax.fori_loop(0, seg, fix, 0, unroll=8)

    def gate_out(m, carry):
        rows = pl.ds(pl.multiple_of(m * seg, seg), seg)
        for hh in range(nh):
            lanes = slice(hh * LRU_BLOCK, (hh + 1) * LRU_BLOCK)
            y = hl_sc[0, hh, pl.ds(m, seg, stride=nseg), :]
            o_ref[0, rows, lanes] = (g_ref[0, rows, lanes].astype(_F32) * y).astype(_BF16)
        return carry

    lax.fori_loop(0, nseg, gate_out, 0)


def _scan(v_lat, v_ctx, g_lat, w_cat, bias, lam, cc=512):
    nb, s, r = v_lat.shape
    lc = v_ctx.shape[1]
    nh = cc // LRU_BLOCK
    coef_rows = s + lc
    blk = lambda L: pl.BlockSpec((1, L, cc), lambda b, c: (b, 0, c))
    return pl.pallas_call(
        functools.partial(_scan_kernel, nh=nh),
        grid=(nb, r // cc),
        in_specs=[blk(s), blk(lc), blk(s),
                  pl.BlockSpec((nh, LRU_BLOCK, 4 * LRU_BLOCK), lambda b, c: (c, 0, 0)),
                  pl.BlockSpec((4, cc), lambda b, c: (0, c)),
                  pl.BlockSpec((2, cc), lambda b, c: (0, c))],
        out_specs=blk(s),
        out_shape=jax.ShapeDtypeStruct((nb, s, r), _BF16),
        scratch_shapes=[pltpu.VMEM((2, nh, coef_rows, V7X_LANES), _F32),
                        pltpu.VMEM((2, nh, coef_rows, V7X_LANES), _F32),
                        pltpu.VMEM((2, nh, s, V7X_LANES), _F32),
                        pltpu.VMEM((2, nh, s, V7X_LANES), _F32)],
        compiler_params=_params(("arbitrary", "arbitrary"), 56),
        name="rglru_scan",
    )(v_lat, v_ctx, g_lat, w_cat, bias, lam)


def _outproj_kernel(x_ref, a_ref, gt_ref, g_ref, w_ref, o_ref, *, rc):
    tm = o_ref.shape[1]
    x2 = x_ref.at[0]
    o2 = o_ref.at[0]
    gg = gt_ref[0] * g_ref[...]

    def residual(r):
        for q in range(r * rc, (r + 1) * rc, ROW_CHUNK):
            rows = slice(q, q + ROW_CHUNK)
            inv = _inv_rms(o2[rows, :])
            o2[rows, :] = x2[rows, :] + (o2[rows, :] * inv) * gg

    for r in range(tm // rc):
        rows = slice(r * rc, (r + 1) * rc)
        o2[rows, :] = _dot(a_ref[0, rows, :], w_ref[...])
        if r > 0:
            residual(r - 1)
    residual(tm // rc - 1)


def _outproj(x, a, mods, k_gate, g, w, tm=1024, rc=256):
    nb, L, d = x.shape
    r = w.shape[0]
    return pl.pallas_call(
        functools.partial(_outproj_kernel, rc=rc),
        grid=(nb, L // tm),
        in_specs=[pl.BlockSpec((1, tm, d), lambda b, i: (b, i, 0)),
                  pl.BlockSpec((1, tm, r), lambda b, i: (b, i, 0)),
                  pl.BlockSpec((1, 1, d), lambda b, i: (b, 0, k_gate)),
                  pl.BlockSpec((1, d), lambda b, i: (0, 0)),
                  pl.BlockSpec((r, d), lambda b, i: (0, 0), pipeline_mode=pl.Buffered(1))],
        out_specs=pl.BlockSpec((1, tm, d), lambda b, i: (b, i, 0)),
        out_shape=jax.ShapeDtypeStruct((nb, L, d), _F32),
        compiler_params=_params(("arbitrary", "arbitrary"), 56),
        name="mixer_out",
    )(x, a, mods, g.reshape(1, d), w)


def _dft_tables(n):
    k = np.arange(n, dtype=np.int64)
    ang = 2.0 * np.pi * ((k[:, None] * k[None, :]) % n).astype(np.float64) / n
    return np.cos(ang), np.sin(ang)


def _chdft_kernel(x_ref, sh_ref, sc_ref, g_ref, cs_ref, o_ref, h_ref):
    _norm_mod_into(x_ref.at[0], h_ref, g_ref[...], sh_ref[0], sc_ref[0])
    d = h_ref.shape[1]
    for gi in range(d // FGROUP):
        lanes = slice(gi * FGROUP, (gi + 1) * FGROUP)
        t = _dot(h_ref[:, lanes], cs_ref[...])
        o_ref[0, 0, :, lanes] = t[:, :FGROUP].astype(_BF16)
        o_ref[0, 1, :, lanes] = t[:, FGROUP:].astype(_BF16)


def _chdft(x, mods, k0, g, cs, tm=1024):
    nb, L, d = x.shape
    mod_spec = lambda k: pl.BlockSpec((1, 1, d), lambda b, i: (b, 0, k))
    return pl.pallas_call(
        _chdft_kernel,
        grid=(nb, L // tm),
        in_specs=[pl.BlockSpec((1, tm, d), lambda b, i: (b, i, 0)),
                  mod_spec(k0), mod_spec(k0 + 1),
                  pl.BlockSpec((1, d), lambda b, i: (0, 0)),
                  pl.BlockSpec((FGROUP, 2 * FGROUP), lambda b, i: (0, 0))],
        out_specs=pl.BlockSpec((1, 2, tm, d), lambda b, i: (b, 0, i, 0)),
        out_shape=jax.ShapeDtypeStruct((nb, 2, L, d), _BF16),
        scratch_shapes=[pltpu.VMEM((tm, d), _BF16)],
        compiler_params=_params(("arbitrary", "arbitrary"), 48),
        name="fourier_channel_dft",
    )(x, mods, mods, g.reshape(1, d), cs)


def _seqdft_kernel(ac_ref, as_ref, ctop_ref, r_ref, bc_ref, bs_ref, o_ref, fc_sc, fs_sc, h_sc,
                   *, scale, rc):
    half = as_ref.shape[0]
    n = 2 * half
    mb = MIRROR_BLOCK
    nblk = half // mb
    for qb in range(nblk):
        rows = slice(qb * mb, (qb + 1) * mb)
        if qb == 0:
            rm, src = r_ref[:, :mb], slice(n - mb, n)
        else:
            rm, src = r_ref[...], slice(n - (qb + 1) * mb, n - (qb - 1) * mb)
        fc_sc[rows, :] = (bc_ref[rows, :].astype(_F32) + _dot(rm, bc_ref[src, :])).astype(_BF16)
        fs_sc[rows, :] = (bs_ref[rows, :].astype(_F32) - _dot(rm, bs_ref[src, :])).astype(_BF16)
    fc_sc[half:half + mb, :] = bc_ref[half:half + mb, :]

    h_sc[half:half + mb, :] = (_dot(ctop_ref[...], fc_sc[...]) * scale).astype(_BF16)
    for r in range(half // rc):
        rows = slice(r * rc, (r + 1) * rc)
        p = _dot(ac_ref[rows, :], fc_sc[...])
        q = _dot(as_ref[rows, :], fs_sc[...])
        o_ref[0, 0, rows, :] = ((p - q) * scale).astype(_BF16)
        h_sc[rows, :] = ((p + q) * scale).astype(_BF16)
    for qb in range(nblk):
        src = h_sc[(nblk - 1 - qb) * mb:(nblk + 1 - qb) * mb, :]
        o_ref[0, 1, qb * mb:(qb + 1) * mb, :] = _dot(r_ref[...], src).astype(_BF16)


def _seqdft(xcs, scale, tn=1024, rc=1024):
    nb, _, L, d = xcs.shape
    half = L // 2
    mb = MIRROR_BLOCK
    cos_s, sin_s = _dft_tables(L)
    kc = half + mb
    cos_f = np.zeros((half + mb, kc))
    cos_f[:, :half + 1] = cos_s[:half + mb, :half + 1]
    a_cos = jnp.asarray(cos_f[:half], _BF16)
    a_sin = jnp.asarray(sin_s[:half, :half], _BF16)
    c_top = jnp.asarray(cos_f[half:], _BF16)
    mirror = np.zeros((mb, 2 * mb), np.float32)
    mirror[np.arange(mb), mb - np.arange(mb)] = 1.0
    whole = lambda rows, cols: pl.BlockSpec((rows, cols), lambda b, n: (0, 0))
    b_spec = lambda part: pl.BlockSpec((None, None, L, tn), lambda b, n: (b, part, 0, n))
    return pl.pallas_call(
        functools.partial(_seqdft_kernel, scale=scale, rc=rc),
        grid=(nb, d // tn),
        in_specs=[whole(half, kc), whole(half, half), whole(mb, kc), whole(mb, 2 * mb),
                  b_spec(0), b_spec(1)],
        out_specs=pl.BlockSpec((1, 2, half, tn), lambda b, n: (b, 0, 0, n)),
        out_shape=jax.ShapeDtypeStruct((nb, 2, half, d), _BF16),
        scratch_shapes=[pltpu.VMEM((kc, tn), _BF16), pltpu.VMEM((half, tn), _BF16),
                        pltpu.VMEM((half + mb, tn), _BF16)],
        compiler_params=_params(("arbitrary", "arbitrary"), 48),
        name="fourier_seq_dft",
    )(a_cos, a_sin, c_top, jnp.asarray(mirror, _BF16), xcs, xcs)


def kernel(x, c, ctx, c_ctx, mod_w, mod_b, norm_g, ffn_w_in, ffn_w_out, rec_w_in, rec_conv_w,
           rec_conv_b, rec_gate_w, rec_gate_b, rec_lam, rec_w_out, fou_w_out):
    nb, s, d = x.shape
    lc = ctx.shape[1]
    depth = mod_w.shape[0]
    n_mixers = 2
    last_rec = ((depth - 1) // n_mixers) * n_mixers
    assert last_rec == 0, "context stream is only carried up to the first RG-LRU mixer"

    rows = 2 * V7X_SUBLANES
    cc = jnp.concatenate([c, c_ctx[None], jnp.zeros((rows - nb - 1, d), _F32)], axis=0)
    mods_all = _adaln(cc, mod_w, mod_b).reshape(depth, rows, 1, N_MOD * d)

    w_ffn = (ffn_w_in[0, 0].astype(_BF16), ffn_w_out[0, 0].astype(_BF16))
    order = [(i, sub) for i in range(depth) for sub in range(2)]

    def ffn_lat(x, mods, k0, g_pre, g_post, layer, sub):
        nonlocal w_ffn
        nxt = order.index((layer, sub)) + 1
        next_w = (ffn_w_in, ffn_w_out) + order[nxt] if nxt < len(order) else None
        out = _ffn(x, mods, lat_row, k0, g_pre, g_post, *w_ffn, next_w=next_w)
        if next_w is None:
            return out
        w_ffn = (out[1], out[2])
        return out[0]

    lat_row = lambda b: b
    ctx_row = lambda b: nb

    for i in range(depth):
        is_rec = (i % n_mixers) == 0
        j = i // n_mixers
        mods = mods_all[i]
        g = norm_g[i]
        w_pre = w_ffn
        x = ffn_lat(x, mods, 0, g[0], g[1], i, 0)
        if is_rec:
            ctx1 = _ffn(ctx.reshape(1, nb * lc, d), mods, ctx_row, 0, g[0], g[1], *w_pre)
            w_rec = rec_w_in[j].astype(_BF16)
            g_lat, v_lat = _recin(x, mods, lat_row, 3, g[2], w_rec, rec_conv_w[j], rec_conv_b[j],
                                  tm=1024, blocklen=GRID_W, need_gate=True)
            _, v_ctx = _recin(ctx1, mods, ctx_row, 3, g[2], w_rec, rec_conv_w[j], rec_conv_b[j],
                              tm=1024, blocklen=lc, need_gate=False)
            v_ctx = v_ctx.reshape(nb, lc, -1)
            gw = rec_gate_w[j]
            nh = gw.shape[2]
            w_cat = jnp.transpose(gw, (2, 3, 0, 1, 4)).reshape(nh, LRU_BLOCK, 4 * LRU_BLOCK)
            a = _scan(v_lat, v_ctx, g_lat, (0.5 * w_cat).astype(_BF16),
                      rec_gate_b[j].reshape(4, -1), rec_lam[j])
            w_mix = rec_w_out[j].astype(_BF16)
        else:
            cos_c, sin_c = _dft_tables(FGROUP)
            cs = jnp.asarray(np.concatenate([cos_c, sin_c], axis=1), _BF16)
            xcs = _chdft(x, mods, 3, g[2], cs)
            a = _seqdft(xcs, 1.0 / math.sqrt(s * FGROUP)).reshape(nb, s, d)
            w_mix = fou_w_out[j].astype(_BF16)
        x = _outproj(x, a, mods, 5, g[3], w_mix)
        x = ffn_lat(x, mods, 6, g[4], g[5], i, 1)
    return x
```

```python
import functools
import math

import jax
import jax.numpy as jnp
import numpy as np
from jax import lax
from jax.experimental import pallas as pl
from jax.experimental.pallas import tpu as pltpu

N_SUB = 3
N_MOD = 3 * N_SUB
MACARON = 0.5
GRID_W = 64
CONV_W = 4
CONV_LEFT = CONV_W // 2
LRU_BLOCK = 128
RG_C = 8.0
LOG2E = math.log2(math.e)
F32_TINY = float(np.finfo(np.float32).tiny)
FGROUP = 256
MIRROR_BLOCK = 128
EPS = 1e-6

V7X_SUBLANES = 8
ROW_CHUNK = 64
REC_ROW_CHUNK = 256
REC_COL_CHUNK = 256
V7X_LANES = 128
MIB = 1024 * 1024

_BF16 = jnp.bfloat16
_F32 = jnp.float32


def _params(semantics, vmem_mib):
    return pltpu.CompilerParams(dimension_semantics=semantics,
                                vmem_limit_bytes=vmem_mib * MIB)


def _dot(a, b):
    return jnp.dot(a, b, preferred_element_type=_F32)


def _for_row_chunks(n_rows, fn, rn=ROW_CHUNK, unroll=2):
    def body(r, carry):
        fn(pl.ds(pl.multiple_of(r * rn, rn), rn))
        return carry
    lax.fori_loop(0, n_rows // rn, body, 0, unroll=unroll)


def _inv_rms(x):
    return lax.rsqrt(jnp.mean(x * x, axis=-1, keepdims=True) + EPS)


def _norm_mod_into(x_ref, h_ref, g, shift, scale):
    gm = g * (1.0 + scale)

    def fn(rows):
        inv = _inv_rms(x_ref[rows, :])
        h_ref[rows, :] = (x_ref[rows, :] * inv * gm + shift).astype(h_ref.dtype)
    _for_row_chunks(x_ref.shape[0], fn)


def _adaln_kernel(c_ref, w_ref, b_ref, o_ref):
    c = c_ref[...]
    a = (c * jax.nn.sigmoid(c)).astype(_BF16)
    o_ref[0] = _dot(a, w_ref[0].astype(_BF16)) + b_ref[0]


def _adaln(cc, mod_w, mod_b, tn=2048):
    depth, d, n = mod_w.shape
    rows = cc.shape[0]
    return pl.pallas_call(
        _adaln_kernel,
        grid=(depth, n // tn),
        in_specs=[pl.BlockSpec((rows, d), lambda l, j: (0, 0)),
                  pl.BlockSpec((1, d, tn), lambda l, j: (l, 0, j)),
                  pl.BlockSpec((1, 1, tn), lambda l, j: (l, 0, j))],
        out_specs=pl.BlockSpec((1, rows, tn), lambda l, j: (l, 0, j)),
        out_shape=jax.ShapeDtypeStruct((depth, rows, n), _F32),
        compiler_params=_params(("arbitrary", "arbitrary"), 48),
        name="adaln",
    )(cc, mod_w, mod_b.reshape(depth, 1, n))


def _ffn_kernel(x_ref, sh_ref, sc_ref, gt_ref, gpre_ref, gpost_ref, wg_ref, wu_ref, wo_ref,
                *rest, rc, rc_mid, nf, cast_next):
    if cast_next:
        nwi_ref, nwo_ref, o_ref, cwi_ref, cwo_ref, h_ref = rest
        cwi_ref[...] = nwi_ref[...].astype(_BF16)
        cwo_ref[...] = nwo_ref[...].astype(_BF16)
    else:
        o_ref, h_ref = rest
    j = pl.program_id(2)
    nj = pl.num_programs(2)
    tm = h_ref.shape[0]
    x2 = x_ref.at[0]
    o2 = o_ref.at[0]

    def sweep(chunk):
        start, size = chunk
        return [slice(q, q + ROW_CHUNK) for q in range(start, start + size, ROW_CHUNK)]

    def norm_rows(chunk, gm, shift):
        for rows in sweep(chunk):
            inv = _inv_rms(x2[rows, :])
            h_ref[rows, :] = (x2[rows, :] * inv * gm + shift).astype(_BF16)

    def residual_rows(chunk, gg):
        for rows in sweep(chunk):
            inv = _inv_rms(o2[rows, :])
            o2[rows, :] = x2[rows, :] + (o2[rows, :] * inv) * gg

    def step(first, last):
        size = rc if (first or last) else rc_mid
        chunks = [(start, size) for start in range(0, tm, size)]
        if first:
            gm = gpre_ref[...] * (1.0 + sc_ref[0])
            shift = sh_ref[0]
            norm_rows(chunks[0], gm, shift)
        if last:
            gg = (MACARON * gt_ref[0]) * gpost_ref[...]
        for r, (start, size) in enumerate(chunks):
            rows = slice(start, start + size)
            h = h_ref[rows, :]
            g = _dot(h, wg_ref[...])
            u = _dot(h, wu_ref[...])
            hg = 0.5 * g
            a = ((hg + hg * jnp.tanh(hg)) * u).astype(_BF16)
            y = _dot(a, wo_ref[...])
            if first:
                o2[rows, :] = y
                if r + 1 < len(chunks):
                    norm_rows(chunks[r + 1], gm, shift)
            else:
                o2[rows, :] += y
            if last and r > 0:
                residual_rows(chunks[r - 1], gg)
        if last:
            residual_rows(chunks[-1], gg)

    if nf == 1:
        step(True, True)
    else:
        pl.when(j == 0)(lambda: step(True, False))
        pl.when(j == nj - 1)(lambda: step(False, True))
        if nf > 2:
            pl.when((j > 0) & (j < nj - 1))(lambda: step(False, False))


def _ffn(x, mods, row_of, k0, g_pre, g_post, w_in, w_out, next_w=None, tm=1024, tf=512, rc=512,
         rc_mid=1024):
    nb, L, d = x.shape
    f = w_out.shape[0]
    nf = f // tf
    nt = nb * (L // tm)
    mod_spec = lambda k: pl.BlockSpec((1, 1, d), lambda b, i, j: (row_of(b), 0, k))
    vec_spec = pl.BlockSpec((1, d), lambda b, i, j: (0, 0))
    in_specs = [pl.BlockSpec((1, tm, d), lambda b, i, j: (b, i, 0)),
                mod_spec(k0), mod_spec(k0 + 1), mod_spec(k0 + 2),
                vec_spec, vec_spec,
                pl.BlockSpec((d, tf), lambda b, i, j: (0, j)),
                pl.BlockSpec((d, tf), lambda b, i, j: (0, nf + j)),
                pl.BlockSpec((tf, d), lambda b, i, j: (j, 0))]
    args = [x, mods, mods, mods, g_pre.reshape(1, d), g_post.reshape(1, d), w_in, w_in, w_out]
    out_specs = [pl.BlockSpec((1, tm, d), lambda b, i, j: (b, i, 0))]
    out_shape = [jax.ShapeDtypeStruct((nb, L, d), _F32)]
    if next_w is not None:
        nw_in, nw_out, layer, sub = next_w
        assert d % nt == 0 and (2 * f) % nf == 0 and f % (nt * nf) == 0
        ri, ci, ro = d // nt, 2 * f // nf, f // (nt * nf)
        tile = lambda i: i * (L // tm)
        in_specs += [pl.BlockSpec((None, None, ri, ci),
                                  lambda b, i, j: (layer, sub, tile(b) + i, j)),
                     pl.BlockSpec((None, None, ro, d),
                                  lambda b, i, j: (layer, sub, (tile(b) + i) * nf + j, 0))]
        args += [nw_in, nw_out]
        out_specs += [pl.BlockSpec((ri, ci), lambda b, i, j: (tile(b) + i, j)),
                      pl.BlockSpec((ro, d), lambda b, i, j: ((tile(b) + i) * nf + j, 0))]
        out_shape += [jax.ShapeDtypeStruct((d, 2 * f), _BF16), jax.ShapeDtypeStruct((f, d), _BF16)]
    out = pl.pallas_call(
        functools.partial(_ffn_kernel, rc=rc, rc_mid=rc_mid, nf=nf,
                          cast_next=next_w is not None),
        grid=(nb, L // tm, nf),
        in_specs=in_specs, out_specs=out_specs, out_shape=out_shape,
        scratch_shapes=[pltpu.VMEM((tm, d), _BF16)],
        compiler_params=_params(("arbitrary", "arbitrary", "arbitrary"), 60),
        name="ffn",
    )(*args)
    return out if next_w is not None else out[0]


def _conv_rows(v, cw, cb, blocklen):
    tm = v.shape[0]
    pos = lax.broadcasted_iota(jnp.int32, (tm, 1), 0) % blocklen
    out = cb + cw[CONV_LEFT:CONV_LEFT + 1] * v
    for k in range(CONV_W):
        off = k - CONV_LEFT
        if off == 0:
            continue
        shifted = pltpu.roll(v, (-off) % tm, axis=0)
        valid = (pos + off >= 0) & (pos + off < blocklen)
        out = out + cw[k:k + 1] * jnp.where(valid, shifted, 0.0)
    return out


def _recin_kernel(x_ref, sh_ref, sc_ref, g_ref, wg_ref, wv_ref, cw_ref, cb_ref,
                  og_ref, ov_ref, h_ref, *, blocklen):
    n = pl.program_id(2)

    tm = h_ref.shape[0]
    rc = min(tm, REC_ROW_CHUNK)
    assert rc % blocklen == 0 and tm % rc == 0

    x2 = x_ref.at[0]
    nr = tm // rc

    def norm_rows(r, gm, shift):
        for q in range(rc // ROW_CHUNK):
            rows = slice(r * rc + q * ROW_CHUNK, r * rc + (q + 1) * ROW_CHUNK)
            inv = _inv_rms(x2[rows, :])
            h_ref[rows, :] = (x2[rows, :] * inv * gm + shift).astype(_BF16)

    def step(first):
        if not first:
            h = h_ref[...]
            for c0 in range(0, ov_ref.shape[2], REC_COL_CHUNK):
                cols = slice(c0, c0 + REC_COL_CHUNK)
                if og_ref is not None:
                    og_ref[0, :, cols] = _gelu_tanh(_dot(h, wg_ref[:, cols])).astype(_BF16)
                v = _dot(h, wv_ref[:, cols])
                ov_ref[0, :, cols] = _conv_rows(v, cw_ref[:, cols], cb_ref[:, cols],
                                                blocklen).astype(_BF16)
            return
        if first:
            gm = g_ref[...] * (1.0 + sc_ref[0])
            shift = sh_ref[0]
            norm_rows(0, gm, shift)
        for r in range(nr):
            rows = slice(r * rc, (r + 1) * rc)
            h = h_ref[rows, :]
            if og_ref is not None:
                og_ref[0, rows, :] = _gelu_tanh(_dot(h, wg_ref[...])).astype(_BF16)
            v = _dot(h, wv_ref[...])
            ov_ref[0, rows, :] = _conv_rows(v, cw_ref[...], cb_ref[...], blocklen).astype(_BF16)
            if first and r + 1 < nr:
                norm_rows(r + 1, gm, shift)

    pl.when(n == 0)(lambda: step(True))
    pl.when(n > 0)(lambda: step(False))


def _recin_nogate_kernel(x_ref, sh_ref, sc_ref, g_ref, wv_ref, cw_ref, cb_ref, ov_ref, h_ref,
                         *, blocklen):
    _recin_kernel(x_ref, sh_ref, sc_ref, g_ref, None, wv_ref, cw_ref, cb_ref, None, ov_ref,
                  h_ref, blocklen=blocklen)


def _recin(x, mods, row_of, k0, g, w_in, conv_w, conv_b, tm, blocklen, need_gate, tn=512):
    nb, L, d = x.shape
    r = w_in.shape[1] // 2
    nn = r // tn
    mod_spec = lambda k: pl.BlockSpec((1, 1, d), lambda b, i, n: (row_of(b), 0, k))
    x_spec = pl.BlockSpec((1, tm, d), lambda b, i, n: (b, i, 0))
    vec_spec = pl.BlockSpec((1, d), lambda b, i, n: (0, 0))
    wg_spec = pl.BlockSpec((d, tn), lambda b, i, n: (0, n))
    wv_spec = pl.BlockSpec((d, tn), lambda b, i, n: (0, nn + n))
    cw_spec = pl.BlockSpec((CONV_W, tn), lambda b, i, n: (0, n))
    cb_spec = pl.BlockSpec((1, tn), lambda b, i, n: (0, n))
    o_spec = pl.BlockSpec((1, tm, tn), lambda b, i, n: (b, i, n))
    o_shape = jax.ShapeDtypeStruct((nb, L, r), _BF16)
    common = dict(grid=(nb, L // tm, nn),
                  scratch_shapes=[pltpu.VMEM((tm, d), _BF16)],
                  compiler_params=_params(("arbitrary", "arbitrary", "arbitrary"), 56))
    g2 = g.reshape(1, d)
    cb2 = conv_b.reshape(1, r)
    if need_gate:
        return pl.pallas_call(
            functools.partial(_recin_kernel, blocklen=blocklen),
            in_specs=[x_spec, mod_spec(k0), mod_spec(k0 + 1), vec_spec, wg_spec, wv_spec,
                      cw_spec, cb_spec],
            out_specs=[o_spec, o_spec], out_shape=[o_shape, o_shape],
            name="rec_in", **common,
        )(x, mods, mods, g2, w_in, w_in, conv_w, cb2)
    return None, pl.pallas_call(
        functools.partial(_recin_nogate_kernel, blocklen=blocklen),
        in_specs=[x_spec, mod_spec(k0), mod_spec(k0 + 1), vec_spec, wv_spec, cw_spec, cb_spec],
        out_specs=o_spec, out_shape=o_shape,
        name="rec_in_ctx", **common,
    )(x, mods, mods, g2, w_in, conv_w, cb2)


def _gelu_tanh(x):
    c = math.sqrt(2.0 / math.pi)
    hx = 0.5 * x
    return hx + hx * jnp.tanh(x * (c + (0.044715 * c) * (x * x)))


def _scan_kernel(v_ref, vc_ref, g_ref, w_ref, lam_ref, o_ref,
                 a_sc, b_sc, hl_sc, al_sc, *, nh):
    lc = vc_ref.shape[1]
    s = v_ref.shape[1]
    nseg = V7X_SUBLANES
    seg, cseg = s // nseg, lc // nseg
    cb = s

    neg_lam = -lam_ref[...]
    sp = jnp.maximum(neg_lam, 0.0) + jnp.log(1.0 + jnp.exp(-jnp.abs(neg_lam)))
    q = (-0.5 * RG_C * LOG2E) * sp

    def gate_dot(vb, hh):
        lane = lax.broadcasted_iota(jnp.int32, vb.shape, 1)
        ones2 = jnp.where(lane < 2, 1.0, 0.0).astype(_BF16)
        return _dot(jnp.concatenate([vb, ones2], axis=1), w_ref[hh])

    def coeffs(vb, pre, hh, store):
        lanes = slice(hh * LRU_BLOCK, (hh + 1) * LRU_BLOCK)
        vh = 0.5 * vb.astype(_F32)
        for d in range(2):
            c0 = (2 * d) * LRU_BLOCK
            ta = jnp.tanh(pre[:, c0:c0 + LRU_BLOCK])
            tx = jnp.tanh(pre[:, c0 + LRU_BLOCK:c0 + 2 * LRU_BLOCK])
            qd = q[d:d + 1, lanes]
            a = jnp.exp2(qd + qd * ta)
            m2 = 1.0 - a * a
            mult = m2 * lax.rsqrt(jnp.maximum(m2, F32_TINY))
            store(d, a, (mult * vh) * (1.0 + tx))

    def lat_coeffs(m, carry):
        src = pl.ds(pl.multiple_of(m * seg, seg), seg)
        dst = pl.ds(m, seg, stride=nseg)
        for hh in range(nh):
            vb = v_ref[0, src, hh * LRU_BLOCK:(hh + 1) * LRU_BLOCK]

            def store(d, a, b, hh=hh):
                a_sc[d, hh, dst, :] = a
                b_sc[d, hh, dst, :] = b
            coeffs(vb, gate_dot(vb, hh), hh, store)
        return carry

    lax.fori_loop(0, nseg, lat_coeffs, 0)

    for hh in range(nh):
        vb = vc_ref[0, :, hh * LRU_BLOCK:(hh + 1) * LRU_BLOCK]

        def store(d, a, b, hh=hh):
            for m in range(nseg):
                dst = pl.ds(cb + m, cseg, stride=nseg)
                a_sc[d, hh, dst, :] = a[m * cseg:(m + 1) * cseg]
                b_sc[d, hh, dst, :] = b[m * cseg:(m + 1) * cseg]
        coeffs(vb, gate_dot(vb, hh), hh, store)

    vshape = (nseg, V7X_LANES)
    row = lax.broadcasted_iota(jnp.int32, vshape, 0)
    zeros = tuple(jnp.zeros(vshape, _F32) for _ in range(nh))
    ones = tuple(jnp.ones(vshape, _F32) for _ in range(nh))

    def sublane_scan(a, b, reverse):
        for k in (1, 2, 4):
            keep = (row < nseg - k) if reverse else (row >= k)
            shift = (nseg - k) if reverse else k
            a_sh = jnp.where(keep, pltpu.roll(a, shift, axis=0), 1.0)
            b_sh = jnp.where(keep, pltpu.roll(b, shift, axis=0), 0.0)
            b = a * b_sh + b
            a = a * a_sh
        return a, b

    def local_pass(base, length, save):
        def body(i, carry):
            hs, prods = carry
            new_h, new_p = [], []
            for d, k in ((0, i), (1, length - 1 - i)):
                t = pl.ds(pl.multiple_of(k * nseg, nseg), nseg)
                idx = pl.ds(pl.multiple_of(base + k * nseg, nseg), nseg)
                for hh in range(nh):
                    a = a_sc[d, hh, idx, :]
                    h = a * hs[d * nh + hh] + b_sc[d, hh, idx, :]
                    pr = prods[d * nh + hh] * a
                    if save:
                        hl_sc[d, hh, t, :] = h
                        al_sc[d, hh, t, :] = pr
                    new_h.append(h)
                    new_p.append(pr)
            return tuple(new_h), tuple(new_p)
        return lax.fori_loop(0, length, body, (zeros + zeros, ones + ones), unroll=8)

    def entry_states(h_fin, a_tot, g_in, reverse):
        acum, bcum = sublane_scan(a_tot, h_fin, reverse)
        e = acum * g_in + bcum
        if reverse:
            entry = jnp.where(row < nseg - 1, pltpu.roll(e, nseg - 1, axis=0), g_in)
            final = e[0:1, :]
        else:
            entry = jnp.where(row >= 1, pltpu.roll(e, 1, axis=0), g_in)
            final = e[nseg - 1:nseg, :]
        return entry, jnp.broadcast_to(final, vshape)

    hc, pc_tot = local_pass(cb, cseg, save=False)
    hl, pl_tot = local_pass(0, seg, save=True)
    entries = []
    for d, reverse in ((0, False), (1, True)):
        for hh in range(nh):
            n = d * nh + hh
            _, g_ctx = entry_states(hc[n], pc_tot[n], zeros[hh], reverse)
            entry, _ = entry_states(hl[n], pl_tot[n], g_ctx, reverse)
            entries.append(entry)

    def fix(k, carry):
        t = pl.ds(pl.multiple_of(k * nseg, nseg), nseg)
        for hh in range(nh):
            hl_sc[0, hh, t, :] = ((hl_sc[0, hh, t, :] + al_sc[0, hh, t, :] * entries[hh])
                                  + (hl_sc[1, hh, t, :] + al_sc[1, hh, t, :] * entries[nh + hh]))
        return carry
    lax.fori_loop(0, seg, fix, 0, unroll=8)

    def gate_out(m, carry):
        rows = pl.ds(pl.multiple_of(m * seg, seg), seg)
        for hh in range(nh):
            lanes = slice(hh * LRU_BLOCK, (hh + 1) * LRU_BLOCK)
            y = hl_sc[0, hh, pl.ds(m, seg, stride=nseg), :]
            o_ref[0, rows, lanes] = (g_ref[0, rows, lanes].astype(_F32) * y).astype(_BF16)
        return carry

    lax.fori_loop(0, nseg, gate_out, 0)


def _gate_weights(w_cat, bias):
    nh_all = w_cat.shape[0]
    hb = 0.5 * jnp.transpose(bias.reshape(4, nh_all, LRU_BLOCK), (1, 0, 2)).reshape(nh_all, 1, -1)
    hi = hb.astype(_BF16)
    lo = (hb - hi.astype(_F32)).astype(_BF16)
    pad = jnp.zeros((nh_all, LRU_BLOCK - 2, 4 * LRU_BLOCK), _BF16)
    return jnp.concatenate([(0.5 * w_cat).astype(_BF16), hi, lo, pad], axis=1)


def _scan(v_lat, v_ctx, g_lat, w_aug, lam, cc=512):
    nb, s, r = v_lat.shape
    lc = v_ctx.shape[1]
    nh = cc // LRU_BLOCK
    coef_rows = s + lc
    blk = lambda L: pl.BlockSpec((1, L, cc), lambda b, c: (b, 0, c))
    return pl.pallas_call(
        functools.partial(_scan_kernel, nh=nh),
        grid=(nb, r // cc),
        in_specs=[blk(s), blk(lc), blk(s),
                  pl.BlockSpec((nh, 2 * LRU_BLOCK, 4 * LRU_BLOCK), lambda b, c: (c, 0, 0)),
                  pl.BlockSpec((2, cc), lambda b, c: (0, c))],
        out_specs=blk(s),
        out_shape=jax.ShapeDtypeStruct((nb, s, r), _BF16),
        scratch_shapes=[pltpu.VMEM((2, nh, coef_rows, V7X_LANES), _F32),
                        pltpu.VMEM((2, nh, coef_rows, V7X_LANES), _F32),
                        pltpu.VMEM((2, nh, s, V7X_LANES), _F32),
                        pltpu.VMEM((2, nh, s, V7X_LANES), _F32)],
        compiler_params=_params(("arbitrary", "arbitrary"), 56),
        name="rglru_scan",
    )(v_lat, v_ctx, g_lat, w_aug, lam)


def _outproj_kernel(x_ref, a_ref, gt_ref, g_ref, w_ref, o_ref, *, rc):
    tm = o_ref.shape[1]
    x2 = x_ref.at[0]
    o2 = o_ref.at[0]
    gg = gt_ref[0] * g_ref[...]

    def residual(r):
        for q in range(r * rc, (r + 1) * rc, ROW_CHUNK):
            rows = slice(q, q + ROW_CHUNK)
            inv = _inv_rms(o2[rows, :])
            o2[rows, :] = x2[rows, :] + (o2[rows, :] * inv) * gg

    for r in range(tm // rc):
        rows = slice(r * rc, (r + 1) * rc)
        o2[rows, :] = _dot(a_ref[0, rows, :], w_ref[...])
        if r > 0:
            residual(r - 1)
    residual(tm // rc - 1)


def _outproj(x, a, mods, k_gate, g, w, tm=1024, rc=256):
    nb, L, d = x.shape
    r = w.shape[0]
    return pl.pallas_call(
        functools.partial(_outproj_kernel, rc=rc),
        grid=(nb, L // tm),
        in_specs=[pl.BlockSpec((1, tm, d), lambda b, i: (b, i, 0)),
                  pl.BlockSpec((1, tm, r), lambda b, i: (b, i, 0)),
                  pl.BlockSpec((1, 1, d), lambda b, i: (b, 0, k_gate)),
                  pl.BlockSpec((1, d), lambda b, i: (0, 0)),
                  pl.BlockSpec((r, d), lambda b, i: (0, 0), pipeline_mode=pl.Buffered(1))],
        out_specs=pl.BlockSpec((1, tm, d), lambda b, i: (b, i, 0)),
        out_shape=jax.ShapeDtypeStruct((nb, L, d), _F32),
        compiler_params=_params(("arbitrary", "arbitrary"), 56),
        name="mixer_out",
    )(x, a, mods, g.reshape(1, d), w)


def _dft_tables(n):
    k = np.arange(n, dtype=np.int64)
    ang = 2.0 * np.pi * ((k[:, None] * k[None, :]) % n).astype(np.float64) / n
    return np.cos(ang), np.sin(ang)


def _chdft_kernel(x_ref, sh_ref, sc_ref, g_ref, cs_ref, o_ref, h_ref):
    _norm_mod_into(x_ref.at[0], h_ref, g_ref[...], sh_ref[0], sc_ref[0])
    d = h_ref.shape[1]
    for gi in range(d // FGROUP):
        lanes = slice(gi * FGROUP, (gi + 1) * FGROUP)
        t = _dot(h_ref[:, lanes], cs_ref[...])
        o_ref[0, 0, :, lanes] = t[:, :FGROUP].astype(_BF16)
        o_ref[0, 1, :, lanes] = t[:, FGROUP:].astype(_BF16)


def _chdft(x, mods, k0, g, cs, tm=1024):
    nb, L, d = x.shape
    mod_spec = lambda k: pl.BlockSpec((1, 1, d), lambda b, i: (b, 0, k))
    return pl.pallas_call(
        _chdft_kernel,
        grid=(nb, L // tm),
        in_specs=[pl.BlockSpec((1, tm, d), lambda b, i: (b, i, 0)),
                  mod_spec(k0), mod_spec(k0 + 1),
                  pl.BlockSpec((1, d), lambda b, i: (0, 0)),
                  pl.BlockSpec((FGROUP, 2 * FGROUP), lambda b, i: (0, 0))],
        out_specs=pl.BlockSpec((1, 2, tm, d), lambda b, i: (b, 0, i, 0)),
        out_shape=jax.ShapeDtypeStruct((nb, 2, L, d), _BF16),
        scratch_shapes=[pltpu.VMEM((tm, d), _BF16)],
        compiler_params=_params(("arbitrary", "arbitrary"), 48),
        name="fourier_channel_dft",
    )(x, mods, mods, g.reshape(1, d), cs)


def _seqdft_kernel(ac_ref, as_ref, ctop_ref, r_ref, bc_ref, bs_ref, o_ref, fc_sc, fs_sc, h_sc,
                   *, scale, rc):
    half = as_ref.shape[0]
    n = 2 * half
    mb = MIRROR_BLOCK
    nblk = half // mb
    for qb in range(nblk):
        rows = slice(qb * mb, (qb + 1) * mb)
        if qb == 0:
            rm, src = r_ref[:, :mb], slice(n - mb, n)
        else:
            rm, src = r_ref[...], slice(n - (qb + 1) * mb, n - (qb - 1) * mb)
        fc_sc[rows, :] = (bc_ref[rows, :].astype(_F32) + _dot(rm, bc_ref[src, :])).astype(_BF16)
        fs_sc[rows, :] = (bs_ref[rows, :].astype(_F32) - _dot(rm, bs_ref[src, :])).astype(_BF16)
    fc_sc[half:half + mb, :] = bc_ref[half:half + mb, :]

    h_sc[half:half + mb, :] = (_dot(ctop_ref[...], fc_sc[...]) * scale).astype(_BF16)
    for r in range(half // rc):
        rows = slice(r * rc, (r + 1) * rc)
        p = _dot(ac_ref[rows, :], fc_sc[...])
        q = _dot(as_ref[rows, :], fs_sc[...])
        o_ref[0, 0, rows, :] = ((p - q) * scale).astype(_BF16)
        h_sc[rows, :] = ((p + q) * scale).astype(_BF16)
    for qb in range(nblk):
        src = h_sc[(nblk - 1 - qb) * mb:(nblk + 1 - qb) * mb, :]
        o_ref[0, 1, qb * mb:(qb + 1) * mb, :] = _dot(r_ref[...], src).astype(_BF16)


def _seqdft(xcs, scale, tn=1024, rc=1024):
    nb, _, L, d = xcs.shape
    half = L // 2
    mb = MIRROR_BLOCK
    cos_s, sin_s = _dft_tables(L)
    kc = half + mb
    cos_f = np.zeros((half + mb, kc))
    cos_f[:, :half + 1] = cos_s[:half + mb, :half + 1]
    a_cos = jnp.asarray(cos_f[:half], _BF16)
    a_sin = jnp.asarray(sin_s[:half, :half], _BF16)
    c_top = jnp.asarray(cos_f[half:], _BF16)
    mirror = np.zeros((mb, 2 * mb), np.float32)
    mirror[np.arange(mb), mb - np.arange(mb)] = 1.0
    whole = lambda rows, cols: pl.BlockSpec((rows, cols), lambda b, n: (0, 0))
    b_spec = lambda part: pl.BlockSpec((None, None, L, tn), lambda b, n: (b, part, 0, n))
    return pl.pallas_call(
        functools.partial(_seqdft_kernel, scale=scale, rc=rc),
        grid=(nb, d // tn),
        in_specs=[whole(half, kc), whole(half, half), whole(mb, kc), whole(mb, 2 * mb),
                  b_spec(0), b_spec(1)],
        out_specs=pl.BlockSpec((1, 2, half, tn), lambda b, n: (b, 0, 0, n)),
        out_shape=jax.ShapeDtypeStruct((nb, 2, half, d), _BF16),
        scratch_shapes=[pltpu.VMEM((kc, tn), _BF16), pltpu.VMEM((half, tn), _BF16),
                        pltpu.VMEM((half + mb, tn), _BF16)],
        compiler_params=_params(("arbitrary", "arbitrary"), 48),
        name="fourier_seq_dft",
    )(a_cos, a_sin, c_top, jnp.asarray(mirror, _BF16), xcs, xcs)


def kernel(x, c, ctx, c_ctx, mod_w, mod_b, norm_g, ffn_w_in, ffn_w_out, rec_w_in, rec_conv_w,
           rec_conv_b, rec_gate_w, rec_gate_b, rec_lam, rec_w_out, fou_w_out):
    nb, s, d = x.shape
    lc = ctx.shape[1]
    depth = mod_w.shape[0]
    n_mixers = 2
    last_rec = ((depth - 1) // n_mixers) * n_mixers
    assert last_rec == 0, "context stream is only carried up to the first RG-LRU mixer"

    rows = 2 * V7X_SUBLANES
    cc = jnp.concatenate([c, c_ctx[None], jnp.zeros((rows - nb - 1, d), _F32)], axis=0)
    mods_all = _adaln(cc, mod_w, mod_b).reshape(depth, rows, 1, N_MOD * d)

    w_ffn = (ffn_w_in[0, 0].astype(_BF16), ffn_w_out[0, 0].astype(_BF16))
    order = [(i, sub) for i in range(depth) for sub in range(2)]

    def ffn_lat(x, mods, k0, g_pre, g_post, layer, sub):
        nonlocal w_ffn
        nxt = order.index((layer, sub)) + 1
        next_w = (ffn_w_in, ffn_w_out) + order[nxt] if nxt < len(order) else None
        out = _ffn(x, mods, lat_row, k0, g_pre, g_post, *w_ffn, next_w=next_w)
        if next_w is None:
            return out
        w_ffn = (out[1], out[2])
        return out[0]

    lat_row = lambda b: b
    ctx_row = lambda b: nb

    for i in range(depth):
        is_rec = (i % n_mixers) == 0
        j = i // n_mixers
        mods = mods_all[i]
        g = norm_g[i]
        w_pre = w_ffn
        x = ffn_lat(x, mods, 0, g[0], g[1], i, 0)
        if is_rec:
            ctx1 = _ffn(ctx.reshape(1, nb * lc, d), mods, ctx_row, 0, g[0], g[1], *w_pre)
            w_rec = rec_w_in[j].astype(_BF16)
            g_lat, v_lat = _recin(x, mods, lat_row, 3, g[2], w_rec, rec_conv_w[j], rec_conv_b[j],
                                  tm=1024, blocklen=GRID_W, need_gate=True)
            _, v_ctx = _recin(ctx1, mods, ctx_row, 3, g[2], w_rec, rec_conv_w[j], rec_conv_b[j],
                              tm=1024, blocklen=lc, need_gate=False)
            v_ctx = v_ctx.reshape(nb, lc, -1)
            gw = rec_gate_w[j]
            nh = gw.shape[2]
            w_cat = jnp.transpose(gw, (2, 3, 0, 1, 4)).reshape(nh, LRU_BLOCK, 4 * LRU_BLOCK)
            a = _scan(v_lat, v_ctx, g_lat,
                      _gate_weights(w_cat, rec_gate_b[j].reshape(4, -1)), rec_lam[j])
            w_mix = rec_w_out[j].astype(_BF16)
        else:
            cos_c, sin_c = _dft_tables(FGROUP)
            cs = jnp.asarray(np.concatenate([cos_c, sin_c], axis=1), _BF16)
            xcs = _chdft(x, mods, 3, g[2], cs)
            a = _seqdft(xcs, 1.0 / math.sqrt(s * FGROUP)).reshape(nb, s, d)
            w_mix = fou_w_out[j].astype(_BF16)
        x = _outproj(x, a, mods, 5, g[3], w_mix)
        x = ffn_lat(x, mods, 6, g[4], g[5], i, 1)
    return x
```
